```python
import jax, jax.numpy as jnp
from jax import lax
import numpy as np

D_MODEL = 1024
BATCH = 4
SEQ = 8192
DEPTH = 1

D_MIX = D_MODEL
D_SGU = D_MIX // 2
SGU_HEADS = 8
SGU_HEAD_DIM = D_SGU // SGU_HEADS
SGU_CHUNK = 128
D_GLA = D_MIX - D_SGU
GLA_HEADS = 4
GLA_DK = D_GLA // 2
GLA_HEAD_K = GLA_DK // GLA_HEADS
GLA_HEAD_V = D_GLA // GLA_HEADS
GLA_GATE_RANK = 16
GLA_TAU = 16.0
GLA_CHUNK = 64
D_IN = 2 * D_SGU + 2 * GLA_DK + 2 * D_GLA + GLA_GATE_RANK
N_EXPERTS = 32
TOP_K = 4
D_EXPERT = D_MODEL
SWIGLU_LIMIT = 7.0
SWIGLU_ALPHA = 1.702
MOE_BLOCK = 128
EPS = 1e-6

kernel_name = "hybrid_sgu_gla_moe_block"


def rms_norm(x, g):
    xf = x.astype(jnp.float32)
    y = xf * lax.rsqrt(jnp.mean(xf * xf, axis=-1, keepdims=True) + EPS)
    return (y * g.astype(jnp.float32)).astype(x.dtype)


def spatial_gating(u, v, g_v, w_s, b_s):
    B, S, _ = u.shape
    N = S // SGU_CHUNK
    vh = rms_norm(v.reshape(B, S, SGU_HEADS, SGU_HEAD_DIM), g_v.reshape(SGU_HEADS, SGU_HEAD_DIM))
    vc = vh.reshape(B, N, SGU_CHUNK, SGU_HEADS, SGU_HEAD_DIM)
    w_causal = jnp.tril(w_s)
    sv = jnp.einsum('hts,bnshd->bnthd', w_causal, vc) + b_s.T[None, None, :, :, None]
    return (u.reshape(B, N, SGU_CHUNK, SGU_HEADS, SGU_HEAD_DIM) * sv).reshape(B, S, D_SGU)


def gated_linear_attention(q, k, v, log_a):
    B, S, H, dk = q.shape
    dv = v.shape[-1]
    N = S // GLA_CHUNK
    dt = q.dtype

    def chunk(t):
        return t.reshape(B, N, GLA_CHUNK, H, t.shape[-1]).transpose(0, 3, 1, 2, 4)

    qc, kc, vc = chunk(q) * (GLA_HEAD_K ** -0.5), chunk(k), chunk(v)
    bcum = jnp.cumsum(chunk(log_a).astype(jnp.float32), axis=3)
    blast = bcum[:, :, :, -1:, :]
    q_s = qc * jnp.exp(bcum).astype(dt)
    k_s = kc * jnp.exp(-bcum).astype(dt)
    k_dec = kc * jnp.exp(blast - bcum).astype(dt)
    scores = jnp.einsum('bhnid,bhnjd->bhnij', q_s, k_s)
    causal = jnp.tril(jnp.ones((GLA_CHUNK, GLA_CHUNK), dtype=bool))
    scores = jnp.where(causal, scores, jnp.zeros((), dt))
    intra = jnp.einsum('bhnij,bhnjv->bhniv', scores, vc)
    d_state = jnp.einsum('bhncd,bhncv->bhndv', k_dec, vc)
    decay = jnp.exp(blast[:, :, :, 0, :]).astype(d_state.dtype)

    def step(state, inp):
        dn, dsn = inp
        return dn[..., None] * state + dsn, state

    init = jnp.zeros((B, H, dk, dv), d_state.dtype)
    _, prev = lax.scan(step, init, (jnp.moveaxis(decay, 2, 0), jnp.moveaxis(d_state, 2, 0)))
    prev = jnp.moveaxis(prev, 0, 2)
    inter = jnp.einsum('bhncd,bhndv->bhncv', q_s, prev)
    o = intra + inter
    return o.transpose(0, 2, 3, 1, 4).reshape(B, S, H, dv)


def moe_ffn(xn, w_router, b_router, w_gate, b_gate, w_up, b_up, w_down, b_down):
    B, S, D = xn.shape
    T = B * S
    xt = xn.reshape(T, D)
    logits = (xt @ w_router + b_router).astype(jnp.float32)
    top_v, top_i = lax.top_k(logits, TOP_K)
    probs = jax.nn.softmax(top_v, axis=-1).astype(xn.dtype)
    TK = T * TOP_K
    flat_e = top_i.reshape(TK)
    flat_tok = jnp.repeat(jnp.arange(T, dtype=jnp.int32), TOP_K)
    flat_w = probs.reshape(TK)
    order = jnp.argsort(flat_e, stable=True)
    sorted_e = flat_e[order]
    counts = jnp.bincount(flat_e, length=N_EXPERTS)
    start = jnp.cumsum(counts) - counts
    padded = ((counts + MOE_BLOCK - 1) // MOE_BLOCK) * MOE_BLOCK
    pend = jnp.cumsum(padded)
    pstart = pend - padded
    dest = pstart[sorted_e] + (jnp.arange(TK, dtype=jnp.int32) - start[sorted_e])
    n_blocks = -(-TK // MOE_BLOCK) + N_EXPERTS
    P = n_blocks * MOE_BLOCK
    tok_buf = jnp.zeros((P,), jnp.int32).at[dest].set(flat_tok[order])
    w_buf = jnp.zeros((P,), xn.dtype).at[dest].set(flat_w[order])
    block_e = jnp.clip(jnp.searchsorted(pend, jnp.arange(n_blocks) * MOE_BLOCK, side='right'), 0, N_EXPERTS - 1)
    x_blocks = xt[tok_buf].reshape(n_blocks, MOE_BLOCK, D)

    def expert_block(args):
        xb, e = args
        g = jnp.minimum(xb @ w_gate[e] + b_gate[e], SWIGLU_LIMIT)
        u = jnp.clip(xb @ w_up[e] + b_up[e], -SWIGLU_LIMIT, SWIGLU_LIMIT)
        hdn = (u + 1.0) * (g * jax.nn.sigmoid(SWIGLU_ALPHA * g))
        return hdn @ w_down[e] + b_down[e]

    out = lax.map(expert_block, (x_blocks, block_e)).reshape(P, D)
    y = jnp.zeros((T, D), out.dtype).at[tok_buf].add(out * w_buf[:, None])
    return y.reshape(B, S, D)


def setup_inputs(seed: int = 0) -> dict:
    key = jax.random.key(seed)
    ks = jax.random.split(key, 26)
    f32 = jnp.float32
    L = DEPTH

    def nrm(k, shape, scale):
        return jax.random.normal(k, shape, f32) * scale

    def gain(k, shape):
        return 1.0 + 0.05 * jax.random.normal(k, shape, f32)

    return {
        "x": jax.random.normal(ks[0], (BATCH, SEQ, D_MODEL), f32),
        "c": jax.random.normal(ks[1], (BATCH, D_MODEL), f32),
        "w_ada": nrm(ks[2], (L, D_MODEL, 6 * D_MODEL), 0.5 * D_MODEL ** -0.5),
        "b_ada": nrm(ks[3], (L, 6 * D_MODEL), 0.02),
        "g_pre_mix": gain(ks[4], (L, D_MODEL)),
        "w_in": nrm(ks[5], (L, D_MODEL, D_IN), D_MODEL ** -0.5),
        "g_sgu_v": gain(ks[6], (L, D_SGU)),
        "w_s": nrm(ks[7], (L, SGU_HEADS, SGU_CHUNK, SGU_CHUNK), SGU_CHUNK ** -0.5),
        "b_s": gain(ks[8], (L, SGU_HEADS, SGU_CHUNK)),
        "w_gk2": nrm(ks[9], (L, GLA_GATE_RANK, GLA_DK), GLA_GATE_RANK ** -0.5),
        "b_gk": nrm(ks[10], (L, GLA_DK), 0.1),
        "g_gla_out": gain(ks[11], (L, GLA_HEAD_V)),
        "w_out": nrm(ks[12], (L, D_MIX, D_MODEL), D_MIX ** -0.5),
        "g_post_mix": gain(ks[13], (L, D_MODEL)),
        "g_pre_ffn": gain(ks[14], (L, D_MODEL)),
        "w_router": nrm(ks[15], (L, D_MODEL, N_EXPERTS), D_MODEL ** -0.5),
        "b_router": nrm(ks[16], (L, N_EXPERTS), 0.01),
        "w_gate": nrm(ks[17], (L, N_EXPERTS, D_MODEL, D_EXPERT), D_MODEL ** -0.5),
        "b_gate": nrm(ks[18], (L, N_EXPERTS, D_EXPERT), 0.01),
        "w_up": nrm(ks[19], (L, N_EXPERTS, D_MODEL, D_EXPERT), D_MODEL ** -0.5),
        "b_up": nrm(ks[20], (L, N_EXPERTS, D_EXPERT), 0.01),
        "w_down": nrm(ks[21], (L, N_EXPERTS, D_EXPERT, D_MODEL), D_EXPERT ** -0.5),
        "b_down": nrm(ks[22], (L, N_EXPERTS, D_MODEL), 0.01),
        "g_post_ffn": gain(ks[23], (L, D_MODEL)),
    }


def reference(x, c, w_ada, b_ada, g_pre_mix, w_in, g_sgu_v, w_s, b_s, w_gk2, b_gk, g_gla_out, w_out,
              g_post_mix, g_pre_ffn, w_router, b_router, w_gate, b_gate, w_up, b_up, w_down, b_down, g_post_ffn):
    B, S, _ = x.shape
    split_at = list(np.cumsum([D_SGU, D_SGU, GLA_DK, GLA_DK, D_GLA, D_GLA]))
    h = x
    for l in range(DEPTH):
        mods = jax.nn.silu(c) @ w_ada[l] + b_ada[l]
        shift1, scale1, gate1, shift2, scale2, gate2 = [m[:, None, :] for m in jnp.split(mods, 6, axis=-1)]

        n = rms_norm(h, g_pre_mix[l]) * (1.0 + scale1) + shift1
        proj = n @ w_in[l]
        u_s, v_s, q, k, v, r, g_low = jnp.split(proj, split_at, axis=-1)
        a_out = spatial_gating(jax.nn.gelu(u_s), jax.nn.gelu(v_s), g_sgu_v[l], w_s[l], b_s[l])
        log_a = jax.nn.log_sigmoid((g_low @ w_gk2[l] + b_gk[l]).astype(jnp.float32)) / GLA_TAU
        o = gated_linear_attention(q.reshape(B, S, GLA_HEADS, GLA_HEAD_K),
                                   k.reshape(B, S, GLA_HEADS, GLA_HEAD_K),
                                   v.reshape(B, S, GLA_HEADS, GLA_HEAD_V),
                                   log_a.reshape(B, S, GLA_HEADS, GLA_HEAD_K))
        o = rms_norm(o, g_gla_out[l]).reshape(B, S, D_GLA) * jax.nn.silu(r)
        mix = jnp.concatenate([a_out, o], axis=-1) @ w_out[l]
        h = h + gate1 * rms_norm(mix, g_post_mix[l])

        n2 = rms_norm(h, g_pre_ffn[l]) * (1.0 + scale2) + shift2
        f = moe_ffn(n2, w_router[l], b_router[l], w_gate[l], b_gate[l], w_up[l], b_up[l], w_down[l], b_down[l])
        h = h + gate2 * rms_norm(f, g_post_ffn[l])
    return h
```

```python
import functools

import jax
import jax.numpy as jnp
from jax import lax
from jax.experimental import pallas as pl
from jax.experimental.pallas import tpu as pltpu
from jax.experimental.pallas import tpu_sc as plsc

F32 = jnp.float32
BF16 = jnp.bfloat16

D_MODEL = 1024
D_SGU = 512
SGU_HEADS = 8
SGU_HEAD_DIM = 64
SGU_CHUNK = 128
D_GLA = 512
GLA_HEADS = 4
GLA_DK = 256
GLA_HEAD_K = 64
GLA_HEAD_V = 128
GLA_GATE_RANK = 16
GLA_TAU = 16.0
GLA_CHUNK = 64
N_EXPERTS = 32
TOP_K = 4
SWIGLU_LIMIT = 7.0
SWIGLU_ALPHA = 1.702
EPS = 1e-6

LANES = 128
D_IN_MAIN = 2 * D_SGU + 2 * GLA_DK + 2 * D_GLA
D_IN_PAD = D_IN_MAIN + LANES

MIX_TM = 256
MOE_BM = 512
CMB_TM = 256
SC_CORES = 2
SC_SUBCORES = 16
SC_WORKERS = SC_CORES * SC_SUBCORES
SC_CH = 64
NEG_BIG = -1e30

VMEM_LIMIT = 56 * 1024 * 1024


def _ada_kernel(c_ref, w_ref, b_ref, o_ref):
    c = c_ref[...]
    sc = c * jax.nn.sigmoid(c)
    o_ref[...] = jnp.dot(sc.astype(BF16), w_ref[...].astype(BF16), preferred_element_type=F32) + b_ref[...]


def _ada(c_pad, w_ada, b_ada):
    rows, d = c_pad.shape
    n = w_ada.shape[1]
    tn = 1536
    return pl.pallas_call(
        _ada_kernel,
        grid=(n // tn,),
        in_specs=[pl.BlockSpec((rows, d), lambda j: (0, 0)),
                  pl.BlockSpec((d, tn), lambda j: (0, j)),
                  pl.BlockSpec((1, tn), lambda j: (0, j))],
        out_specs=pl.BlockSpec((rows, tn), lambda j: (0, j)),
        out_shape=jax.ShapeDtypeStruct((rows, n), F32),
        compiler_params=pltpu.CompilerParams(dimension_semantics=("arbitrary",), vmem_limit_bytes=VMEM_LIMIT),
        name="ada",
    )(c_pad, w_ada, b_ada)


def _rms(x):
    return x * lax.rsqrt(jnp.mean(x * x, axis=-1, keepdims=True) + EPS)


def _gelu_tanh(x):
    return 0.5 * x * (1.0 + jnp.tanh(0.7978845608028654 * (x + 0.044715 * (x * x * x))))


def _log_sigmoid(z):
    return jnp.minimum(z, 0.0) - jnp.log(1.0 + jnp.exp(-jnp.abs(z)))


def _dot(a, b):
    return jnp.dot(a, b, preferred_element_type=F32)


def _dot_nt(a, b):
    return lax.dot_general(a, b, (((1,), (1,)), ((), ())), preferred_element_type=F32)


def _dot_tn(a, b):
    return lax.dot_general(a, b, (((0,), (0,)), ((), ())), preferred_element_type=F32)


def _split_dot(l_bf, a):
    hi = a.astype(BF16)
    lo = (a - hi.astype(F32)).astype(BF16)
    return _dot(l_bf, hi) + _dot(l_bf, lo)


def _mixer_kernel(x_ref, mods_ref, g_pre_ref, w_in_ref, gmat_ref, g_sgu_ref, ws_ref, bs_ref, wgk_ref, bgk_ref,
                  g_gla_ref, w_out_ref, g_post_ref, g_ffn_ref, w_r_ref, b_r_ref,
                  h1_ref, n2_ref, topi_ref, probs_ref, rank_ref, cnt_ref,
                  st_ref, carry_ref):
    tm = x_ref.shape[1]
    b = pl.program_id(0)
    j = pl.program_id(1)

    @pl.when(j == 0)
    def _():
        st_ref[...] = jnp.zeros_like(st_ref)

    @pl.when((b == 0) & (j == 0))
    def _():
        carry_ref[...] = jnp.zeros_like(carry_ref)

    mods = mods_ref[0]
    shift1, scale1, gate1 = mods[0:1], mods[1:2], mods[2:3]
    shift2, scale2 = mods[3:4], mods[4:5]

    x = x_ref[0]
    n = _rms(x) * (g_pre_ref[...] * (1.0 + scale1)) + shift1
    nb = n.astype(BF16)

    u = _gelu_tanh(_dot(nb, w_in_ref[:, 0:D_SGU]))
    v = _gelu_tanh(_dot(nb, w_in_ref[:, D_SGU:2 * D_SGU]))
    msv = _dot((v * v).astype(BF16), gmat_ref[...])
    vh = v * lax.rsqrt(msv + EPS) * g_sgu_ref[...]

    n_sc = tm // SGU_CHUNK
    lane_c = lax.broadcasted_iota(jnp.int32, (SGU_CHUNK, LANES), 1)
    low_half = lane_c < SGU_HEAD_DIM
    w_row = lax.broadcasted_iota(jnp.int32, (SGU_CHUNK, 2 * SGU_CHUNK), 0)
    w_col = lax.broadcasted_iota(jnp.int32, (SGU_CHUNK, 2 * SGU_CHUNK), 1) & (SGU_CHUNK - 1)
    a_cols = []
    for p in range(SGU_HEADS // 2):
        wcat = jnp.where(w_row >= w_col, ws_ref[p], 0.0).astype(BF16)
        rhs = []
        for ci in range(n_sc):
            vp = vh[ci * SGU_CHUNK:(ci + 1) * SGU_CHUNK, p * LANES:(p + 1) * LANES]
            rhs.append(jnp.concatenate([jnp.where(low_half, vp, 0.0), jnp.where(low_half, 0.0, vp)],
                                       axis=0).astype(BF16))
        res = _dot(wcat, jnp.concatenate(rhs, axis=1))
        bias = bs_ref[:, p * LANES:(p + 1) * LANES]
        a_cols.append(jnp.concatenate(
            [res[:, ci * LANES:(ci + 1) * LANES] + bias for ci in range(n_sc)], axis=0))
    a_out = u * jnp.concatenate(a_cols, axis=1)

    qk = _dot(nb, w_in_ref[:, 2 * D_SGU:2 * D_SGU + 2 * GLA_DK])
    q = qk[:, :GLA_DK] * (GLA_HEAD_K ** -0.5)
    k = qk[:, GLA_DK:]
    vv = _dot(nb, w_in_ref[:, 2 * D_SGU + 2 * GLA_DK:2 * D_SGU + 2 * GLA_DK + D_GLA]).astype(BF16)
    r = _dot(nb, w_in_ref[:, D_IN_MAIN - D_GLA:D_IN_MAIN])
    g_low = _dot(nb, w_in_ref[:, D_IN_MAIN:D_IN_PAD])
    z = _dot(g_low.astype(BF16), wgk_ref[...]) + bgk_ref[...]
    log_a = _log_sigmoid(z) * (1.0 / GLA_TAU)

    t_row = lax.broadcasted_iota(jnp.int32, (tm, tm), 0)
    t_col = lax.broadcasted_iota(jnp.int32, (tm, tm), 1)
    same_chunk = (t_row >> 6) == (t_col >> 6)
    l_cum = jnp.where(same_chunk & (t_row >= t_col), 1.0, 0.0).astype(BF16)
    l_all = jnp.where(same_chunk, 1.0, 0.0).astype(BF16)
    bcum = _split_dot(l_cum, log_a)
    blast = _split_dot(l_all, log_a)
    q_s = q * jnp.exp(bcum)
    k_s = (k * jnp.exp(-bcum)).astype(BF16)
    k_dec = (k * jnp.exp(blast - bcum)).astype(BF16)
    decay = jnp.exp(blast)

    lane_g = lax.broadcasted_iota(jnp.int32, (GLA_CHUNK, LANES), 1)
    causal = (lax.broadcasted_iota(jnp.int32, (GLA_CHUNK, GLA_CHUNK), 0)
              >= lax.broadcasted_iota(jnp.int32, (GLA_CHUNK, GLA_CHUNK), 1))
    g_gla = g_gla_ref[...]
    o_rows = []
    for ci in range(tm // GLA_CHUNK):
        rs = slice(ci * GLA_CHUNK, (ci + 1) * GLA_CHUNK)
        o_heads = []
        for h in range(GLA_HEADS):
            pl_ = slice((h // 2) * LANES, (h // 2 + 1) * LANES)
            own = (lane_g >= GLA_HEAD_K) if (h % 2) else (lane_g < GLA_HEAD_K)
            qm = jnp.where(own, q_s[rs, pl_], 0.0).astype(BF16)
            v_h = vv[rs, h * GLA_HEAD_V:(h + 1) * GLA_HEAD_V]
            scores = jnp.where(causal, _dot_nt(qm, k_s[rs, pl_]), 0.0)
            intra = _dot(scores.astype(BF16), v_h)
            st = st_ref[h]
            inter = _dot_nt(qm, st.astype(BF16))
            st_ref[h] = st * decay[ci * GLA_CHUNK:ci * GLA_CHUNK + 1, pl_] + _dot_tn(v_h, k_dec[rs, pl_])
            o_heads.append(_rms(intra + inter) * g_gla)
        o_rows.append(jnp.concatenate(o_heads, axis=1))
    o = jnp.concatenate(o_rows, axis=0) * (r * jax.nn.sigmoid(r))

    mix = _dot(jnp.concatenate([a_out, o], axis=1).astype(BF16), w_out_ref[...])
    h1 = x + gate1 * (_rms(mix) * g_post_ref[...])
    h1_ref[0] = h1

    n2 = _rms(h1) * (g_ffn_ref[...] * (1.0 + scale2)) + shift2
    n2_ref[...] = n2
    logits = _dot(n2.astype(BF16), w_r_ref[...]) + b_r_ref[...]

    lane = lax.broadcasted_iota(jnp.int32, (tm, LANES), 1)
    lane_f = lane.astype(F32)
    vals = logits
    sels, tops, idxs = [], [], []
    for _ in range(TOP_K):
        m = jnp.max(vals, axis=-1, keepdims=True)
        idx = jnp.min(jnp.where(vals == m, lane_f, float(LANES)), axis=-1, keepdims=True)
        sel = lane_f == idx
        sels.append(sel)
        tops.append(m)
        idxs.append(idx)
        vals = jnp.where(sel, -jnp.inf, vals)
    es = [jnp.exp(t - tops[0]) for t in tops]
    inv = 1.0 / (es[0] + es[1] + es[2] + es[3])

    multi = jnp.zeros((tm, LANES), F32)
    for sel in sels:
        multi = multi + jnp.where(sel, 1.0, 0.0)
    l_strict = jnp.where(t_row > t_col, 1.0, 0.0).astype(BF16)
    before = _dot(l_strict, multi.astype(BF16)) + carry_ref[...]
    carry_new = carry_ref[...] + jnp.sum(multi, axis=0, keepdims=True)
    carry_ref[...] = carry_new
    cnt_ref[...] = jnp.broadcast_to(carry_new, cnt_ref.shape)

    topi = jnp.zeros((tm, LANES), F32)
    probs = jnp.zeros((tm, LANES), F32)
    rank = jnp.zeros((tm, LANES), F32)
    for kk in range(TOP_K):
        here = lane == kk
        topi = jnp.where(here, idxs[kk], topi)
        probs = jnp.where(here, es[kk] * inv, probs)
        rank = jnp.where(here, jnp.sum(jnp.where(sels[kk], before, 0.0), axis=-1, keepdims=True), rank)
    topi_ref[...] = topi.astype(jnp.int32)
    probs_ref[...] = probs
    rank_ref[...] = rank.astype(jnp.int32)


def _const_spec(shape):
    return pl.BlockSpec(shape, lambda b, j: (0,) * len(shape))


def _mixer(x, mods, g_pre, w_in_p, gmat, g_sgu, ws_cat, bs_exp, wgk_p, bgk, g_gla, w_out_b, g_post, g_ffn, w_r_p, b_r_p):
    B, S, D = x.shape
    T = B * S
    tm = MIX_TM
    nj = S // tm
    tok_spec = pl.BlockSpec((tm, LANES), lambda b, j: (b * nj + j, 0))
    return pl.pallas_call(
        _mixer_kernel,
        grid=(B, nj),
        in_specs=[pl.BlockSpec((1, tm, D), lambda b, j: (b, j, 0)),
                  pl.BlockSpec((1, 6, D), lambda b, j: (b, 0, 0)),
                  _const_spec((1, D)), _const_spec(w_in_p.shape), _const_spec(gmat.shape), _const_spec((1, D_SGU)),
                  _const_spec(ws_cat.shape), _const_spec(bs_exp.shape), _const_spec(wgk_p.shape),
                  _const_spec((1, GLA_DK)), _const_spec((1, GLA_HEAD_V)), _const_spec(w_out_b.shape),
                  _const_spec((1, D)), _const_spec((1, D)), _const_spec(w_r_p.shape), _const_spec((1, LANES))],
        out_specs=[pl.BlockSpec((1, tm, D), lambda b, j: (b, j, 0)),
                   pl.BlockSpec((tm, D), lambda b, j: (b * nj + j, 0)),
                   tok_spec, tok_spec, tok_spec,
                   pl.BlockSpec((8, LANES), lambda b, j: (0, 0))],
        out_shape=[jax.ShapeDtypeStruct((B, S, D), F32),
                   jax.ShapeDtypeStruct((T, D), F32),
                   jax.ShapeDtypeStruct((T, LANES), jnp.int32),
                   jax.ShapeDtypeStruct((T, LANES), F32),
                   jax.ShapeDtypeStruct((T, LANES), jnp.int32),
                   jax.ShapeDtypeStruct((8, LANES), F32)],
        scratch_shapes=[pltpu.VMEM((GLA_HEADS, GLA_HEAD_V, LANES), F32),
                        pltpu.VMEM((1, LANES), F32)],
        compiler_params=pltpu.CompilerParams(dimension_semantics=("arbitrary", "arbitrary"),
                                             vmem_limit_bytes=VMEM_LIMIT),
        name="mixer",
    )(x, mods, g_pre, w_in_p, gmat, g_sgu, ws_cat, bs_exp, wgk_p, bgk, g_gla, w_out_b, g_post, g_ffn, w_r_p, b_r_p)


def _sc_dispatch(rows, dest3, n_out):
    T, D = rows.shape
    n_chunks, top_k, ch = dest3.shape
    cpw = n_chunks // SC_WORKERS
    mesh = plsc.VectorSubcoreMesh(core_axis_name="c", subcore_axis_name="s")

    @functools.partial(
        pl.kernel, mesh=mesh,
        out_type=jax.ShapeDtypeStruct((n_out, D), rows.dtype),
        scratch_types=[pltpu.VMEM((top_k, ch), jnp.int32), pltpu.VMEM((ch, D), rows.dtype)],
    )
    def k(rows_hbm, dest_hbm, out_hbm, idx_v, rows_v):
        wid = lax.axis_index("s") * SC_CORES + lax.axis_index("c")

        @pl.loop(0, cpw)
        def _(i):
            c = wid * cpw + i
            pltpu.sync_copy(dest_hbm.at[c], idx_v)
            pltpu.sync_copy(rows_hbm.at[pl.ds(c * ch, ch)], rows_v)
            for kk in range(top_k):
                pltpu.sync_copy(rows_v, out_hbm.at[idx_v.at[kk]])

    return k(rows, dest3)


def _sc_gather(src, idx2):
    _, D = src.shape
    n_chunks, ch = idx2.shape
    cpw = n_chunks // SC_WORKERS
    mesh = plsc.VectorSubcoreMesh(core_axis_name="c", subcore_axis_name="s")

    @functools.partial(
        pl.kernel, mesh=mesh,
        out_type=jax.ShapeDtypeStruct((n_chunks * ch, D), src.dtype),
        scratch_types=[pltpu.VMEM((ch,), jnp.int32), pltpu.VMEM((ch, D), src.dtype)],
    )
    def k(src_hbm, idx_hbm, out_hbm, idx_v, rows_v):
        wid = lax.axis_index("s") * SC_CORES + lax.axis_index("c")

        @pl.loop(0, cpw)
        def _(i):
            c = wid * cpw + i
            pltpu.sync_copy(idx_hbm.at[c], idx_v)
            pltpu.sync_copy(src_hbm.at[idx_v], rows_v)
            pltpu.sync_copy(rows_v, out_hbm.at[pl.ds(c * ch, ch)])

    return k(src, idx2)


def _expert_kernel(be_ref, nu_ref, x_ref, wg_ref, bg_ref, wu_ref, bu_ref, wd_ref, bd_ref, o_ref, wbf_ref):
    i = pl.program_id(0)
    prev = be_ref[jnp.maximum(i - 1, 0)]
    active = i < nu_ref[0]

    @pl.when(active & ((i == 0) | (be_ref[i] != prev)))
    def _():
        wbf_ref[0] = wg_ref[0].astype(BF16)
        wbf_ref[1] = wu_ref[0].astype(BF16)
        wbf_ref[2] = wd_ref[0].astype(BF16)

    @pl.when(active)
    def _():
        xb = x_ref[...].astype(BF16)
        g = jnp.minimum(_dot(xb, wbf_ref[0]) + bg_ref[0], SWIGLU_LIMIT)
        u = jnp.clip(_dot(xb, wbf_ref[1]) + bu_ref[0], -SWIGLU_LIMIT, SWIGLU_LIMIT)
        hdn = (u + 1.0) * (g * jax.nn.sigmoid(SWIGLU_ALPHA * g))
        o_ref[...] = _dot(hdn.astype(BF16), wbf_ref[2]) + bd_ref[0]


def _experts(block_e, n_used, xs, w_gate, b_gate, w_up, b_up, w_down, b_down):
    P, D = xs.shape
    bm = MOE_BM
    n_blocks = P // bm
    E, _, DE = w_gate.shape

    def row_map(i, be, nu):
        return (jnp.minimum(i, nu[0] - 1), 0)

    def w_map(i, be, nu):
        return (be[jnp.minimum(i, nu[0] - 1)], 0, 0)

    grid_spec = pltpu.PrefetchScalarGridSpec(
        num_scalar_prefetch=2,
        grid=(n_blocks,),
        in_specs=[pl.BlockSpec((bm, D), row_map),
                  pl.BlockSpec((1, D, DE), w_map), pl.BlockSpec((1, 1, DE), w_map),
                  pl.BlockSpec((1, D, DE), w_map), pl.BlockSpec((1, 1, DE), w_map),
                  pl.BlockSpec((1, DE, D), w_map), pl.BlockSpec((1, 1, D), w_map)],
        out_specs=pl.BlockSpec((bm, D), row_map),
        scratch_shapes=[pltpu.VMEM((3, D, DE), BF16)],
    )
    return pl.pallas_call(
        _expert_kernel,
        grid_spec=grid_spec,
        out_shape=jax.ShapeDtypeStruct((P, D), F32),
        compiler_params=pltpu.CompilerParams(dimension_semantics=("arbitrary",), vmem_limit_bytes=VMEM_LIMIT),
        name="experts",
    )(block_e, n_used, xs, w_gate, b_gate.reshape(E, 1, DE), w_up, b_up.reshape(E, 1, DE),
      w_down, b_down.reshape(E, 1, D))


def _combine_kernel(h1_ref, og_ref, probs_ref, mods_ref, g_ref, o_ref):
    tm = h1_ref.shape[1]
    probs = probs_ref[...]
    lane = lax.broadcasted_iota(jnp.int32, (tm, LANES), 1)
    f = jnp.zeros((tm, h1_ref.shape[2]), F32)
    for kk in range(TOP_K):
        pk = jnp.sum(jnp.where(lane == kk, probs, 0.0), axis=-1, keepdims=True)
        f = f + pk * og_ref[kk]
    gate2 = mods_ref[0][5:6]
    o_ref[0] = h1_ref[0] + gate2 * (_rms(f) * g_ref[...])


def _combine(h1, og, probs, mods, g_post_ffn):
    B, S, D = h1.shape
    tm = CMB_TM
    nj = S // tm
    return pl.pallas_call(
        _combine_kernel,
        grid=(B, nj),
        in_specs=[pl.BlockSpec((1, tm, D), lambda b, j: (b, j, 0)),
                  pl.BlockSpec((TOP_K, tm, D), lambda b, j: (0, b * nj + j, 0)),
                  pl.BlockSpec((tm, LANES), lambda b, j: (b * nj + j, 0)),
                  pl.BlockSpec((1, 6, D), lambda b, j: (b, 0, 0)),
                  pl.BlockSpec((1, D), lambda b, j: (0, 0))],
        out_specs=pl.BlockSpec((1, tm, D), lambda b, j: (b, j, 0)),
        out_shape=jax.ShapeDtypeStruct((B, S, D), F32),
        compiler_params=pltpu.CompilerParams(dimension_semantics=("arbitrary", "arbitrary"),
                                             vmem_limit_bytes=VMEM_LIMIT),
        name="combine",
    )(h1, og, probs, mods, g_post_ffn)


def _layer(h, c_pad, w_ada, b_ada, g_pre_mix, w_in, g_sgu_v, w_s, b_s, w_gk2, b_gk, g_gla_out, w_out, g_post_mix,
           g_pre_ffn, w_router, b_router, w_gate, b_gate, w_up, b_up, w_down, b_down, g_post_ffn):
    B, S, D = h.shape
    T = B * S
    TK = T * TOP_K

    mods = _ada(c_pad, w_ada, b_ada.reshape(1, -1))[:B].reshape(B, 6, D)

    w_in_p = jnp.pad(w_in, ((0, 0), (0, D_IN_PAD - w_in.shape[1]))).astype(BF16)
    head_of = jnp.arange(D_SGU) // SGU_HEAD_DIM
    gmat = jnp.where(head_of[:, None] == head_of[None, :], 1.0 / SGU_HEAD_DIM, 0.0).astype(BF16)
    ws_cat = w_s.reshape(SGU_HEADS // 2, 2, SGU_CHUNK, SGU_CHUNK).transpose(0, 2, 1, 3).reshape(
        SGU_HEADS // 2, SGU_CHUNK, 2 * SGU_CHUNK)
    bs_exp = jnp.repeat(b_s.T, SGU_HEAD_DIM, axis=1)
    wgk_p = jnp.pad(w_gk2, ((0, LANES - GLA_GATE_RANK), (0, 0))).astype(BF16)
    w_r_p = jnp.pad(w_router, ((0, 0), (0, LANES - N_EXPERTS))).astype(BF16)
    b_r_p = jnp.concatenate([b_router, jnp.full((LANES - N_EXPERTS,), NEG_BIG, F32)]).reshape(1, LANES)

    h1, n2, topi, probs, rank, cnt = _mixer(
        h, mods, g_pre_mix.reshape(1, D), w_in_p, gmat, g_sgu_v.reshape(1, D_SGU), ws_cat, bs_exp, wgk_p,
        b_gk.reshape(1, GLA_DK), g_gla_out.reshape(1, GLA_HEAD_V), w_out.astype(BF16), g_post_mix.reshape(1, D),
        g_pre_ffn.reshape(1, D), w_r_p, b_r_p)

    bm = MOE_BM
    counts = cnt[0, :N_EXPERTS].astype(jnp.int32)
    padded = ((counts + bm - 1) // bm) * bm
    pend = jnp.cumsum(padded)
    pstart = pend - padded
    e_kt = topi[:, :TOP_K].T
    onehot = e_kt[:, :, None] == jnp.arange(N_EXPERTS, dtype=jnp.int32)[None, None, :]
    dest_kt = rank[:, :TOP_K].T + jnp.sum(jnp.where(onehot, pstart[None, None, :], 0), axis=-1)
    n_blocks = TK // bm + N_EXPERTS
    block_start = jnp.arange(n_blocks, dtype=jnp.int32) * bm
    block_e = jnp.minimum(jnp.sum((pend[None, :] <= block_start[:, None]).astype(jnp.int32), axis=1), N_EXPERTS - 1)
    n_used = (pend[-1] // bm).astype(jnp.int32).reshape(1)

    dest3 = dest_kt.reshape(TOP_K, T // SC_CH, SC_CH).transpose(1, 0, 2)
    xs = _sc_dispatch(n2, dest3, n_blocks * bm)
    out = _experts(block_e, n_used, xs, w_gate, b_gate, w_up, b_up, w_down, b_down)
    og = _sc_gather(out, dest_kt.reshape(TK // SC_CH, SC_CH)).reshape(TOP_K, T, D)
    return _combine(h1, og, probs, mods, g_post_ffn.reshape(1, D))


def kernel(x, c, w_ada, b_ada, g_pre_mix, w_in, g_sgu_v, w_s, b_s, w_gk2, b_gk, g_gla_out, w_out, g_post_mix, g_pre_ffn, w_router, b_router, w_gate, b_gate, w_up, b_up, w_down, b_down, g_post_ffn):
    B = x.shape[0]
    c_pad = jnp.pad(c, ((0, -B % 8), (0, 0)))
    h = x
    for l in range(w_ada.shape[0]):
        h = _layer(h, c_pad, w_ada[l], b_ada[l], g_pre_mix[l], w_in[l], g_sgu_v[l], w_s[l], b_s[l], w_gk2[l], b_gk[l],
                   g_gla_out[l], w_out[l], g_post_mix[l], g_pre_ffn[l], w_router[l], b_router[l], w_gate[l], b_gate[l],
                   w_up[l], b_up[l], w_down[l], b_down[l], g_post_ffn[l])
    return h
```

```python
import functools

import jax
import jax.numpy as jnp
from jax import lax
from jax.experimental import pallas as pl
from jax.experimental.pallas import tpu as pltpu
from jax.experimental.pallas import tpu_sc as plsc

F32 = jnp.float32
BF16 = jnp.bfloat16

D_MODEL = 1024
D_SGU = 512
SGU_HEADS = 8
SGU_HEAD_DIM = 64
SGU_CHUNK = 128
D_GLA = 512
GLA_HEADS = 4
GLA_DK = 256
GLA_HEAD_K = 64
GLA_HEAD_V = 128
GLA_GATE_RANK = 16
GLA_TAU = 16.0
GLA_CHUNK = 64
N_EXPERTS = 32
TOP_K = 4
SWIGLU_LIMIT = 7.0
SWIGLU_ALPHA = 1.702
EPS = 1e-6

LANES = 128
D_IN_MAIN = 2 * D_SGU + 2 * GLA_DK + 2 * D_GLA
D_IN_PAD = D_IN_MAIN + LANES

MIX_TM = 256
MOE_BM = 512
CMB_TM = 256
SC_CORES = 2
SC_SUBCORES = 16
SC_WORKERS = SC_CORES * SC_SUBCORES
SC_CH = 64
NEG_BIG = -1e30

VMEM_LIMIT = 56 * 1024 * 1024


def _ada_kernel(c_ref, w_ref, b_ref, o_ref):
    c = c_ref[...]
    sc = c * jax.nn.sigmoid(c)
    o_ref[...] = jnp.dot(sc.astype(BF16), w_ref[...].astype(BF16), preferred_element_type=F32) + b_ref[...]


def _ada(c_pad, w_ada, b_ada):
    rows, d = c_pad.shape
    n = w_ada.shape[1]
    tn = 1536
    return pl.pallas_call(
        _ada_kernel,
        grid=(n // tn,),
        in_specs=[pl.BlockSpec((rows, d), lambda j: (0, 0)),
                  pl.BlockSpec((d, tn), lambda j: (0, j)),
                  pl.BlockSpec((1, tn), lambda j: (0, j))],
        out_specs=pl.BlockSpec((rows, tn), lambda j: (0, j)),
        out_shape=jax.ShapeDtypeStruct((rows, n), F32),
        compiler_params=pltpu.CompilerParams(dimension_semantics=("arbitrary",), vmem_limit_bytes=VMEM_LIMIT),
        name="ada",
    )(c_pad, w_ada, b_ada)


def _rms(x):
    return x * lax.rsqrt(jnp.mean(x * x, axis=-1, keepdims=True) + EPS)


def _gelu_tanh(x):
    return 0.5 * x * (1.0 + jnp.tanh(0.7978845608028654 * (x + 0.044715 * (x * x * x))))


def _log_sigmoid(z):
    return jnp.minimum(z, 0.0) - jnp.log(1.0 + jnp.exp(-jnp.abs(z)))


def _dot(a, b):
    return jnp.dot(a, b, preferred_element_type=F32)


def _dot_nt(a, b):
    return lax.dot_general(a, b, (((1,), (1,)), ((), ())), preferred_element_type=F32)


def _dot_tn(a, b):
    return lax.dot_general(a, b, (((0,), (0,)), ((), ())), preferred_element_type=F32)


def _pack_rows(x):
    n = x.shape[1] // 2
    lo = lax.bitcast_convert_type(x[:, :n].astype(BF16).astype(F32), jnp.uint32)
    hi = lax.bitcast_convert_type(x[:, n:].astype(BF16).astype(F32), jnp.uint32)
    return (lo >> 16) | (hi & jnp.uint32(0xFFFF0000))


def _unpack_rows(p):
    lo = lax.bitcast_convert_type(p << 16, F32)
    hi = lax.bitcast_convert_type(p & jnp.uint32(0xFFFF0000), F32)
    return lo, hi


def _split_dot(l_bf, a):
    hi = a.astype(BF16)
    lo = (a - hi.astype(F32)).astype(BF16)
    return _dot(l_bf, hi) + _dot(l_bf, lo)


def _mixer_kernel(x_ref, mods_ref, g_pre_ref, w_in_ref, gmat_ref, g_sgu_ref, ws_ref, bs_ref, wgk_ref, bgk_ref,
                  g_gla_ref, w_out_ref, g_post_ref, g_ffn_ref, w_r_ref, b_r_ref,
                  h1_ref, n2_ref, topi_ref, probs_ref, rank_ref, cnt_ref,
                  st_ref, carry_ref):
    tm = x_ref.shape[1]
    b = pl.program_id(0)
    j = pl.program_id(1)

    @pl.when(j == 0)
    def _():
        st_ref[...] = jnp.zeros_like(st_ref)

    @pl.when((b == 0) & (j == 0))
    def _():
        carry_ref[...] = jnp.zeros_like(carry_ref)

    mods = mods_ref[0]
    shift1, scale1, gate1 = mods[0:1], mods[1:2], mods[2:3]
    shift2, scale2 = mods[3:4], mods[4:5]

    x = x_ref[0]
    n = _rms(x) * (g_pre_ref[...] * (1.0 + scale1)) + shift1
    nb = n.astype(BF16)

    u = _gelu_tanh(_dot(nb, w_in_ref[:, 0:D_SGU]))
    v = _gelu_tanh(_dot(nb, w_in_ref[:, D_SGU:2 * D_SGU]))
    msv = _dot((v * v).astype(BF16), gmat_ref[...])
    vh = v * lax.rsqrt(msv + EPS) * g_sgu_ref[...]

    n_sc = tm // SGU_CHUNK
    lane_c = lax.broadcasted_iota(jnp.int32, (SGU_CHUNK, LANES), 1)
    low_half = lane_c < SGU_HEAD_DIM
    w_row = lax.broadcasted_iota(jnp.int32, (SGU_CHUNK, 2 * SGU_CHUNK), 0)
    w_col = lax.broadcasted_iota(jnp.int32, (SGU_CHUNK, 2 * SGU_CHUNK), 1) & (SGU_CHUNK - 1)
    a_cols = []
    for p in range(SGU_HEADS // 2):
        wcat = jnp.where(w_row >= w_col, ws_ref[p], 0.0).astype(BF16)
        rhs = []
        for ci in range(n_sc):
            vp = vh[ci * SGU_CHUNK:(ci + 1) * SGU_CHUNK, p * LANES:(p + 1) * LANES]
            rhs.append(jnp.concatenate([jnp.where(low_half, vp, 0.0), jnp.where(low_half, 0.0, vp)],
                                       axis=0).astype(BF16))
        res = _dot(wcat, jnp.concatenate(rhs, axis=1))
        bias = bs_ref[:, p * LANES:(p + 1) * LANES]
        a_cols.append(jnp.concatenate(
            [res[:, ci * LANES:(ci + 1) * LANES] + bias for ci in range(n_sc)], axis=0))
    a_out = u * jnp.concatenate(a_cols, axis=1)

    qk = _dot(nb, w_in_ref[:, 2 * D_SGU:2 * D_SGU + 2 * GLA_DK])
    q = qk[:, :GLA_DK] * (GLA_HEAD_K ** -0.5)
    k = qk[:, GLA_DK:]
    vv = _dot(nb, w_in_ref[:, 2 * D_SGU + 2 * GLA_DK:2 * D_SGU + 2 * GLA_DK + D_GLA]).astype(BF16)
    r = _dot(nb, w_in_ref[:, D_IN_MAIN - D_GLA:D_IN_MAIN])
    g_low = _dot(nb, w_in_ref[:, D_IN_MAIN:D_IN_PAD])
    z = _dot(g_low.astype(BF16), wgk_ref[...]) + bgk_ref[...]
    log_a = _log_sigmoid(z) * (1.0 / GLA_TAU)

    t_row = lax.broadcasted_iota(jnp.int32, (tm, tm), 0)
    t_col = lax.broadcasted_iota(jnp.int32, (tm, tm), 1)
    same_chunk = (t_row >> 6) == (t_col >> 6)
    l_cum = jnp.where(same_chunk & (t_row >= t_col), 1.0, 0.0).astype(BF16)
    l_all = jnp.where(same_chunk, 1.0, 0.0).astype(BF16)
    bcum = _split_dot(l_cum, log_a)
    blast = _split_dot(l_all, log_a)
    q_s = q * jnp.exp(bcum)
    k_s = (k * jnp.exp(-bcum)).astype(BF16)
    k_dec = (k * jnp.exp(blast - bcum)).astype(BF16)
    decay = jnp.exp(blast)

    lane_g = lax.broadcasted_iota(jnp.int32, (GLA_CHUNK, LANES), 1)
    causal = (lax.broadcasted_iota(jnp.int32, (GLA_CHUNK, GLA_CHUNK), 0)
              >= lax.broadcasted_iota(jnp.int32, (GLA_CHUNK, GLA_CHUNK), 1))
    g_gla = g_gla_ref[...]
    o_rows = []
    for ci in range(tm // GLA_CHUNK):
        rs = slice(ci * GLA_CHUNK, (ci + 1) * GLA_CHUNK)
        o_heads = []
        for h in range(GLA_HEADS):
            pl_ = slice((h // 2) * LANES, (h // 2 + 1) * LANES)
            own = (lane_g >= GLA_HEAD_K) if (h % 2) else (lane_g < GLA_HEAD_K)
            qm = jnp.where(own, q_s[rs, pl_], 0.0).astype(BF16)
            v_h = vv[rs, h * GLA_HEAD_V:(h + 1) * GLA_HEAD_V]
            scores = jnp.where(causal, _dot_nt(qm, k_s[rs, pl_]), 0.0)
            intra = _dot(scores.astype(BF16), v_h)
            st = st_ref[h]
            inter = _dot_nt(qm, st.astype(BF16))
            st_ref[h] = st * decay[ci * GLA_CHUNK:ci * GLA_CHUNK + 1, pl_] + _dot_tn(v_h, k_dec[rs, pl_])
            o_heads.append(_rms(intra + inter) * g_gla)
        o_rows.append(jnp.concatenate(o_heads, axis=1))
    o = jnp.concatenate(o_rows, axis=0) * (r * jax.nn.sigmoid(r))

    mix = _dot(jnp.concatenate([a_out, o], axis=1).astype(BF16), w_out_ref[...])
    h1 = x + gate1 * (_rms(mix) * g_post_ref[...])
    h1_ref[0] = h1

    n2 = _rms(h1) * (g_ffn_ref[...] * (1.0 + scale2)) + shift2
    n2_ref[...] = _pack_rows(n2)
    logits = _dot(n2.astype(BF16), w_r_ref[...]) + b_r_ref[...]

    lane = lax.broadcasted_iota(jnp.int32, (tm, LANES), 1)
    lane_f = lane.astype(F32)
    vals = logits
    sels, tops, idxs = [], [], []
    for _ in range(TOP_K):
        m = jnp.max(vals, axis=-1, keepdims=True)
        idx = jnp.min(jnp.where(vals == m, lane_f, float(LANES)), axis=-1, keepdims=True)
        sel = lane_f == idx
        sels.append(sel)
        tops.append(m)
        idxs.append(idx)
        vals = jnp.where(sel, -jnp.inf, vals)
    es = [jnp.exp(t - tops[0]) for t in tops]
    inv = 1.0 / (es[0] + es[1] + es[2] + es[3])

    multi = jnp.zeros((tm, LANES), F32)
    for sel in sels:
        multi = multi + jnp.where(sel, 1.0, 0.0)
    l_strict = jnp.where(t_row > t_col, 1.0, 0.0).astype(BF16)
    before = _dot(l_strict, multi.astype(BF16)) + carry_ref[...]
    carry_new = carry_ref[...] + jnp.sum(multi, axis=0, keepdims=True)
    carry_ref[...] = carry_new
    cnt_ref[...] = jnp.broadcast_to(carry_new, cnt_ref.shape)

    topi = jnp.zeros((tm, LANES), F32)
    probs = jnp.zeros((tm, LANES), F32)
    rank = jnp.zeros((tm, LANES), F32)
    for kk in range(TOP_K):
        here = lane == kk
        topi = jnp.where(here, idxs[kk], topi)
        probs = jnp.where(here, es[kk] * inv, probs)
        rank = jnp.where(here, jnp.sum(jnp.where(sels[kk], before, 0.0), axis=-1, keepdims=True), rank)
    topi_ref[...] = topi.astype(jnp.int32)
    probs_ref[...] = probs
    rank_ref[...] = rank.astype(jnp.int32)


def _const_spec(shape):
    return pl.BlockSpec(shape, lambda b, j: (0,) * len(shape))


def _mixer(x, mods, g_pre, w_in_p, gmat, g_sgu, ws_cat, bs_exp, wgk_p, bgk, g_gla, w_out_b, g_post, g_ffn, w_r_p, b_r_p):
    B, S, D = x.shape
    T = B * S
    tm = MIX_TM
    nj = S // tm
    tok_spec = pl.BlockSpec((tm, LANES), lambda b, j: (b * nj + j, 0))
    return pl.pallas_call(
        _mixer_kernel,
        grid=(B, nj),
        in_specs=[pl.BlockSpec((1, tm, D), lambda b, j: (b, j, 0)),
                  pl.BlockSpec((1, 6, D), lambda b, j: (b, 0, 0)),
                  _const_spec((1, D)), _const_spec(w_in_p.shape), _const_spec(gmat.shape), _const_spec((1, D_SGU)),
                  _const_spec(ws_cat.shape), _const_spec(bs_exp.shape), _const_spec(wgk_p.shape),
                  _const_spec((1, GLA_DK)), _const_spec((1, GLA_HEAD_V)), _const_spec(w_out_b.shape),
                  _const_spec((1, D)), _const_spec((1, D)), _const_spec(w_r_p.shape), _const_spec((1, LANES))],
        out_specs=[pl.BlockSpec((1, tm, D), lambda b, j: (b, j, 0)),
                   pl.BlockSpec((tm, D // 2), lambda b, j: (b * nj + j, 0)),
                   tok_spec, tok_spec, tok_spec,
                   pl.BlockSpec((8, LANES), lambda b, j: (0, 0))],
        out_shape=[jax.ShapeDtypeStruct((B, S, D), F32),
                   jax.ShapeDtypeStruct((T, D // 2), jnp.uint32),
                   jax.ShapeDtypeStruct((T, LANES), jnp.int32),
                   jax.ShapeDtypeStruct((T, LANES), F32),
                   jax.ShapeDtypeStruct((T, LANES), jnp.int32),
                   jax.ShapeDtypeStruct((8, LANES), F32)],
        scratch_shapes=[pltpu.VMEM((GLA_HEADS, GLA_HEAD_V, LANES), F32),
                        pltpu.VMEM((1, LANES), F32)],
        compiler_params=pltpu.CompilerParams(dimension_semantics=("arbitrary", "arbitrary"),
                                             vmem_limit_bytes=VMEM_LIMIT),
        name="mixer",
    )(x, mods, g_pre, w_in_p, gmat, g_sgu, ws_cat, bs_exp, wgk_p, bgk, g_gla, w_out_b, g_post, g_ffn, w_r_p, b_r_p)


def _sc_dispatch(rows, dest3, n_out):
    T, D = rows.shape
    n_chunks, top_k, ch = dest3.shape
    cpw = n_chunks // SC_WORKERS
    mesh = plsc.VectorSubcoreMesh(core_axis_name="c", subcore_axis_name="s")

    @functools.partial(
        pl.kernel, mesh=mesh,
        out_type=jax.ShapeDtypeStruct((n_out, D), rows.dtype),
        scratch_types=[pltpu.VMEM((top_k, ch), jnp.int32), pltpu.VMEM((ch, D), rows.dtype)],
    )
    def k(rows_hbm, dest_hbm, out_hbm, idx_v, rows_v):
        wid = lax.axis_index("s") * SC_CORES + lax.axis_index("c")

        @pl.loop(0, cpw)
        def _(i):
            c = wid * cpw + i
            pltpu.sync_copy(dest_hbm.at[c], idx_v)
            pltpu.sync_copy(rows_hbm.at[pl.ds(c * ch, ch)], rows_v)
            for kk in range(top_k):
                pltpu.sync_copy(rows_v, out_hbm.at[idx_v.at[kk]])

    return k(rows, dest3)


def _sc_gather(src, idx2):
    _, D = src.shape
    n_chunks, ch = idx2.shape
    cpw = n_chunks // SC_WORKERS
    mesh = plsc.VectorSubcoreMesh(core_axis_name="c", subcore_axis_name="s")

    @functools.partial(
        pl.kernel, mesh=mesh,
        out_type=jax.ShapeDtypeStruct((n_chunks * ch, D), src.dtype),
        scratch_types=[pltpu.VMEM((ch,), jnp.int32), pltpu.VMEM((ch, D), src.dtype)],
    )
    def k(src_hbm, idx_hbm, out_hbm, idx_v, rows_v):
        wid = lax.axis_index("s") * SC_CORES + lax.axis_index("c")

        @pl.loop(0, cpw)
        def _(i):
            c = wid * cpw + i
            pltpu.sync_copy(idx_hbm.at[c], idx_v)
            pltpu.sync_copy(src_hbm.at[idx_v], rows_v)
            pltpu.sync_copy(rows_v, out_hbm.at[pl.ds(c * ch, ch)])

    return k(src, idx2)


def _expert_kernel(be_ref, nu_ref, x_ref, wg_ref, bg_ref, wu_ref, bu_ref, wd_ref, bd_ref, o_ref, wbf_ref):
    i = pl.program_id(0)
    prev = be_ref[jnp.maximum(i - 1, 0)]
    active = i < nu_ref[0]

    @pl.when(active & ((i == 0) | (be_ref[i] != prev)))
    def _():
        wbf_ref[0] = wg_ref[0].astype(BF16)
        wbf_ref[1] = wu_ref[0].astype(BF16)
        wbf_ref[2] = wd_ref[0].astype(BF16)

    @pl.when(active)
    def _():
        x_lo, x_hi = _unpack_rows(x_ref[...])
        xb = jnp.concatenate([x_lo, x_hi], axis=1).astype(BF16)
        g = jnp.minimum(_dot(xb, wbf_ref[0]) + bg_ref[0], SWIGLU_LIMIT)
        u = jnp.clip(_dot(xb, wbf_ref[1]) + bu_ref[0], -SWIGLU_LIMIT, SWIGLU_LIMIT)
        hdn = (u + 1.0) * (g * jax.nn.sigmoid(SWIGLU_ALPHA * g))
        o_ref[...] = _pack_rows(_dot(hdn.astype(BF16), wbf_ref[2]) + bd_ref[0])


def _experts(block_e, n_used, xs, w_gate, b_gate, w_up, b_up, w_down, b_down):
    P, half = xs.shape
    bm = MOE_BM
    n_blocks = P // bm
    E, D, DE = w_gate.shape

    def row_map(i, be, nu):
        return (jnp.minimum(i, nu[0] - 1), 0)

    def w_map(i, be, nu):
        return (be[jnp.minimum(i, nu[0] - 1)], 0, 0)

    grid_spec = pltpu.PrefetchScalarGridSpec(
        num_scalar_prefetch=2,
        grid=(n_blocks,),
        in_specs=[pl.BlockSpec((bm, half), row_map),
                  pl.BlockSpec((1, D, DE), w_map), pl.BlockSpec((1, 1, DE), w_map),
                  pl.BlockSpec((1, D, DE), w_map), pl.BlockSpec((1, 1, DE), w_map),
                  pl.BlockSpec((1, DE, D), w_map), pl.BlockSpec((1, 1, D), w_map)],
        out_specs=pl.BlockSpec((bm, half), row_map),
        scratch_shapes=[pltpu.VMEM((3, D, DE), BF16)],
    )
    return pl.pallas_call(
        _expert_kernel,
        grid_spec=grid_spec,
        out_shape=jax.ShapeDtypeStruct((P, half), jnp.uint32),
        compiler_params=pltpu.CompilerParams(dimension_semantics=("arbitrary",), vmem_limit_bytes=VMEM_LIMIT),
        name="experts",
    )(block_e, n_used, xs, w_gate, b_gate.reshape(E, 1, DE), w_up, b_up.reshape(E, 1, DE),
      w_down, b_down.reshape(E, 1, D))


def _combine_kernel(h1_ref, og_ref, probs_ref, mods_ref, g_ref, o_ref):
    tm = h1_ref.shape[1]
    probs = probs_ref[...]
    lane = lax.broadcasted_iota(jnp.int32, (tm, LANES), 1)
    half = og_ref.shape[2]
    f_lo = jnp.zeros((tm, half), F32)
    f_hi = jnp.zeros((tm, half), F32)
    for kk in range(TOP_K):
        pk = jnp.sum(jnp.where(lane == kk, probs, 0.0), axis=-1, keepdims=True)
        o_lo, o_hi = _unpack_rows(og_ref[kk])
        f_lo = f_lo + pk * o_lo
        f_hi = f_hi + pk * o_hi
    f = jnp.concatenate([f_lo, f_hi], axis=1)
    gate2 = mods_ref[0][5:6]
    o_ref[0] = h1_ref[0] + gate2 * (_rms(f) * g_ref[...])


def _combine(h1, og, probs, mods, g_post_ffn):
    B, S, D = h1.shape
    tm = CMB_TM
    nj = S // tm
    return pl.pallas_call(
        _combine_kernel,
        grid=(B, nj),
        in_specs=[pl.BlockSpec((1, tm, D), lambda b, j: (b, j, 0)),
                  pl.BlockSpec((TOP_K, tm, D // 2), lambda b, j: (0, b * nj + j, 0)),
                  pl.BlockSpec((tm, LANES), lambda b, j: (b * nj + j, 0)),
                  pl.BlockSpec((1, 6, D), lambda b, j: (b, 0, 0)),
                  pl.BlockSpec((1, D), lambda b, j: (0, 0))],
        out_specs=pl.BlockSpec((1, tm, D), lambda b, j: (b, j, 0)),
        out_shape=jax.ShapeDtypeStruct((B, S, D), F32),
        compiler_params=pltpu.CompilerParams(dimension_semantics=("arbitrary", "arbitrary"),
                                             vmem_limit_bytes=VMEM_LIMIT),
        name="combine",
    )(h1, og, probs, mods, g_post_ffn)


def _layer(h, c_pad, w_ada, b_ada, g_pre_mix, w_in, g_sgu_v, w_s, b_s, w_gk2, b_gk, g_gla_out, w_out, g_post_mix,
           g_pre_ffn, w_router, b_router, w_gate, b_gate, w_up, b_up, w_down, b_down, g_post_ffn):
    B, S, D = h.shape
    T = B * S
    TK = T * TOP_K

    mods = _ada(c_pad, w_ada, b_ada.reshape(1, -1))[:B].reshape(B, 6, D)

    w_in_p = jnp.pad(w_in, ((0, 0), (0, D_IN_PAD - w_in.shape[1]))).astype(BF16)
    head_of = jnp.arange(D_SGU) // SGU_HEAD_DIM
    gmat = jnp.where(head_of[:, None] == head_of[None, :], 1.0 / SGU_HEAD_DIM, 0.0).astype(BF16)
    ws_cat = w_s.reshape(SGU_HEADS // 2, 2, SGU_CHUNK, SGU_CHUNK).transpose(0, 2, 1, 3).reshape(
        SGU_HEADS // 2, SGU_CHUNK, 2 * SGU_CHUNK)
    bs_exp = jnp.repeat(b_s.T, SGU_HEAD_DIM, axis=1)
    wgk_p = jnp.pad(w_gk2, ((0, LANES - GLA_GATE_RANK), (0, 0))).astype(BF16)
    w_r_p = jnp.pad(w_router, ((0, 0), (0, LANES - N_EXPERTS))).astype(BF16)
    b_r_p = jnp.concatenate([b_router, jnp.full((LANES - N_EXPERTS,), NEG_BIG, F32)]).reshape(1, LANES)

    h1, n2, topi, probs, rank, cnt = _mixer(
        h, mods, g_pre_mix.reshape(1, D), w_in_p, gmat, g_sgu_v.reshape(1, D_SGU), ws_cat, bs_exp, wgk_p,
        b_gk.reshape(1, GLA_DK), g_gla_out.reshape(1, GLA_HEAD_V), w_out.astype(BF16), g_post_mix.reshape(1, D),
        g_pre_ffn.reshape(1, D), w_r_p, b_r_p)

    bm = MOE_BM
    counts = cnt[0, :N_EXPERTS].astype(jnp.int32)
    padded = ((counts + bm - 1) // bm) * bm
    pend = jnp.cumsum(padded)
    pstart = pend - padded
    e_kt = topi[:, :TOP_K].T
    onehot = e_kt[:, :, None] == jnp.arange(N_EXPERTS, dtype=jnp.int32)[None, None, :]
    dest_kt = rank[:, :TOP_K].T + jnp.sum(jnp.where(onehot, pstart[None, None, :], 0), axis=-1)
    n_blocks = TK // bm + N_EXPERTS
    block_start = jnp.arange(n_blocks, dtype=jnp.int32) * bm
    block_e = jnp.minimum(jnp.sum((pend[None, :] <= block_start[:, None]).astype(jnp.int32), axis=1), N_EXPERTS - 1)
    n_used = (pend[-1] // bm).astype(jnp.int32).reshape(1)

    dest3 = dest_kt.reshape(TOP_K, T // SC_CH, SC_CH).transpose(1, 0, 2)
    xs = _sc_dispatch(n2, dest3, n_blocks * bm)
    out = _experts(block_e, n_used, xs, w_gate, b_gate, w_up, b_up, w_down, b_down)
    og = _sc_gather(out, dest_kt.reshape(TK // SC_CH, SC_CH)).reshape(TOP_K, T, D // 2)
    return _combine(h1, og, probs, mods, g_post_ffn.reshape(1, D))


def kernel(x, c, w_ada, b_ada, g_pre_mix, w_in, g_sgu_v, w_s, b_s, w_gk2, b_gk, g_gla_out, w_out, g_post_mix, g_pre_ffn, w_router, b_router, w_gate, b_gate, w_up, b_up, w_down, b_down, g_post_ffn):
    B = x.shape[0]
    c_pad = jnp.pad(c, ((0, -B % 8), (0, 0)))
    h = x
    for l in range(w_ada.shape[0]):
        h = _layer(h, c_pad, w_ada[l], b_ada[l], g_pre_mix[l], w_in[l], g_sgu_v[l], w_s[l], b_s[l], w_gk2[l], b_gk[l],
                   g_gla_out[l], w_out[l], g_post_mix[l], g_pre_ffn[l], w_router[l], b_router[l], w_gate[l], b_gate[l],
                   w_up[l], b_up[l], w_down[l], b_down[l], g_post_ffn[l])
    return h
```

```python
import functools

import jax
import jax.numpy as jnp
from jax import lax
from jax.experimental import pallas as pl
from jax.experimental.pallas import tpu as pltpu
from jax.experimental.pallas import tpu_sc as plsc

F32 = jnp.float32
BF16 = jnp.bfloat16

D_MODEL = 1024
D_SGU = 512
SGU_HEADS = 8
SGU_HEAD_DIM = 64
SGU_CHUNK = 128
D_GLA = 512
GLA_HEADS = 4
GLA_DK = 256
GLA_HEAD_K = 64
GLA_HEAD_V = 128
GLA_GATE_RANK = 16
GLA_TAU = 16.0
GLA_CHUNK = 64
N_EXPERTS = 32
TOP_K = 4
SWIGLU_LIMIT = 7.0
SWIGLU_ALPHA = 1.702
EPS = 1e-6

LANES = 128
D_IN_MAIN = 2 * D_SGU + 2 * GLA_DK + 2 * D_GLA
D_IN_PAD = D_IN_MAIN + LANES

MIX_TM = 256
MOE_BM = 512
CMB_TM = 256
N_GROUPS = 2
SC_CORES = 2
SC_SUBCORES = 16
SC_WORKERS = SC_CORES * SC_SUBCORES
SC_CH = 64
NEG_BIG = -1e30

VMEM_LIMIT = 56 * 1024 * 1024


def _ada_kernel(c_ref, w_ref, b_ref, o_ref):
    c = c_ref[...]
    sc = c * jax.nn.sigmoid(c)
    o_ref[...] = jnp.dot(sc.astype(BF16), w_ref[...].astype(BF16), preferred_element_type=F32) + b_ref[...]


def _ada(c_pad, w_ada, b_ada):
    rows, d = c_pad.shape
    n = w_ada.shape[1]
    tn = 1536
    return pl.pallas_call(
        _ada_kernel,
        grid=(n // tn,),
        in_specs=[pl.BlockSpec((rows, d), lambda j: (0, 0)),
                  pl.BlockSpec((d, tn), lambda j: (0, j)),
                  pl.BlockSpec((1, tn), lambda j: (0, j))],
        out_specs=pl.BlockSpec((rows, tn), lambda j: (0, j)),
        out_shape=jax.ShapeDtypeStruct((rows, n), F32),
        compiler_params=pltpu.CompilerParams(dimension_semantics=("arbitrary",), vmem_limit_bytes=VMEM_LIMIT),
        name="ada",
    )(c_pad, w_ada, b_ada)


def _rms(x):
    return x * lax.rsqrt(jnp.mean(x * x, axis=-1, keepdims=True) + EPS)


def _gelu_tanh(x):
    return 0.5 * x * (1.0 + jnp.tanh(0.7978845608028654 * (x + 0.044715 * (x * x * x))))


def _log_sigmoid(z):
    return jnp.minimum(z, 0.0) - jnp.log(1.0 + jnp.exp(-jnp.abs(z)))


def _dot(a, b):
    return jnp.dot(a, b, preferred_element_type=F32)


def _dot_nt(a, b):
    return lax.dot_general(a, b, (((1,), (1,)), ((), ())), preferred_element_type=F32)


def _dot_tn(a, b):
    return lax.dot_general(a, b, (((0,), (0,)), ((), ())), preferred_element_type=F32)


def _pack_rows(x):
    n = x.shape[1] // 2
    lo = lax.bitcast_convert_type(x[:, :n].astype(BF16).astype(F32), jnp.uint32)
    hi = lax.bitcast_convert_type(x[:, n:].astype(BF16).astype(F32), jnp.uint32)
    return (lo >> 16) | (hi & jnp.uint32(0xFFFF0000))


def _unpack_rows(p):
    lo = lax.bitcast_convert_type(p << 16, F32)
    hi = lax.bitcast_convert_type(p & jnp.uint32(0xFFFF0000), F32)
    return lo, hi


def _split_dot(l_bf, a):
    hi = a.astype(BF16)
    lo = (a - hi.astype(F32)).astype(BF16)
    return _dot(l_bf, hi) + _dot(l_bf, lo)


def _mixer_kernel(x_ref, mods_ref, g_pre_ref, w_in_ref, gmat_ref, g_sgu_ref, ws_ref, bs_ref, wgk_ref, bgk_ref,
                  g_gla_ref, w_out_ref, g_post_ref, g_ffn_ref, w_r_ref, b_r_ref,
                  h1_ref, n2_ref, topi_ref, probs_ref, rank_ref, cnt_ref,
                  st_ref, carry_ref):
    tm = x_ref.shape[1]
    b = pl.program_id(0)
    j = pl.program_id(1)

    @pl.when(j == 0)
    def _():
        st_ref[...] = jnp.zeros_like(st_ref)

    @pl.when((b == 0) & (j == 0))
    def _():
        carry_ref[...] = jnp.zeros_like(carry_ref)

    mods = mods_ref[0]
    shift1, scale1, gate1 = mods[0:1], mods[1:2], mods[2:3]
    shift2, scale2 = mods[3:4], mods[4:5]

    x = x_ref[0]
    n = _rms(x) * (g_pre_ref[...] * (1.0 + scale1)) + shift1
    nb = n.astype(BF16)

    u = _gelu_tanh(_dot(nb, w_in_ref[:, 0:D_SGU]))
    v = _gelu_tanh(_dot(nb, w_in_ref[:, D_SGU:2 * D_SGU]))
    msv = _dot((v * v).astype(BF16), gmat_ref[...])
    vh = v * lax.rsqrt(msv + EPS) * g_sgu_ref[...]

    n_sc = tm // SGU_CHUNK
    lane_c = lax.broadcasted_iota(jnp.int32, (SGU_CHUNK, LANES), 1)
    low_half = lane_c < SGU_HEAD_DIM
    w_row = lax.broadcasted_iota(jnp.int32, (SGU_CHUNK, 2 * SGU_CHUNK), 0)
    w_col = lax.broadcasted_iota(jnp.int32, (SGU_CHUNK, 2 * SGU_CHUNK), 1) & (SGU_CHUNK - 1)
    a_cols = []
    for p in range(SGU_HEADS // 2):
        wcat = jnp.where(w_row >= w_col, ws_ref[p], 0.0).astype(BF16)
        rhs = []
        for ci in range(n_sc):
            vp = vh[ci * SGU_CHUNK:(ci + 1) * SGU_CHUNK, p * LANES:(p + 1) * LANES]
            rhs.append(jnp.concatenate([jnp.where(low_half, vp, 0.0), jnp.where(low_half, 0.0, vp)],
                                       axis=0).astype(BF16))
        res = _dot(wcat, jnp.concatenate(rhs, axis=1))
        bias = bs_ref[:, p * LANES:(p + 1) * LANES]
        a_cols.append(jnp.concatenate(
            [res[:, ci * LANES:(ci + 1) * LANES] + bias for ci in range(n_sc)], axis=0))
    a_out = u * jnp.concatenate(a_cols, axis=1)

    qk = _dot(nb, w_in_ref[:, 2 * D_SGU:2 * D_SGU + 2 * GLA_DK])
    q = qk[:, :GLA_DK] * (GLA_HEAD_K ** -0.5)
    k = qk[:, GLA_DK:]
    vv = _dot(nb, w_in_ref[:, 2 * D_SGU + 2 * GLA_DK:2 * D_SGU + 2 * GLA_DK + D_GLA]).astype(BF16)
    r = _dot(nb, w_in_ref[:, D_IN_MAIN - D_GLA:D_IN_MAIN])
    g_low = _dot(nb, w_in_ref[:, D_IN_MAIN:D_IN_PAD])
    z = _dot(g_low.astype(BF16), wgk_ref[...]) + bgk_ref[...]
    log_a = _log_sigmoid(z) * (1.0 / GLA_TAU)

    t_row = lax.broadcasted_iota(jnp.int32, (tm, tm), 0)
    t_col = lax.broadcasted_iota(jnp.int32, (tm, tm), 1)
    same_chunk = (t_row >> 6) == (t_col >> 6)
    l_cum = jnp.where(same_chunk & (t_row >= t_col), 1.0, 0.0).astype(BF16)
    l_all = jnp.where(same_chunk, 1.0, 0.0).astype(BF16)
    bcum = _split_dot(l_cum, log_a)
    blast = _split_dot(l_all, log_a)
    q_s = q * jnp.exp(bcum)
    k_s = (k * jnp.exp(-bcum)).astype(BF16)
    k_dec = (k * jnp.exp(blast - bcum)).astype(BF16)
    decay = jnp.exp(blast)

    lane_g = lax.broadcasted_iota(jnp.int32, (GLA_CHUNK, LANES), 1)
    causal = (lax.broadcasted_iota(jnp.int32, (GLA_CHUNK, GLA_CHUNK), 0)
              >= lax.broadcasted_iota(jnp.int32, (GLA_CHUNK, GLA_CHUNK), 1))
    g_gla = g_gla_ref[...]
    o_rows = []
    for ci in range(tm // GLA_CHUNK):
        rs = slice(ci * GLA_CHUNK, (ci + 1) * GLA_CHUNK)
        o_heads = []
        for h in range(GLA_HEADS):
            pl_ = slice((h // 2) * LANES, (h // 2 + 1) * LANES)
            own = (lane_g >= GLA_HEAD_K) if (h % 2) else (lane_g < GLA_HEAD_K)
            qm = jnp.where(own, q_s[rs, pl_], 0.0).astype(BF16)
            v_h = vv[rs, h * GLA_HEAD_V:(h + 1) * GLA_HEAD_V]
            scores = jnp.where(causal, _dot_nt(qm, k_s[rs, pl_]), 0.0)
            intra = _dot(scores.astype(BF16), v_h)
            st = st_ref[h]
            inter = _dot_nt(qm, st.astype(BF16))
            st_ref[h] = st * decay[ci * GLA_CHUNK:ci * GLA_CHUNK + 1, pl_] + _dot_tn(v_h, k_dec[rs, pl_])
            o_heads.append(_rms(intra + inter) * g_gla)
        o_rows.append(jnp.concatenate(o_heads, axis=1))
    o = jnp.concatenate(o_rows, axis=0) * (r * jax.nn.sigmoid(r))

    mix = _dot(jnp.concatenate([a_out, o], axis=1).astype(BF16), w_out_ref[...])
    h1 = x + gate1 * (_rms(mix) * g_post_ref[...])
    h1_ref[0] = h1

    n2 = _rms(h1) * (g_ffn_ref[...] * (1.0 + scale2)) + shift2
    n2_ref[...] = _pack_rows(n2)
    logits = _dot(n2.astype(BF16), w_r_ref[...]) + b_r_ref[...]

    lane = lax.broadcasted_iota(jnp.int32, (tm, LANES), 1)
    lane_f = lane.astype(F32)
    vals = logits
    sels, tops, idxs = [], [], []
    for _ in range(TOP_K):
        m = jnp.max(vals, axis=-1, keepdims=True)
        idx = jnp.min(jnp.where(vals == m, lane_f, float(LANES)), axis=-1, keepdims=True)
        sel = lane_f == idx
        sels.append(sel)
        tops.append(m)
        idxs.append(idx)
        vals = jnp.where(sel, -jnp.inf, vals)
    es = [jnp.exp(t - tops[0]) for t in tops]
    inv = 1.0 / (es[0] + es[1] + es[2] + es[3])

    multi = jnp.zeros((tm, LANES), F32)
    for sel in sels:
        multi = multi + jnp.where(sel, 1.0, 0.0)
    l_strict = jnp.where(t_row > t_col, 1.0, 0.0).astype(BF16)
    before = _dot(l_strict, multi.astype(BF16)) + carry_ref[...]
    carry_new = carry_ref[...] + jnp.sum(multi, axis=0, keepdims=True)
    carry_ref[...] = carry_new
    cnt_ref[...] = jnp.broadcast_to(carry_new, cnt_ref.shape)

    topi = jnp.zeros((tm, LANES), F32)
    probs = jnp.zeros((tm, LANES), F32)
    rank = jnp.zeros((tm, LANES), F32)
    for kk in range(TOP_K):
        here = lane == kk
        topi = jnp.where(here, idxs[kk], topi)
        probs = jnp.where(here, es[kk] * inv, probs)
        rank = jnp.where(here, jnp.sum(jnp.where(sels[kk], before, 0.0), axis=-1, keepdims=True), rank)
    topi_ref[...] = topi.astype(jnp.int32)
    probs_ref[...] = probs
    rank_ref[...] = rank.astype(jnp.int32)


def _const_spec(shape):
    return pl.BlockSpec(shape, lambda b, j: (0,) * len(shape))


def _mixer(b0, B, x, mods, g_pre, w_in_p, gmat, g_sgu, ws_cat, bs_exp, wgk_p, bgk, g_gla, w_out_b, g_post, g_ffn, w_r_p,
           b_r_p):
    _, S, D = x.shape
    T = B * S
    tm = MIX_TM
    nj = S // tm
    tok_spec = pl.BlockSpec((tm, LANES), lambda b, j: (b * nj + j, 0))
    return pl.pallas_call(
        _mixer_kernel,
        grid=(B, nj),
        in_specs=[pl.BlockSpec((1, tm, D), lambda b, j: (b + b0, j, 0)),
                  pl.BlockSpec((1, 6, D), lambda b, j: (b + b0, 0, 0)),
                  _const_spec((1, D)), _const_spec(w_in_p.shape), _const_spec(gmat.shape), _const_spec((1, D_SGU)),
                  _const_spec(ws_cat.shape), _const_spec(bs_exp.shape), _const_spec(wgk_p.shape),
                  _const_spec((1, GLA_DK)), _const_spec((1, GLA_HEAD_V)), _const_spec(w_out_b.shape),
                  _const_spec((1, D)), _const_spec((1, D)), _const_spec(w_r_p.shape), _const_spec((1, LANES))],
        out_specs=[pl.BlockSpec((1, tm, D), lambda b, j: (b, j, 0)),
                   pl.BlockSpec((tm, D // 2), lambda b, j: (b * nj + j, 0)),
                   tok_spec, tok_spec, tok_spec,
                   pl.BlockSpec((8, LANES), lambda b, j: (0, 0))],
        out_shape=[jax.ShapeDtypeStruct((B, S, D), F32),
                   jax.ShapeDtypeStruct((T, D // 2), jnp.uint32),
                   jax.ShapeDtypeStruct((T, LANES), jnp.int32),
                   jax.ShapeDtypeStruct((T, LANES), F32),
                   jax.ShapeDtypeStruct((T, LANES), jnp.int32),
                   jax.ShapeDtypeStruct((8, LANES), F32)],
        scratch_shapes=[pltpu.VMEM((GLA_HEADS, GLA_HEAD_V, LANES), F32),
                        pltpu.VMEM((1, LANES), F32)],
        compiler_params=pltpu.CompilerParams(dimension_semantics=("arbitrary", "arbitrary"),
                                             vmem_limit_bytes=VMEM_LIMIT),
        name="mixer",
    )(x, mods, g_pre, w_in_p, gmat, g_sgu, ws_cat, bs_exp, wgk_p, bgk, g_gla, w_out_b, g_post, g_ffn, w_r_p, b_r_p)


def _sc_dispatch(rows, dest3, n_out):
    T, D = rows.shape
    n_chunks, top_k, ch = dest3.shape
    cpw = n_chunks // SC_WORKERS
    mesh = plsc.VectorSubcoreMesh(core_axis_name="c", subcore_axis_name="s")

    @functools.partial(
        pl.kernel, mesh=mesh,
        out_type=jax.ShapeDtypeStruct((n_out, D), rows.dtype),
        scratch_types=[pltpu.VMEM((top_k, ch), jnp.int32), pltpu.VMEM((ch, D), rows.dtype)],
    )
    def k(rows_hbm, dest_hbm, out_hbm, idx_v, rows_v):
        wid = lax.axis_index("s") * SC_CORES + lax.axis_index("c")

        @pl.loop(0, cpw)
        def _(i):
            c = wid * cpw + i
            pltpu.sync_copy(dest_hbm.at[c], idx_v)
            pltpu.sync_copy(rows_hbm.at[pl.ds(c * ch, ch)], rows_v)
            for kk in range(top_k):
                pltpu.sync_copy(rows_v, out_hbm.at[idx_v.at[kk]])

    return k(rows, dest3)


def _sc_gather(src, idx2):
    _, D = src.shape
    n_chunks, ch = idx2.shape
    cpw = n_chunks // SC_WORKERS
    mesh = plsc.VectorSubcoreMesh(core_axis_name="c", subcore_axis_name="s")

    @functools.partial(
        pl.kernel, mesh=mesh,
        out_type=jax.ShapeDtypeStruct((n_chunks * ch, D), src.dtype),
        scratch_types=[pltpu.VMEM((ch,), jnp.int32), pltpu.VMEM((ch, D), src.dtype)],
    )
    def k(src_hbm, idx_hbm, out_hbm, idx_v, rows_v):
        wid = lax.axis_index("s") * SC_CORES + lax.axis_index("c")

        @pl.loop(0, cpw)
        def _(i):
            c = wid * cpw + i
            pltpu.sync_copy(idx_hbm.at[c], idx_v)
            pltpu.sync_copy(src_hbm.at[idx_v], rows_v)
            pltpu.sync_copy(rows_v, out_hbm.at[pl.ds(c * ch, ch)])

    return k(src, idx2)


def _expert_kernel(be_ref, nu_ref, x_ref, wg_ref, bg_ref, wu_ref, bu_ref, wd_ref, bd_ref, o_ref, wbf_ref):
    i = pl.program_id(0)
    prev = be_ref[jnp.maximum(i - 1, 0)]
    active = i < nu_ref[0]

    @pl.when(active & ((i == 0) | (be_ref[i] != prev)))
    def _():
        wbf_ref[0] = wg_ref[0].astype(BF16)
        wbf_ref[1] = wu_ref[0].astype(BF16)
        wbf_ref[2] = wd_ref[0].astype(BF16)

    @pl.when(active)
    def _():
        x_lo, x_hi = _unpack_rows(x_ref[...])
        xb = jnp.concatenate([x_lo, x_hi], axis=1).astype(BF16)
        g = jnp.minimum(_dot(xb, wbf_ref[0]) + bg_ref[0], SWIGLU_LIMIT)
        u = jnp.clip(_dot(xb, wbf_ref[1]) + bu_ref[0], -SWIGLU_LIMIT, SWIGLU_LIMIT)
        hdn = (u + 1.0) * (g * jax.nn.sigmoid(SWIGLU_ALPHA * g))
        o_ref[...] = _pack_rows(_dot(hdn.astype(BF16), wbf_ref[2]) + bd_ref[0])


def _experts(block_e, n_used, xs, w_gate, b_gate, w_up, b_up, w_down, b_down):
    P, half = xs.shape
    bm = MOE_BM
    n_blocks = P // bm
    E, D, DE = w_gate.shape

    def row_map(i, be, nu):
        return (jnp.minimum(i, nu[0] - 1), 0)

    def w_map(i, be, nu):
        return (be[jnp.minimum(i, nu[0] - 1)], 0, 0)

    grid_spec = pltpu.PrefetchScalarGridSpec(
        num_scalar_prefetch=2,
        grid=(n_blocks,),
        in_specs=[pl.BlockSpec((bm, half), row_map),
                  pl.BlockSpec((1, D, DE), w_map), pl.BlockSpec((1, 1, DE), w_map),
                  pl.BlockSpec((1, D, DE), w_map), pl.BlockSpec((1, 1, DE), w_map),
                  pl.BlockSpec((1, DE, D), w_map), pl.BlockSpec((1, 1, D), w_map)],
        out_specs=pl.BlockSpec((bm, half), row_map),
        scratch_shapes=[pltpu.VMEM((3, D, DE), BF16)],
    )
    return pl.pallas_call(
        _expert_kernel,
        grid_spec=grid_spec,
        out_shape=jax.ShapeDtypeStruct((P, half), jnp.uint32),
        compiler_params=pltpu.CompilerParams(dimension_semantics=("arbitrary",), vmem_limit_bytes=VMEM_LIMIT),
        name="experts",
    )(block_e, n_used, xs, w_gate, b_gate.reshape(E, 1, DE), w_up, b_up.reshape(E, 1, DE),
      w_down, b_down.reshape(E, 1, D))


def _combine_kernel(h1_ref, og_ref, probs_ref, mods_ref, g_ref, *rest):
    o_ref = rest[-1]
    tm = h1_ref.shape[1]
    probs = probs_ref[...]
    lane = lax.broadcasted_iota(jnp.int32, (tm, LANES), 1)
    half = og_ref.shape[2]
    f_lo = jnp.zeros((tm, half), F32)
    f_hi = jnp.zeros((tm, half), F32)
    for kk in range(TOP_K):
        pk = jnp.sum(jnp.where(lane == kk, probs, 0.0), axis=-1, keepdims=True)
        o_lo, o_hi = _unpack_rows(og_ref[kk])
        f_lo = f_lo + pk * o_lo
        f_hi = f_hi + pk * o_hi
    f = jnp.concatenate([f_lo, f_hi], axis=1)
    gate2 = mods_ref[0][5:6]
    o_ref[0] = h1_ref[0] + gate2 * (_rms(f) * g_ref[...])


def _combine(b0, b_total, h1, og, probs, mods, g_post_ffn, out_so_far):
    B, S, D = h1.shape
    tm = CMB_TM
    nj = S // tm
    in_specs = [pl.BlockSpec((1, tm, D), lambda b, j: (b, j, 0)),
                pl.BlockSpec((TOP_K, tm, D // 2), lambda b, j: (0, b * nj + j, 0)),
                pl.BlockSpec((tm, LANES), lambda b, j: (b * nj + j, 0)),
                pl.BlockSpec((1, 6, D), lambda b, j: (b + b0, 0, 0)),
                pl.BlockSpec((1, D), lambda b, j: (0, 0))]
    args = [h1, og, probs, mods, g_post_ffn]
    aliases = {}
    if out_so_far is not None:
        in_specs.append(pl.BlockSpec(memory_space=pl.ANY))
        args.append(out_so_far)
        aliases = {len(args) - 1: 0}
    return pl.pallas_call(
        _combine_kernel,
        grid=(B, nj),
        in_specs=in_specs,
        out_specs=pl.BlockSpec((1, tm, D), lambda b, j: (b + b0, j, 0)),
        out_shape=jax.ShapeDtypeStruct((b_total, S, D), F32),
        input_output_aliases=aliases,
        compiler_params=pltpu.CompilerParams(dimension_semantics=("arbitrary", "arbitrary"),
                                             vmem_limit_bytes=VMEM_LIMIT),
        name="combine",
    )(*args)


def _layer(h, c_pad, w_ada, b_ada, g_pre_mix, w_in, g_sgu_v, w_s, b_s, w_gk2, b_gk, g_gla_out, w_out, g_post_mix,
           g_pre_ffn, w_router, b_router, w_gate, b_gate, w_up, b_up, w_down, b_down, g_post_ffn):
    B, S, D = h.shape

    mods = _ada(c_pad, w_ada, b_ada.reshape(1, -1))[:B].reshape(B, 6, D)

    w_in_p = jnp.pad(w_in, ((0, 0), (0, D_IN_PAD - w_in.shape[1]))).astype(BF16)
    head_of = jnp.arange(D_SGU) // SGU_HEAD_DIM
    gmat = jnp.where(head_of[:, None] == head_of[None, :], 1.0 / SGU_HEAD_DIM, 0.0).astype(BF16)
    ws_cat = w_s.reshape(SGU_HEADS // 2, 2, SGU_CHUNK, SGU_CHUNK).transpose(0, 2, 1, 3).reshape(
        SGU_HEADS // 2, SGU_CHUNK, 2 * SGU_CHUNK)
    bs_exp = jnp.repeat(b_s.T, SGU_HEAD_DIM, axis=1)
    wgk_p = jnp.pad(w_gk2, ((0, LANES - GLA_GATE_RANK), (0, 0))).astype(BF16)
    w_r_p = jnp.pad(w_router, ((0, 0), (0, LANES - N_EXPERTS))).astype(BF16)
    b_r_p = jnp.concatenate([b_router, jnp.full((LANES - N_EXPERTS,), NEG_BIG, F32)]).reshape(1, LANES)

    w_out_b = w_out.astype(BF16)

    n_groups = N_GROUPS if B % N_GROUPS == 0 else 1
    bg = B // n_groups
    tg = bg * S
    result = None
    for gi in range(n_groups):
        b0 = gi * bg
        h1, n2, topi, probs, rank, cnt = _mixer(
            b0, bg, h, mods, g_pre_mix.reshape(1, D), w_in_p, gmat, g_sgu_v.reshape(1, D_SGU), ws_cat, bs_exp, wgk_p,
            b_gk.reshape(1, GLA_DK), g_gla_out.reshape(1, GLA_HEAD_V), w_out_b, g_post_mix.reshape(1, D),
            g_pre_ffn.reshape(1, D), w_r_p, b_r_p)

        bm = MOE_BM
        counts = cnt[0, :N_EXPERTS].astype(jnp.int32)
        padded = ((counts + bm - 1) // bm) * bm
        pend = jnp.cumsum(padded)
        pstart = pend - padded
        e_kt = topi[:, :TOP_K].T
        onehot = e_kt[:, :, None] == jnp.arange(N_EXPERTS, dtype=jnp.int32)[None, None, :]
        dest_kt = rank[:, :TOP_K].T + jnp.sum(jnp.where(onehot, pstart[None, None, :], 0), axis=-1)
        n_blocks = tg * TOP_K // bm + N_EXPERTS
        block_start = jnp.arange(n_blocks, dtype=jnp.int32) * bm
        block_e = jnp.minimum(jnp.sum((pend[None, :] <= block_start[:, None]).astype(jnp.int32), axis=1),
                              N_EXPERTS - 1)
        n_used = (pend[-1] // bm).astype(jnp.int32).reshape(1)

        dest3 = dest_kt.reshape(TOP_K, tg // SC_CH, SC_CH).transpose(1, 0, 2)
        xs = _sc_dispatch(n2, dest3, n_blocks * bm)
        out = _experts(block_e, n_used, xs, w_gate, b_gate, w_up, b_up, w_down, b_down)
        og = _sc_gather(out, dest_kt.reshape(tg * TOP_K // SC_CH, SC_CH)).reshape(TOP_K, tg, D // 2)
        result = _combine(b0, B, h1, og, probs, mods, g_post_ffn.reshape(1, D), result)
    return result


def kernel(x, c, w_ada, b_ada, g_pre_mix, w_in, g_sgu_v, w_s, b_s, w_gk2, b_gk, g_gla_out, w_out, g_post_mix, g_pre_ffn, w_router, b_router, w_gate, b_gate, w_up, b_up, w_down, b_down, g_post_ffn):
    B = x.shape[0]
    c_pad = jnp.pad(c, ((0, -B % 8), (0, 0)))
    h = x
    for l in range(w_ada.shape[0]):
        h = _layer(h, c_pad, w_ada[l], b_ada[l], g_pre_mix[l], w_in[l], g_sgu_v[l], w_s[l], b_s[l], w_gk2[l], b_gk[l],
                   g_gla_out[l], w_out[l], g_post_mix[l], g_pre_ffn[l], w_router[l], b_router[l], w_gate[l], b_gate[l],
                   w_up[l], b_up[l], w_down[l], b_down[l], g_post_ffn[l])
    return h
```

```python
import functools

import jax
import jax.numpy as jnp
from jax import lax
from jax.experimental import pallas as pl
from jax.experimental.pallas import tpu as pltpu
from jax.experimental.pallas import tpu_sc as plsc

F32 = jnp.float32
BF16 = jnp.bfloat16

D_MODEL = 1024
D_SGU = 512
SGU_HEADS = 8
SGU_HEAD_DIM = 64
SGU_CHUNK = 128
D_GLA = 512
GLA_HEADS = 4
GLA_DK = 256
GLA_HEAD_K = 64
GLA_HEAD_V = 128
GLA_GATE_RANK = 16
GLA_TAU = 16.0
GLA_CHUNK = 64
N_EXPERTS = 32
TOP_K = 4
SWIGLU_LIMIT = 7.0
SWIGLU_ALPHA = 1.702
EPS = 1e-6

LANES = 128
D_IN_MAIN = 2 * D_SGU + 2 * GLA_DK + 2 * D_GLA
D_IN_PAD = D_IN_MAIN + LANES

MIX_TM = 256
MOE_BM = 512
CMB_TM = 256
N_GROUPS = 2
SC_CORES = 2
SC_SUBCORES = 16
SC_WORKERS = SC_CORES * SC_SUBCORES
SC_CH = 64
NEG_BIG = -1e30

VMEM_LIMIT = 56 * 1024 * 1024


def _ada_kernel(c_ref, w_ref, b_ref, o_ref):
    c = c_ref[...]
    sc = c * jax.nn.sigmoid(c)
    o_ref[...] = jnp.dot(sc.astype(BF16), w_ref[...].astype(BF16), preferred_element_type=F32) + b_ref[...]


def _ada(c_pad, w_ada, b_ada):
    rows, d = c_pad.shape
    n = w_ada.shape[1]
    tn = 1536
    return pl.pallas_call(
        _ada_kernel,
        grid=(n // tn,),
        in_specs=[pl.BlockSpec((rows, d), lambda j: (0, 0)),
                  pl.BlockSpec((d, tn), lambda j: (0, j)),
                  pl.BlockSpec((1, tn), lambda j: (0, j))],
        out_specs=pl.BlockSpec((rows, tn), lambda j: (0, j)),
        out_shape=jax.ShapeDtypeStruct((rows, n), F32),
        compiler_params=pltpu.CompilerParams(dimension_semantics=("arbitrary",), vmem_limit_bytes=VMEM_LIMIT),
        name="ada",
    )(c_pad, w_ada, b_ada)


def _rms(x):
    return x * lax.rsqrt(jnp.mean(x * x, axis=-1, keepdims=True) + EPS)


def _gelu_tanh(x):
    return 0.5 * x * (1.0 + jnp.tanh(0.7978845608028654 * (x + 0.044715 * (x * x * x))))


def _log_sigmoid(z):
    return jnp.minimum(z, 0.0) - jnp.log(1.0 + jnp.exp(-jnp.abs(z)))


def _dot(a, b):
    return jnp.dot(a, b, preferred_element_type=F32)


def _dot_nt(a, b):
    return lax.dot_general(a, b, (((1,), (1,)), ((), ())), preferred_element_type=F32)


def _dot_tn(a, b):
    return lax.dot_general(a, b, (((0,), (0,)), ((), ())), preferred_element_type=F32)


def _pack_rows(x):
    n = x.shape[1] // 2
    lo = lax.bitcast_convert_type(x[:, :n].astype(BF16).astype(F32), jnp.uint32)
    hi = lax.bitcast_convert_type(x[:, n:].astype(BF16).astype(F32), jnp.uint32)
    return (lo >> 16) | (hi & jnp.uint32(0xFFFF0000))


def _unpack_rows(p):
    lo = lax.bitcast_convert_type(p << 16, F32)
    hi = lax.bitcast_convert_type(p & jnp.uint32(0xFFFF0000), F32)
    return lo, hi


def _split_dot(l_bf, a):
    hi = a.astype(BF16)
    lo = (a - hi.astype(F32)).astype(BF16)
    return _dot(l_bf, hi) + _dot(l_bf, lo)


def _mixer_kernel(x_ref, mods_ref, g_pre_ref, w_in_ref, gmat_ref, g_sgu_ref, ws_ref, bs_ref, wgk_ref, bgk_ref,
                  g_gla_ref, w_out_ref, g_post_ref, g_ffn_ref, w_r_ref, b_r_ref,
                  h1_ref, n2_ref, topi_ref, probs_ref, rank_ref, cnt_ref,
                  st_ref, carry_ref):
    tm = x_ref.shape[1]
    b = pl.program_id(0)
    j = pl.program_id(1)

    @pl.when(j == 0)
    def _():
        st_ref[...] = jnp.zeros_like(st_ref)

    @pl.when((b == 0) & (j == 0))
    def _():
        carry_ref[...] = jnp.zeros_like(carry_ref)

    mods = mods_ref[0]
    shift1, scale1, gate1 = mods[0:1], mods[1:2], mods[2:3]
    shift2, scale2 = mods[3:4], mods[4:5]

    x = x_ref[0]
    n = _rms(x) * (g_pre_ref[...] * (1.0 + scale1)) + shift1
    nb = n.astype(BF16)

    u = _gelu_tanh(_dot(nb, w_in_ref[:, 0:D_SGU]))
    v = _gelu_tanh(_dot(nb, w_in_ref[:, D_SGU:2 * D_SGU]))
    msv = _dot((v * v).astype(BF16), gmat_ref[...])
    vh = v * lax.rsqrt(msv + EPS) * g_sgu_ref[...]

    n_sc = tm // SGU_CHUNK
    lane_c = lax.broadcasted_iota(jnp.int32, (SGU_CHUNK, LANES), 1)
    low_half = lane_c < SGU_HEAD_DIM
    w_row = lax.broadcasted_iota(jnp.int32, (SGU_CHUNK, 2 * SGU_CHUNK), 0)
    w_col = lax.broadcasted_iota(jnp.int32, (SGU_CHUNK, 2 * SGU_CHUNK), 1) & (SGU_CHUNK - 1)
    a_cols = []
    for p in range(SGU_HEADS // 2):
        wcat = jnp.where(w_row >= w_col, ws_ref[p], 0.0).astype(BF16)
        rhs = []
        for ci in range(n_sc):
            vp = vh[ci * SGU_CHUNK:(ci + 1) * SGU_CHUNK, p * LANES:(p + 1) * LANES]
            rhs.append(jnp.concatenate([jnp.where(low_half, vp, 0.0), jnp.where(low_half, 0.0, vp)],
                                       axis=0).astype(BF16))
        res = _dot(wcat, jnp.concatenate(rhs, axis=1))
        bias = bs_ref[:, p * LANES:(p + 1) * LANES]
        a_cols.append(jnp.concatenate(
            [res[:, ci * LANES:(ci + 1) * LANES] + bias for ci in range(n_sc)], axis=0))
    a_out = u * jnp.concatenate(a_cols, axis=1)

    qk = _dot(nb, w_in_ref[:, 2 * D_SGU:2 * D_SGU + 2 * GLA_DK])
    q = qk[:, :GLA_DK] * (GLA_HEAD_K ** -0.5)
    k = qk[:, GLA_DK:]
    vv = _dot(nb, w_in_ref[:, 2 * D_SGU + 2 * GLA_DK:2 * D_SGU + 2 * GLA_DK + D_GLA]).astype(BF16)
    r = _dot(nb, w_in_ref[:, D_IN_MAIN - D_GLA:D_IN_MAIN])
    g_low = _dot(nb, w_in_ref[:, D_IN_MAIN:D_IN_PAD])
    z = _dot(g_low.astype(BF16), wgk_ref[...]) + bgk_ref[...]
    log_a = _log_sigmoid(z) * (1.0 / GLA_TAU)

    t_row = lax.broadcasted_iota(jnp.int32, (tm, tm), 0)
    t_col = lax.broadcasted_iota(jnp.int32, (tm, tm), 1)
    same_chunk = (t_row >> 6) == (t_col >> 6)
    l_cum = jnp.where(same_chunk & (t_row >= t_col), 1.0, 0.0).astype(BF16)
    l_all = jnp.where(same_chunk, 1.0, 0.0).astype(BF16)
    bcum = _split_dot(l_cum, log_a)
    blast = _split_dot(l_all, log_a)
    q_s = q * jnp.exp(bcum)
    k_s = (k * jnp.exp(-bcum)).astype(BF16)
    k_dec = (k * jnp.exp(blast - bcum)).astype(BF16)
    decay = jnp.exp(blast)

    lane_g = lax.broadcasted_iota(jnp.int32, (GLA_CHUNK, LANES), 1)
    causal = (lax.broadcasted_iota(jnp.int32, (GLA_CHUNK, GLA_CHUNK), 0)
              >= lax.broadcasted_iota(jnp.int32, (GLA_CHUNK, GLA_CHUNK), 1))
    g_gla = g_gla_ref[...]
    o_rows = []
    for ci in range(tm // GLA_CHUNK):
        rs = slice(ci * GLA_CHUNK, (ci + 1) * GLA_CHUNK)
        o_heads = []
        for h in range(GLA_HEADS):
            pl_ = slice((h // 2) * LANES, (h // 2 + 1) * LANES)
            own = (lane_g >= GLA_HEAD_K) if (h % 2) else (lane_g < GLA_HEAD_K)
            qm = jnp.where(own, q_s[rs, pl_], 0.0).astype(BF16)
            v_h = vv[rs, h * GLA_HEAD_V:(h + 1) * GLA_HEAD_V]
            scores = jnp.where(causal, _dot_nt(qm, k_s[rs, pl_]), 0.0)
            intra = _dot(scores.astype(BF16), v_h)
            st = st_ref[h]
            inter = _dot_nt(qm, st.astype(BF16))
            st_ref[h] = st * decay[ci * GLA_CHUNK:ci * GLA_CHUNK + 1, pl_] + _dot_tn(v_h, k_dec[rs, pl_])
            o_heads.append(_rms(intra + inter) * g_gla)
        o_rows.append(jnp.concatenate(o_heads, axis=1))
    o = jnp.concatenate(o_rows, axis=0) * (r * jax.nn.sigmoid(r))

    mix = _dot(jnp.concatenate([a_out, o], axis=1).astype(BF16), w_out_ref[...])
    h1 = x + gate1 * (_rms(mix) * g_post_ref[...])
    h1_ref[0] = h1

    n2 = _rms(h1) * (g_ffn_ref[...] * (1.0 + scale2)) + shift2
    n2_ref[...] = _pack_rows(n2)
    logits = _dot(n2.astype(BF16), w_r_ref[...]) + b_r_ref[...]

    lane = lax.broadcasted_iota(jnp.int32, (tm, LANES), 1)
    lane_f = lane.astype(F32)
    vals = logits
    sels, tops, idxs = [], [], []
    for _ in range(TOP_K):
        m = jnp.max(vals, axis=-1, keepdims=True)
        idx = jnp.min(jnp.where(vals == m, lane_f, float(LANES)), axis=-1, keepdims=True)
        sel = lane_f == idx
        sels.append(sel)
        tops.append(m)
        idxs.append(idx)
        vals = jnp.where(sel, -jnp.inf, vals)
    es = [jnp.exp(t - tops[0]) for t in tops]
    inv = 1.0 / (es[0] + es[1] + es[2] + es[3])

    multi = jnp.zeros((tm, LANES), F32)
    for sel in sels:
        multi = multi + jnp.where(sel, 1.0, 0.0)
    l_strict = jnp.where(t_row > t_col, 1.0, 0.0).astype(BF16)
    before = _dot(l_strict, multi.astype(BF16)) + carry_ref[...]
    carry_new = carry_ref[...] + jnp.sum(multi, axis=0, keepdims=True)
    carry_ref[...] = carry_new
    cnt_ref[...] = jnp.broadcast_to(carry_new, cnt_ref.shape)

    topi = jnp.zeros((tm, LANES), F32)
    probs = jnp.zeros((tm, LANES), F32)
    rank = jnp.zeros((tm, LANES), F32)
    for kk in range(TOP_K):
        here = lane == kk
        topi = jnp.where(here, idxs[kk], topi)
        probs = jnp.where(here, es[kk] * inv, probs)
        rank = jnp.where(here, jnp.sum(jnp.where(sels[kk], before, 0.0), axis=-1, keepdims=True), rank)
    topi_ref[...] = topi.astype(jnp.int32)
    probs_ref[...] = probs
    rank_ref[...] = rank.astype(jnp.int32)


def _const_spec(shape):
    return pl.BlockSpec(shape, lambda b, j: (0,) * len(shape))


def _mixer(b0, B, x, mods, g_pre, w_in_p, gmat, g_sgu, ws_cat, bs_exp, wgk_p, bgk, g_gla, w_out_b, g_post, g_ffn, w_r_p,
           b_r_p):
    _, S, D = x.shape
    T = B * S
    tm = MIX_TM
    nj = S // tm
    tok_spec = pl.BlockSpec((tm, LANES), lambda b, j: (b * nj + j, 0))
    return pl.pallas_call(
        _mixer_kernel,
        grid=(B, nj),
        in_specs=[pl.BlockSpec((1, tm, D), lambda b, j: (b + b0, j, 0)),
                  pl.BlockSpec((1, 6, D), lambda b, j: (b + b0, 0, 0)),
                  _const_spec((1, D)), _const_spec(w_in_p.shape), _const_spec(gmat.shape), _const_spec((1, D_SGU)),
                  _const_spec(ws_cat.shape), _const_spec(bs_exp.shape), _const_spec(wgk_p.shape),
                  _const_spec((1, GLA_DK)), _const_spec((1, GLA_HEAD_V)), _const_spec(w_out_b.shape),
                  _const_spec((1, D)), _const_spec((1, D)), _const_spec(w_r_p.shape), _const_spec((1, LANES))],
        out_specs=[pl.BlockSpec((1, tm, D), lambda b, j: (b, j, 0)),
                   pl.BlockSpec((tm, D // 2), lambda b, j: (b * nj + j, 0)),
                   tok_spec, tok_spec, tok_spec,
                   pl.BlockSpec((8, LANES), lambda b, j: (0, 0))],
        out_shape=[jax.ShapeDtypeStruct((B, S, D), F32),
                   jax.ShapeDtypeStruct((T, D // 2), jnp.uint32),
                   jax.ShapeDtypeStruct((T, LANES), jnp.int32),
                   jax.ShapeDtypeStruct((T, LANES), F32),
                   jax.ShapeDtypeStruct((T, LANES), jnp.int32),
                   jax.ShapeDtypeStruct((8, LANES), F32)],
        scratch_shapes=[pltpu.VMEM((GLA_HEADS, GLA_HEAD_V, LANES), F32),
                        pltpu.VMEM((1, LANES), F32)],
        compiler_params=pltpu.CompilerParams(dimension_semantics=("arbitrary", "arbitrary"),
                                             vmem_limit_bytes=VMEM_LIMIT),
        name="mixer",
    )(x, mods, g_pre, w_in_p, gmat, g_sgu, ws_cat, bs_exp, wgk_p, bgk, g_gla, w_out_b, g_post, g_ffn, w_r_p, b_r_p)


def _sc_dispatch(rows, dest3, n_out):
    T, D = rows.shape
    n_chunks, top_k, ch = dest3.shape
    cpw = n_chunks // SC_WORKERS
    mesh = plsc.VectorSubcoreMesh(core_axis_name="c", subcore_axis_name="s")

    @functools.partial(
        pl.kernel, mesh=mesh,
        out_type=jax.ShapeDtypeStruct((n_out, D), rows.dtype),
        scratch_types=[pltpu.VMEM((top_k, ch), jnp.int32), pltpu.VMEM((ch, D), rows.dtype)],
    )
    def k(rows_hbm, dest_hbm, out_hbm, idx_v, rows_v):
        wid = lax.axis_index("s") * SC_CORES + lax.axis_index("c")

        @pl.loop(0, cpw)
        def _(i):
            c = wid * cpw + i
            pltpu.sync_copy(dest_hbm.at[c], idx_v)
            pltpu.sync_copy(rows_hbm.at[pl.ds(c * ch, ch)], rows_v)
            for kk in range(top_k):
                pltpu.sync_copy(rows_v, out_hbm.at[idx_v.at[kk]])

    return k(rows, dest3)


def _sc_gather(src, idx2):
    _, D = src.shape
    n_chunks, ch = idx2.shape
    cpw = n_chunks // SC_WORKERS
    mesh = plsc.VectorSubcoreMesh(core_axis_name="c", subcore_axis_name="s")

    @functools.partial(
        pl.kernel, mesh=mesh,
        out_type=jax.ShapeDtypeStruct((n_chunks * ch, D), src.dtype),
        scratch_types=[pltpu.VMEM((ch,), jnp.int32), pltpu.VMEM((ch, D), src.dtype)],
    )
    def k(src_hbm, idx_hbm, out_hbm, idx_v, rows_v):
        wid = lax.axis_index("s") * SC_CORES + lax.axis_index("c")

        @pl.loop(0, cpw)
        def _(i):
            c = wid * cpw + i
            pltpu.sync_copy(idx_hbm.at[c], idx_v)
            pltpu.sync_copy(src_hbm.at[idx_v], rows_v)
            pltpu.sync_copy(rows_v, out_hbm.at[pl.ds(c * ch, ch)])

    return k(src, idx2)


def _expert_kernel(be_ref, first_ref, last_ref, nxt_ref, slot_ref, nu_ref,
                   x_ref, wg_hbm, bg_ref, wu_hbm, bu_ref, wd_hbm, bd_ref, o_ref,
                   stage_ref, wbf_ref, sem):
    i = pl.program_id(0)
    active = i < nu_ref[0]
    has_next = nxt_ref[i] >= 0

    def weight_copies(e):
        return [pltpu.make_async_copy(w.at[e], stage_ref.at[m], sem.at[m])
                for m, w in enumerate((wg_hbm, wu_hbm, wd_hbm))]

    def cast_stage_into(slot):
        for m in range(3):
            wbf_ref[slot, m] = stage_ref[m].astype(BF16)

    @pl.when(i == 0)
    def _():
        for cp in weight_copies(be_ref[0]):
            cp.start()
        for cp in weight_copies(be_ref[0]):
            cp.wait()
        cast_stage_into(0)

    @pl.when(active & (first_ref[i] == 1) & has_next)
    def _():
        for cp in weight_copies(nxt_ref[i]):
            cp.start()

    @pl.when(active)
    def _():
        slot = slot_ref[i]
        x_lo, x_hi = _unpack_rows(x_ref[...])
        xb = jnp.concatenate([x_lo, x_hi], axis=1).astype(BF16)
        g = jnp.minimum(_dot(xb, wbf_ref[slot, 0]) + bg_ref[0], SWIGLU_LIMIT)
        u = jnp.clip(_dot(xb, wbf_ref[slot, 1]) + bu_ref[0], -SWIGLU_LIMIT, SWIGLU_LIMIT)
        hdn = (u + 1.0) * (g * jax.nn.sigmoid(SWIGLU_ALPHA * g))
        o_ref[...] = _pack_rows(_dot(hdn.astype(BF16), wbf_ref[slot, 2]) + bd_ref[0])

    @pl.when(active & (last_ref[i] == 1) & has_next)
    def _():
        for cp in weight_copies(nxt_ref[i]):
            cp.wait()
        cast_stage_into(1 - slot_ref[i])


def _expert_schedule(padded, n_blocks, bm):
    pend = jnp.cumsum(padded)
    ids = jnp.arange(N_EXPERTS, dtype=jnp.int32)
    idx = jnp.arange(n_blocks, dtype=jnp.int32)
    n_used = (pend[-1] // bm).astype(jnp.int32)
    block_e = jnp.minimum(jnp.sum((pend[None, :] <= (idx * bm)[:, None]).astype(jnp.int32), axis=1), N_EXPERTS - 1)
    onehot = block_e[:, None] == ids[None, :]

    def per_block(table):
        return jnp.sum(jnp.where(onehot, table[None, :], 0), axis=1)

    start_blk = (pend - padded) // bm
    first = (idx == per_block(start_blk)) & (idx < n_used)
    last = (idx == per_block(pend // bm) - 1) & (idx < n_used)
    later = (ids[None, :] > ids[:, None]) & (padded[None, :] > 0)
    nxt_e = jnp.min(jnp.where(later, ids[None, :], N_EXPERTS), axis=1)
    nxt_e = jnp.where(nxt_e == N_EXPERTS, -1, nxt_e)
    order = jnp.cumsum((padded > 0).astype(jnp.int32)) - 1
    as_i32 = lambda a: a.astype(jnp.int32)
    return (block_e, as_i32(first), as_i32(last), per_block(nxt_e), per_block(order) & 1, n_used.reshape(1))


def _experts(schedule, xs, w_gate, b_gate, w_up, b_up, w_down, b_down):
    P, half = xs.shape
    bm = MOE_BM
    n_blocks = P // bm
    E, D, DE = w_gate.shape

    def row_map(i, be, first, last, nxt, slot, nu):
        return (jnp.minimum(i, nu[0] - 1), 0)

    def b_map(i, be, first, last, nxt, slot, nu):
        return (be[jnp.minimum(i, nu[0] - 1)], 0, 0)

    hbm = pl.BlockSpec(memory_space=pl.ANY)
    grid_spec = pltpu.PrefetchScalarGridSpec(
        num_scalar_prefetch=6,
        grid=(n_blocks,),
        in_specs=[pl.BlockSpec((bm, half), row_map),
                  hbm, pl.BlockSpec((1, 1, DE), b_map),
                  hbm, pl.BlockSpec((1, 1, DE), b_map),
                  hbm, pl.BlockSpec((1, 1, D), b_map)],
        out_specs=pl.BlockSpec((bm, half), row_map),
        scratch_shapes=[pltpu.VMEM((3, D, DE), F32),
                        pltpu.VMEM((2, 3, D, DE), BF16),
                        pltpu.SemaphoreType.DMA((3,))],
    )
    return pl.pallas_call(
        _expert_kernel,
        grid_spec=grid_spec,
        out_shape=jax.ShapeDtypeStruct((P, half), jnp.uint32),
        compiler_params=pltpu.CompilerParams(dimension_semantics=("arbitrary",), vmem_limit_bytes=VMEM_LIMIT),
        name="experts",
    )(*schedule, xs, w_gate, b_gate.reshape(E, 1, DE), w_up, b_up.reshape(E, 1, DE),
      w_down, b_down.reshape(E, 1, D))


def _combine_kernel(h1_ref, og_ref, probs_ref, mods_ref, g_ref, *rest):
    o_ref = rest[-1]
    tm = h1_ref.shape[1]
    probs = probs_ref[...]
    lane = lax.broadcasted_iota(jnp.int32, (tm, LANES), 1)
    half = og_ref.shape[2]
    f_lo = jnp.zeros((tm, half), F32)
    f_hi = jnp.zeros((tm, half), F32)
    for kk in range(TOP_K):
        pk = jnp.sum(jnp.where(lane == kk, probs, 0.0), axis=-1, keepdims=True)
        o_lo, o_hi = _unpack_rows(og_ref[kk])
        f_lo = f_lo + pk * o_lo
        f_hi = f_hi + pk * o_hi
    f = jnp.concatenate([f_lo, f_hi], axis=1)
    gate2 = mods_ref[0][5:6]
    o_ref[0] = h1_ref[0] + gate2 * (_rms(f) * g_ref[...])


def _combine(b0, b_total, h1, og, probs, mods, g_post_ffn, out_so_far):
    B, S, D = h1.shape
    tm = CMB_TM
    nj = S // tm
    in_specs = [pl.BlockSpec((1, tm, D), lambda b, j: (b, j, 0)),
                pl.BlockSpec((TOP_K, tm, D // 2), lambda b, j: (0, b * nj + j, 0)),
                pl.BlockSpec((tm, LANES), lambda b, j: (b * nj + j, 0)),
                pl.BlockSpec((1, 6, D), lambda b, j: (b + b0, 0, 0)),
                pl.BlockSpec((1, D), lambda b, j: (0, 0))]
    args = [h1, og, probs, mods, g_post_ffn]
    aliases = {}
    if out_so_far is not None:
        in_specs.append(pl.BlockSpec(memory_space=pl.ANY))
        args.append(out_so_far)
        aliases = {len(args) - 1: 0}
    return pl.pallas_call(
        _combine_kernel,
        grid=(B, nj),
        in_specs=in_specs,
        out_specs=pl.BlockSpec((1, tm, D), lambda b, j: (b + b0, j, 0)),
        out_shape=jax.ShapeDtypeStruct((b_total, S, D), F32),
        input_output_aliases=aliases,
        compiler_params=pltpu.CompilerParams(dimension_semantics=("arbitrary", "arbitrary"),
                                             vmem_limit_bytes=VMEM_LIMIT),
        name="combine",
    )(*args)


def _layer(h, c_pad, w_ada, b_ada, g_pre_mix, w_in, g_sgu_v, w_s, b_s, w_gk2, b_gk, g_gla_out, w_out, g_post_mix,
           g_pre_ffn, w_router, b_router, w_gate, b_gate, w_up, b_up, w_down, b_down, g_post_ffn):
    B, S, D = h.shape

    mods = _ada(c_pad, w_ada, b_ada.reshape(1, -1))[:B].reshape(B, 6, D)

    w_in_p = jnp.pad(w_in, ((0, 0), (0, D_IN_PAD - w_in.shape[1]))).astype(BF16)
    head_of = jnp.arange(D_SGU) // SGU_HEAD_DIM
    gmat = jnp.where(head_of[:, None] == head_of[None, :], 1.0 / SGU_HEAD_DIM, 0.0).astype(BF16)
    ws_cat = w_s.reshape(SGU_HEADS // 2, 2, SGU_CHUNK, SGU_CHUNK).transpose(0, 2, 1, 3).reshape(
        SGU_HEADS // 2, SGU_CHUNK, 2 * SGU_CHUNK)
    bs_exp = jnp.repeat(b_s.T, SGU_HEAD_DIM, axis=1)
    wgk_p = jnp.pad(w_gk2, ((0, LANES - GLA_GATE_RANK), (0, 0))).astype(BF16)
    w_r_p = jnp.pad(w_router, ((0, 0), (0, LANES - N_EXPERTS))).astype(BF16)
    b_r_p = jnp.concatenate([b_router, jnp.full((LANES - N_EXPERTS,), NEG_BIG, F32)]).reshape(1, LANES)

    w_out_b = w_out.astype(BF16)

    n_groups = N_GROUPS if B % N_GROUPS == 0 else 1
    bg = B // n_groups
    tg = bg * S
    result = None
    for gi in range(n_groups):
        b0 = gi * bg
        h1, n2, topi, probs, rank, cnt = _mixer(
            b0, bg, h, mods, g_pre_mix.reshape(1, D), w_in_p, gmat, g_sgu_v.reshape(1, D_SGU), ws_cat, bs_exp, wgk_p,
            b_gk.reshape(1, GLA_DK), g_gla_out.reshape(1, GLA_HEAD_V), w_out_b, g_post_mix.reshape(1, D),
            g_pre_ffn.reshape(1, D), w_r_p, b_r_p)

        bm = MOE_BM
        counts = cnt[0, :N_EXPERTS].astype(jnp.int32)
        padded = ((counts + bm - 1) // bm) * bm
        pend = jnp.cumsum(padded)
        pstart = pend - padded
        e_kt = topi[:, :TOP_K].T
        onehot = e_kt[:, :, None] == jnp.arange(N_EXPERTS, dtype=jnp.int32)[None, None, :]
        dest_kt = rank[:, :TOP_K].T + jnp.sum(jnp.where(onehot, pstart[None, None, :], 0), axis=-1)
        n_blocks = tg * TOP_K // bm + N_EXPERTS
        schedule = _expert_schedule(padded, n_blocks, bm)

        dest3 = dest_kt.reshape(TOP_K, tg // SC_CH, SC_CH).transpose(1, 0, 2)
        xs = _sc_dispatch(n2, dest3, n_blocks * bm)
        out = _experts(schedule, xs, w_gate, b_gate, w_up, b_up, w_down, b_down)
        og = _sc_gather(out, dest_kt.reshape(tg * TOP_K // SC_CH, SC_CH)).reshape(TOP_K, tg, D // 2)
        result = _combine(b0, B, h1, og, probs, mods, g_post_ffn.reshape(1, D), result)
    return result


def kernel(x, c, w_ada, b_ada, g_pre_mix, w_in, g_sgu_v, w_s, b_s, w_gk2, b_gk, g_gla_out, w_out, g_post_mix, g_pre_ffn, w_router, b_router, w_gate, b_gate, w_up, b_up, w_down, b_down, g_post_ffn):
    B = x.shape[0]
    c_pad = jnp.pad(c, ((0, -B % 8), (0, 0)))
    h = x
    for l in range(w_ada.shape[0]):
        h = _layer(h, c_pad, w_ada[l], b_ada[l], g_pre_mix[l], w_in[l], g_sgu_v[l], w_s[l], b_s[l], w_gk2[l], b_gk[l],
                   g_gla_out[l], w_out[l], g_post_mix[l], g_pre_ffn[l], w_router[l], b_router[l], w_gate[l], b_gate[l],
                   w_up[l], b_up[l], w_down[l], b_down[l], g_post_ffn[l])
    return h
```

```python
import functools

import jax
import jax.numpy as jnp
from jax import lax
from jax.experimental import pallas as pl
from jax.experimental.pallas import tpu as pltpu
from jax.experimental.pallas import tpu_sc as plsc

F32 = jnp.float32
BF16 = jnp.bfloat16

D_MODEL = 1024
D_SGU = 512
SGU_HEADS = 8
SGU_HEAD_DIM = 64
SGU_CHUNK = 128
D_GLA = 512
GLA_HEADS = 4
GLA_DK = 256
GLA_HEAD_K = 64
GLA_HEAD_V = 128
GLA_GATE_RANK = 16
GLA_TAU = 16.0
GLA_CHUNK = 64
N_EXPERTS = 32
TOP_K = 4
SWIGLU_LIMIT = 7.0
SWIGLU_ALPHA = 1.702
EPS = 1e-6

LANES = 128
D_IN_MAIN = 2 * D_SGU + 2 * GLA_DK + 2 * D_GLA
D_IN_PAD = D_IN_MAIN + LANES

MIX_TM = 512
MIX_TILE = 256
MIX_SKEW = 1
MIX_SUB = 256
MOE_BM = 512
CMB_TM = 256
N_GROUPS = 2
SC_CORES = 2
SC_SUBCORES = 16
SC_WORKERS = SC_CORES * SC_SUBCORES
SC_CH = 64
NEG_BIG = -1e30

VMEM_LIMIT = 56 * 1024 * 1024


def _ada_kernel(c_ref, w_ref, b_ref, o_ref):
    c = c_ref[...]
    sc = c * jax.nn.sigmoid(c)
    o_ref[...] = jnp.dot(sc.astype(BF16), w_ref[...].astype(BF16), preferred_element_type=F32) + b_ref[...]


def _ada(c_pad, w_ada, b_ada):
    rows, d = c_pad.shape
    n = w_ada.shape[1]
    tn = 1536
    return pl.pallas_call(
        _ada_kernel,
        grid=(n // tn,),
        in_specs=[pl.BlockSpec((rows, d), lambda j: (0, 0)),
                  pl.BlockSpec((d, tn), lambda j: (0, j)),
                  pl.BlockSpec((1, tn), lambda j: (0, j))],
        out_specs=pl.BlockSpec((rows, tn), lambda j: (0, j)),
        out_shape=jax.ShapeDtypeStruct((rows, n), F32),
        compiler_params=pltpu.CompilerParams(dimension_semantics=("arbitrary",), vmem_limit_bytes=VMEM_LIMIT),
        name="ada",
    )(c_pad, w_ada, b_ada)


def _rms(x):
    return x * lax.rsqrt(jnp.mean(x * x, axis=-1, keepdims=True) + EPS)


def _gelu_tanh(x):
    return 0.5 * x * (1.0 + jnp.tanh(0.7978845608028654 * (x + 0.044715 * (x * x * x))))


def _log_sigmoid(z):
    return jnp.minimum(z, 0.0) - jnp.log(1.0 + jnp.exp(-jnp.abs(z)))


def _dot(a, b):
    return jnp.dot(a, b, preferred_element_type=F32)


def _dot_nt(a, b):
    return lax.dot_general(a, b, (((1,), (1,)), ((), ())), preferred_element_type=F32)


def _dot_tn(a, b):
    return lax.dot_general(a, b, (((0,), (0,)), ((), ())), preferred_element_type=F32)


def _pack_rows(x):
    n = x.shape[1] // 2
    lo = lax.bitcast_convert_type(x[:, :n].astype(BF16).astype(F32), jnp.uint32)
    hi = lax.bitcast_convert_type(x[:, n:].astype(BF16).astype(F32), jnp.uint32)
    return (lo >> 16) | (hi & jnp.uint32(0xFFFF0000))


def _unpack_rows(p):
    lo = lax.bitcast_convert_type(p << 16, F32)
    hi = lax.bitcast_convert_type(p & jnp.uint32(0xFFFF0000), F32)
    return lo, hi


def _split_dot(l_bf, a):
    hi = a.astype(BF16)
    lo = (a - hi.astype(F32)).astype(BF16)
    return _dot(l_bf, hi) + _dot(l_bf, lo)


def _mixer_kernel(x_ref, mods_ref, g_pre_ref, w_in_ref, gmat_ref, g_sgu_ref, ws_ref, bs_ref, wgk_ref, bgk_ref,
                  g_gla_ref, w_out_ref, g_post_ref, g_ffn_ref, w_r_ref, b_r_ref,
                  h1_ref, n2_ref, topi_ref, probs_ref, rank_ref, cnt_ref,
                  st_ref, carry_ref):
    b = pl.program_id(0)
    j = pl.program_id(1)

    @pl.when(j == 0)
    def _():
        st_ref[...] = jnp.zeros_like(st_ref)

    @pl.when((b == 0) & (j == 0))
    def _():
        carry_ref[...] = jnp.zeros_like(carry_ref)

    carry_box = [carry_ref[...]]
    tiles = [_mixer_tile(slice(s * MIX_TILE, (s + 1) * MIX_TILE), carry_box,
                         x_ref, mods_ref, g_pre_ref, w_in_ref, gmat_ref, g_sgu_ref, ws_ref, bs_ref, wgk_ref, bgk_ref,
                         g_gla_ref, w_out_ref, g_post_ref, g_ffn_ref, w_r_ref, b_r_ref,
                         h1_ref, n2_ref, topi_ref, probs_ref, rank_ref, st_ref)
             for s in range(x_ref.shape[1] // MIX_TILE)]
    alive = [True] * len(tiles)
    t = 0
    while any(alive):
        for s, tile in enumerate(tiles):
            if alive[s] and t >= s * MIX_SKEW:
                alive[s] = next(tile, None) is not None
        t += 1
    carry_ref[...] = carry_box[0]
    cnt_ref[...] = jnp.broadcast_to(carry_box[0], cnt_ref.shape)


def _mixer_tile(rows, carry_box, x_ref, mods_ref, g_pre_ref, w_in_ref, gmat_ref, g_sgu_ref, ws_ref, bs_ref, wgk_ref,
                bgk_ref, g_gla_ref, w_out_ref, g_post_ref, g_ffn_ref, w_r_ref, b_r_ref,
                h1_ref, n2_ref, topi_ref, probs_ref, rank_ref, st_ref):
    tm = rows.stop - rows.start
    mods = mods_ref[0]
    shift1, scale1, gate1 = mods[0:1], mods[1:2], mods[2:3]
    shift2, scale2 = mods[3:4], mods[4:5]

    x = x_ref[0, rows, :]
    n = _rms(x) * (g_pre_ref[...] * (1.0 + scale1)) + shift1
    nb = n.astype(BF16)
    yield True

    pu = _dot(nb, w_in_ref[:, 0:D_SGU])
    pv = _dot(nb, w_in_ref[:, D_SGU:2 * D_SGU])
    yield True

    u = _gelu_tanh(pu)
    v = _gelu_tanh(pv)
    msv = _dot((v * v).astype(BF16), gmat_ref[...])
    vh = v * lax.rsqrt(msv + EPS) * g_sgu_ref[...]
    yield True

    qk = _dot(nb, w_in_ref[:, 2 * D_SGU:2 * D_SGU + 2 * GLA_DK])
    vv = _dot(nb, w_in_ref[:, 2 * D_SGU + 2 * GLA_DK:2 * D_SGU + 2 * GLA_DK + D_GLA]).astype(BF16)
    r = _dot(nb, w_in_ref[:, D_IN_MAIN - D_GLA:D_IN_MAIN])
    g_low = _dot(nb, w_in_ref[:, D_IN_MAIN:D_IN_PAD])
    yield True

    n_sc = tm // SGU_CHUNK
    lane_c = lax.broadcasted_iota(jnp.int32, (SGU_CHUNK, LANES), 1)
    low_half = lane_c < SGU_HEAD_DIM
    w_row = lax.broadcasted_iota(jnp.int32, (SGU_CHUNK, 2 * SGU_CHUNK), 0)
    w_col = lax.broadcasted_iota(jnp.int32, (SGU_CHUNK, 2 * SGU_CHUNK), 1) & (SGU_CHUNK - 1)
    a_cols = []
    for p in range(SGU_HEADS // 2):
        wcat = jnp.where(w_row >= w_col, ws_ref[p], 0.0).astype(BF16)
        rhs = []
        for ci in range(n_sc):
            vp = vh[ci * SGU_CHUNK:(ci + 1) * SGU_CHUNK, p * LANES:(p + 1) * LANES]
            rhs.append(jnp.concatenate([jnp.where(low_half, vp, 0.0), jnp.where(low_half, 0.0, vp)],
                                       axis=0).astype(BF16))
        res = _dot(wcat, jnp.concatenate(rhs, axis=1))
        bias = bs_ref[:, p * LANES:(p + 1) * LANES]
        a_cols.append(jnp.concatenate(
            [res[:, ci * LANES:(ci + 1) * LANES] + bias for ci in range(n_sc)], axis=0))
    a_out = u * jnp.concatenate(a_cols, axis=1)

    q = qk[:, :GLA_DK] * (GLA_HEAD_K ** -0.5)
    k = qk[:, GLA_DK:]
    z = _dot(g_low.astype(BF16), wgk_ref[...]) + bgk_ref[...]
    log_a = _log_sigmoid(z) * (1.0 / GLA_TAU)

    sub = min(tm, MIX_SUB)
    subs = [slice(s * sub, (s + 1) * sub) for s in range(tm // sub)]
    t_row = lax.broadcasted_iota(jnp.int32, (sub, sub), 0)
    t_col = lax.broadcasted_iota(jnp.int32, (sub, sub), 1)
    same_chunk = (t_row >> 6) == (t_col >> 6)
    l_cum = jnp.where(same_chunk & (t_row >= t_col), 1.0, 0.0).astype(BF16)
    l_all = jnp.where(same_chunk, 1.0, 0.0).astype(BF16)
    bcum = jnp.concatenate([_split_dot(l_cum, log_a[s]) for s in subs], axis=0)
    blast = jnp.concatenate([_split_dot(l_all, log_a[s]) for s in subs], axis=0)
    q_s = q * jnp.exp(bcum)
    k_s = (k * jnp.exp(-bcum)).astype(BF16)
    k_dec = (k * jnp.exp(blast - bcum)).astype(BF16)
    decay = jnp.exp(blast)
    yield True

    lane_g = lax.broadcasted_iota(jnp.int32, (GLA_CHUNK, LANES), 1)
    causal = (lax.broadcasted_iota(jnp.int32, (GLA_CHUNK, GLA_CHUNK), 0)
              >= lax.broadcasted_iota(jnp.int32, (GLA_CHUNK, GLA_CHUNK), 1))
    g_gla = g_gla_ref[...]
    o_rows = []
    for ci in range(tm // GLA_CHUNK):
        rs = slice(ci * GLA_CHUNK, (ci + 1) * GLA_CHUNK)
        o_heads = []
        for h in range(GLA_HEADS):
            pl_ = slice((h // 2) * LANES, (h // 2 + 1) * LANES)
            own = (lane_g >= GLA_HEAD_K) if (h % 2) else (lane_g < GLA_HEAD_K)
            qm = jnp.where(own, q_s[rs, pl_], 0.0).astype(BF16)
            v_h = vv[rs, h * GLA_HEAD_V:(h + 1) * GLA_HEAD_V]
            scores = jnp.where(causal, _dot_nt(qm, k_s[rs, pl_]), 0.0)
            intra = _dot(scores.astype(BF16), v_h)
            st = st_ref[h]
            inter = _dot_nt(qm, st.astype(BF16))
            st_ref[h] = st * decay[ci * GLA_CHUNK:ci * GLA_CHUNK + 1, pl_] + _dot_tn(v_h, k_dec[rs, pl_])
            o_heads.append(_rms(intra + inter) * g_gla)
        o_rows.append(jnp.concatenate(o_heads, axis=1))
    o = jnp.concatenate(o_rows, axis=0) * (r * jax.nn.sigmoid(r))
    mix_in = jnp.concatenate([a_out, o], axis=1).astype(BF16)
    yield True

    mix = _dot(mix_in, w_out_ref[...])
    yield True

    h1 = x + gate1 * (_rms(mix) * g_post_ref[...])
    h1_ref[0, rows, :] = h1
    n2 = _rms(h1) * (g_ffn_ref[...] * (1.0 + scale2)) + shift2
    n2_ref[rows, :] = _pack_rows(n2)
    n2b = n2.astype(BF16)
    yield True

    logits = _dot(n2b, w_r_ref[...]) + b_r_ref[...]
    lane = lax.broadcasted_iota(jnp.int32, (tm, LANES), 1)
    lane_f = lane.astype(F32)
    vals = logits
    sels, tops, idxs = [], [], []
    for _ in range(TOP_K):
        m = jnp.max(vals, axis=-1, keepdims=True)
        idx = jnp.min(jnp.where(vals == m, lane_f, float(LANES)), axis=-1, keepdims=True)
        sel = lane_f == idx
        sels.append(sel)
        tops.append(m)
        idxs.append(idx)
        vals = jnp.where(sel, -jnp.inf, vals)
    es = [jnp.exp(t - tops[0]) for t in tops]
    inv = 1.0 / (es[0] + es[1] + es[2] + es[3])

    multi = jnp.zeros((tm, LANES), F32)
    for sel in sels:
        multi = multi + jnp.where(sel, 1.0, 0.0)
    l_strict = jnp.where(t_row > t_col, 1.0, 0.0).astype(BF16)
    carry = carry_box[0]
    before = []
    for s in subs:
        before.append(_dot(l_strict, multi[s].astype(BF16)) + carry)
        carry = carry + jnp.sum(multi[s], axis=0, keepdims=True)
    before = jnp.concatenate(before, axis=0)
    carry_box[0] = carry

    topi = jnp.zeros((tm, LANES), F32)
    probs = jnp.zeros((tm, LANES), F32)
    rank = jnp.zeros((tm, LANES), F32)
    for kk in range(TOP_K):
        here = lane == kk
        topi = jnp.where(here, idxs[kk], topi)
        probs = jnp.where(here, es[kk] * inv, probs)
        rank = jnp.where(here, jnp.sum(jnp.where(sels[kk], before, 0.0), axis=-1, keepdims=True), rank)
    topi_ref[rows, :] = topi.astype(jnp.int32)
    probs_ref[rows, :] = probs
    rank_ref[rows, :] = rank.astype(jnp.int32)


def _const_spec(shape):
    return pl.BlockSpec(shape, lambda b, j: (0,) * len(shape))


def _mixer(b0, B, x, mods, g_pre, w_in_p, gmat, g_sgu, ws_cat, bs_exp, wgk_p, bgk, g_gla, w_out_b, g_post, g_ffn, w_r_p,
           b_r_p):
    _, S, D = x.shape
    T = B * S
    tm = MIX_TM
    nj = S // tm
    tok_spec = pl.BlockSpec((tm, LANES), lambda b, j: (b * nj + j, 0))
    return pl.pallas_call(
        _mixer_kernel,
        grid=(B, nj),
        in_specs=[pl.BlockSpec((1, tm, D), lambda b, j: (b + b0, j, 0)),
                  pl.BlockSpec((1, 6, D), lambda b, j: (b + b0, 0, 0)),
                  _const_spec((1, D)), _const_spec(w_in_p.shape), _const_spec(gmat.shape), _const_spec((1, D_SGU)),
                  _const_spec(ws_cat.shape), _const_spec(bs_exp.shape), _const_spec(wgk_p.shape),
                  _const_spec((1, GLA_DK)), _const_spec((1, GLA_HEAD_V)), _const_spec(w_out_b.shape),
                  _const_spec((1, D)), _const_spec((1, D)), _const_spec(w_r_p.shape), _const_spec((1, LANES))],
        out_specs=[pl.BlockSpec((1, tm, D), lambda b, j: (b, j, 0)),
                   pl.BlockSpec((tm, D // 2), lambda b, j: (b * nj + j, 0)),
                   tok_spec, tok_spec, tok_spec,
                   pl.BlockSpec((8, LANES), lambda b, j: (0, 0))],
        out_shape=[jax.ShapeDtypeStruct((B, S, D), F32),
                   jax.ShapeDtypeStruct((T, D // 2), jnp.uint32),
                   jax.ShapeDtypeStruct((T, LANES), jnp.int32),
                   jax.ShapeDtypeStruct((T, LANES), F32),
                   jax.ShapeDtypeStruct((T, LANES), jnp.int32),
                   jax.ShapeDtypeStruct((8, LANES), F32)],
        scratch_shapes=[pltpu.VMEM((GLA_HEADS, GLA_HEAD_V, LANES), F32),
                        pltpu.VMEM((1, LANES), F32)],
        compiler_params=pltpu.CompilerParams(dimension_semantics=("arbitrary", "arbitrary"),
                                             vmem_limit_bytes=VMEM_LIMIT),
        name="mixer",
    )(x, mods, g_pre, w_in_p, gmat, g_sgu, ws_cat, bs_exp, wgk_p, bgk, g_gla, w_out_b, g_post, g_ffn, w_r_p, b_r_p)


def _sc_dispatch(rows, dest3, n_out):
    T, D = rows.shape
    n_chunks, top_k, ch = dest3.shape
    cpw = n_chunks // SC_WORKERS
    mesh = plsc.VectorSubcoreMesh(core_axis_name="c", subcore_axis_name="s")

    @functools.partial(
        pl.kernel, mesh=mesh,
        out_type=jax.ShapeDtypeStruct((n_out, D), rows.dtype),
        scratch_types=[pltpu.VMEM((top_k, ch), jnp.int32), pltpu.VMEM((ch, D), rows.dtype)],
    )
    def k(rows_hbm, dest_hbm, out_hbm, idx_v, rows_v):
        wid = lax.axis_index("s") * SC_CORES + lax.axis_index("c")

        @pl.loop(0, cpw)
        def _(i):
            c = wid * cpw + i
            pltpu.sync_copy(dest_hbm.at[c], idx_v)
            pltpu.sync_copy(rows_hbm.at[pl.ds(c * ch, ch)], rows_v)
            for kk in range(top_k):
                pltpu.sync_copy(rows_v, out_hbm.at[idx_v.at[kk]])

    return k(rows, dest3)


def _sc_gather(src, idx2):
    _, D = src.shape
    n_chunks, ch = idx2.shape
    cpw = n_chunks // SC_WORKERS
    mesh = plsc.VectorSubcoreMesh(core_axis_name="c", subcore_axis_name="s")

    @functools.partial(
        pl.kernel, mesh=mesh,
        out_type=jax.ShapeDtypeStruct((n_chunks * ch, D), src.dtype),
        scratch_types=[pltpu.VMEM((ch,), jnp.int32), pltpu.VMEM((ch, D), src.dtype)],
    )
    def k(src_hbm, idx_hbm, out_hbm, idx_v, rows_v):
        wid = lax.axis_index("s") * SC_CORES + lax.axis_index("c")

        @pl.loop(0, cpw)
        def _(i):
            c = wid * cpw + i
            pltpu.sync_copy(idx_hbm.at[c], idx_v)
            pltpu.sync_copy(src_hbm.at[idx_v], rows_v)
            pltpu.sync_copy(rows_v, out_hbm.at[pl.ds(c * ch, ch)])

    return k(src, idx2)


def _expert_kernel(be_ref, first_ref, last_ref, nxt_ref, slot_ref, nu_ref,
                   x_ref, wg_hbm, bg_ref, wu_hbm, bu_ref, wd_hbm, bd_ref, o_ref,
                   stage_ref, wbf_ref, sem):
    i = pl.program_id(0)
    active = i < nu_ref[0]
    has_next = nxt_ref[i] >= 0

    def weight_copies(e):
        return [pltpu.make_async_copy(w.at[e], stage_ref.at[m], sem.at[m])
                for m, w in enumerate((wg_hbm, wu_hbm, wd_hbm))]

    def cast_stage_into(slot):
        for m in range(3):
            wbf_ref[slot, m] = stage_ref[m].astype(BF16)

    @pl.when(i == 0)
    def _():
        for cp in weight_copies(be_ref[0]):
            cp.start()
        for cp in weight_copies(be_ref[0]):
            cp.wait()
        cast_stage_into(0)

    @pl.when(active & (first_ref[i] == 1) & has_next)
    def _():
        for cp in weight_copies(nxt_ref[i]):
            cp.start()

    @pl.when(active)
    def _():
        slot = slot_ref[i]
        x_lo, x_hi = _unpack_rows(x_ref[...])
        xb = jnp.concatenate([x_lo, x_hi], axis=1).astype(BF16)
        g = jnp.minimum(_dot(xb, wbf_ref[slot, 0]) + bg_ref[0], SWIGLU_LIMIT)
        u = jnp.clip(_dot(xb, wbf_ref[slot, 1]) + bu_ref[0], -SWIGLU_LIMIT, SWIGLU_LIMIT)
        hdn = (u + 1.0) * (g * jax.nn.sigmoid(SWIGLU_ALPHA * g))
        o_ref[...] = _pack_rows(_dot(hdn.astype(BF16), wbf_ref[slot, 2]) + bd_ref[0])

    @pl.when(active & (last_ref[i] == 1) & has_next)
    def _():
        for cp in weight_copies(nxt_ref[i]):
            cp.wait()
        cast_stage_into(1 - slot_ref[i])


def _expert_schedule(padded, n_blocks, bm):
    pend = jnp.cumsum(padded)
    ids = jnp.arange(N_EXPERTS, dtype=jnp.int32)
    idx = jnp.arange(n_blocks, dtype=jnp.int32)
    n_used = (pend[-1] // bm).astype(jnp.int32)
    block_e = jnp.minimum(jnp.sum((pend[None, :] <= (idx * bm)[:, None]).astype(jnp.int32), axis=1), N_EXPERTS - 1)
    onehot = block_e[:, None] == ids[None, :]

    def per_block(table):
        return jnp.sum(jnp.where(onehot, table[None, :], 0), axis=1)

    start_blk = (pend - padded) // bm
    first = (idx == per_block(start_blk)) & (idx < n_used)
    last = (idx == per_block(pend // bm) - 1) & (idx < n_used)
    later = (ids[None, :] > ids[:, None]) & (padded[None, :] > 0)
    nxt_e = jnp.min(jnp.where(later, ids[None, :], N_EXPERTS), axis=1)
    nxt_e = jnp.where(nxt_e == N_EXPERTS, -1, nxt_e)
    order = jnp.cumsum((padded > 0).astype(jnp.int32)) - 1
    as_i32 = lambda a: a.astype(jnp.int32)
    return (block_e, as_i32(first), as_i32(last), per_block(nxt_e), per_block(order) & 1, n_used.reshape(1))


def _experts(schedule, xs, w_gate, b_gate, w_up, b_up, w_down, b_down):
    P, half = xs.shape
    bm = MOE_BM
    n_blocks = P // bm
    E, D, DE = w_gate.shape

    def row_map(i, be, first, last, nxt, slot, nu):
        return (jnp.minimum(i, nu[0] - 1), 0)

    def b_map(i, be, first, last, nxt, slot, nu):
        return (be[jnp.minimum(i, nu[0] - 1)], 0, 0)

    hbm = pl.BlockSpec(memory_space=pl.ANY)
    grid_spec = pltpu.PrefetchScalarGridSpec(
        num_scalar_prefetch=6,
        grid=(n_blocks,),
        in_specs=[pl.BlockSpec((bm, half), row_map),
                  hbm, pl.BlockSpec((1, 1, DE), b_map),
                  hbm, pl.BlockSpec((1, 1, DE), b_map),
                  hbm, pl.BlockSpec((1, 1, D), b_map)],
        out_specs=pl.BlockSpec((bm, half), row_map),
        scratch_shapes=[pltpu.VMEM((3, D, DE), F32),
                        pltpu.VMEM((2, 3, D, DE), BF16),
                        pltpu.SemaphoreType.DMA((3,))],
    )
    return pl.pallas_call(
        _expert_kernel,
        grid_spec=grid_spec,
        out_shape=jax.ShapeDtypeStruct((P, half), jnp.uint32),
        compiler_params=pltpu.CompilerParams(dimension_semantics=("arbitrary",), vmem_limit_bytes=VMEM_LIMIT),
        name="experts",
    )(*schedule, xs, w_gate, b_gate.reshape(E, 1, DE), w_up, b_up.reshape(E, 1, DE),
      w_down, b_down.reshape(E, 1, D))


def _combine_kernel(h1_ref, og_ref, probs_ref, mods_ref, g_ref, *rest):
    o_ref = rest[-1]
    tm = h1_ref.shape[1]
    probs = probs_ref[...]
    lane = lax.broadcasted_iota(jnp.int32, (tm, LANES), 1)
    half = og_ref.shape[2]
    f_lo = jnp.zeros((tm, half), F32)
    f_hi = jnp.zeros((tm, half), F32)
    for kk in range(TOP_K):
        pk = jnp.sum(jnp.where(lane == kk, probs, 0.0), axis=-1, keepdims=True)
        o_lo, o_hi = _unpack_rows(og_ref[kk])
        f_lo = f_lo + pk * o_lo
        f_hi = f_hi + pk * o_hi
    f = jnp.concatenate([f_lo, f_hi], axis=1)
    gate2 = mods_ref[0][5:6]
    o_ref[0] = h1_ref[0] + gate2 * (_rms(f) * g_ref[...])


def _combine(b0, b_total, h1, og, probs, mods, g_post_ffn, out_so_far):
    B, S, D = h1.shape
    tm = CMB_TM
    nj = S // tm
    in_specs = [pl.BlockSpec((1, tm, D), lambda b, j: (b, j, 0)),
                pl.BlockSpec((TOP_K, tm, D // 2), lambda b, j: (0, b * nj + j, 0)),
                pl.BlockSpec((tm, LANES), lambda b, j: (b * nj + j, 0)),
                pl.BlockSpec((1, 6, D), lambda b, j: (b + b0, 0, 0)),
                pl.BlockSpec((1, D), lambda b, j: (0, 0))]
    args = [h1, og, probs, mods, g_post_ffn]
    aliases = {}
    if out_so_far is not None:
        in_specs.append(pl.BlockSpec(memory_space=pl.ANY))
        args.append(out_so_far)
        aliases = {len(args) - 1: 0}
    return pl.pallas_call(
        _combine_kernel,
        grid=(B, nj),
        in_specs=in_specs,
        out_specs=pl.BlockSpec((1, tm, D), lambda b, j: (b + b0, j, 0)),
        out_shape=jax.ShapeDtypeStruct((b_total, S, D), F32),
        input_output_aliases=aliases,
        compiler_params=pltpu.CompilerParams(dimension_semantics=("arbitrary", "arbitrary"),
                                             vmem_limit_bytes=VMEM_LIMIT),
        name="combine",
    )(*args)


def _layer(h, c_pad, w_ada, b_ada, g_pre_mix, w_in, g_sgu_v, w_s, b_s, w_gk2, b_gk, g_gla_out, w_out, g_post_mix,
           g_pre_ffn, w_router, b_router, w_gate, b_gate, w_up, b_up, w_down, b_down, g_post_ffn):
    B, S, D = h.shape

    mods = _ada(c_pad, w_ada, b_ada.reshape(1, -1))[:B].reshape(B, 6, D)

    w_in_p = jnp.pad(w_in, ((0, 0), (0, D_IN_PAD - w_in.shape[1]))).astype(BF16)
    head_of = jnp.arange(D_SGU) // SGU_HEAD_DIM
    gmat = jnp.where(head_of[:, None] == head_of[None, :], 1.0 / SGU_HEAD_DIM, 0.0).astype(BF16)
    ws_cat = w_s.reshape(SGU_HEADS // 2, 2, SGU_CHUNK, SGU_CHUNK).transpose(0, 2, 1, 3).reshape(
        SGU_HEADS // 2, SGU_CHUNK, 2 * SGU_CHUNK)
    bs_exp = jnp.repeat(b_s.T, SGU_HEAD_DIM, axis=1)
    wgk_p = jnp.pad(w_gk2, ((0, LANES - GLA_GATE_RANK), (0, 0))).astype(BF16)
    w_r_p = jnp.pad(w_router, ((0, 0), (0, LANES - N_EXPERTS))).astype(BF16)
    b_r_p = jnp.concatenate([b_router, jnp.full((LANES - N_EXPERTS,), NEG_BIG, F32)]).reshape(1, LANES)

    w_out_b = w_out.astype(BF16)

    n_groups = N_GROUPS if B % N_GROUPS == 0 else 1
    bg = B // n_groups
    tg = bg * S
    result = None
    for gi in range(n_groups):
        b0 = gi * bg
        h1, n2, topi, probs, rank, cnt = _mixer(
            b0, bg, h, mods, g_pre_mix.reshape(1, D), w_in_p, gmat, g_sgu_v.reshape(1, D_SGU), ws_cat, bs_exp, wgk_p,
            b_gk.reshape(1, GLA_DK), g_gla_out.reshape(1, GLA_HEAD_V), w_out_b, g_post_mix.reshape(1, D),
            g_pre_ffn.reshape(1, D), w_r_p, b_r_p)

        bm = MOE_BM
        counts = cnt[0, :N_EXPERTS].astype(jnp.int32)
        padded = ((counts + bm - 1) // bm) * bm
        pend = jnp.cumsum(padded)
        pstart = pend - padded
        e_kt = topi[:, :TOP_K].T
        onehot = e_kt[:, :, None] == jnp.arange(N_EXPERTS, dtype=jnp.int32)[None, None, :]
        dest_kt = rank[:, :TOP_K].T + jnp.sum(jnp.where(onehot, pstart[None, None, :], 0), axis=-1)
        n_blocks = tg * TOP_K // bm + N_EXPERTS
        schedule = _expert_schedule(padded, n_blocks, bm)

        dest3 = dest_kt.reshape(TOP_K, tg // SC_CH, SC_CH).transpose(1, 0, 2)
        xs = _sc_dispatch(n2, dest3, n_blocks * bm)
        out = _experts(schedule, xs, w_gate, b_gate, w_up, b_up, w_down, b_down)
        og = _sc_gather(out, dest_kt.reshape(tg * TOP_K // SC_CH, SC_CH)).reshape(TOP_K, tg, D // 2)
        result = _combine(b0, B, h1, og, probs, mods, g_post_ffn.reshape(1, D), result)
    return result


def kernel(x, c, w_ada, b_ada, g_pre_mix, w_in, g_sgu_v, w_s, b_s, w_gk2, b_gk, g_gla_out, w_out, g_post_mix, g_pre_ffn, w_router, b_router, w_gate, b_gate, w_up, b_up, w_down, b_down, g_post_ffn):
    B = x.shape[0]
    c_pad = jnp.pad(c, ((0, -B % 8), (0, 0)))
    h = x
    for l in range(w_ada.shape[0]):
        h = _layer(h, c_pad, w_ada[l], b_ada[l], g_pre_mix[l], w_in[l], g_sgu_v[l], w_s[l], b_s[l], w_gk2[l], b_gk[l],
                   g_gla_out[l], w_out[l], g_post_mix[l], g_pre_ffn[l], w_router[l], b_router[l], w_gate[l], b_gate[l],
                   w_up[l], b_up[l], w_down[l], b_down[l], g_post_ffn[l])
    return h
```

```python
import functools

import jax
import jax.numpy as jnp
from jax import lax
from jax.experimental import pallas as pl
from jax.experimental.pallas import tpu as pltpu
from jax.experimental.pallas import tpu_sc as plsc

F32 = jnp.float32
BF16 = jnp.bfloat16

D_MODEL = 1024
D_SGU = 512
SGU_HEADS = 8
SGU_HEAD_DIM = 64
SGU_CHUNK = 128
D_GLA = 512
GLA_HEADS = 4
GLA_DK = 256
GLA_HEAD_K = 64
GLA_HEAD_V = 128
GLA_GATE_RANK = 16
GLA_TAU = 16.0
GLA_CHUNK = 64
N_EXPERTS = 32
TOP_K = 4
SWIGLU_LIMIT = 7.0
SWIGLU_ALPHA = 1.702
EPS = 1e-6

LANES = 128
D_IN_MAIN = 2 * D_SGU + 2 * GLA_DK + 2 * D_GLA
D_IN_PAD = D_IN_MAIN + LANES

MIX_TM = 512
MIX_TILE = 256
MIX_SKEW = 1
MIX_SUB = 256
MOE_BM = 512
CMB_TM = 256
N_GROUPS = 2
SC_CORES = 2
SC_SUBCORES = 16
SC_WORKERS = SC_CORES * SC_SUBCORES
SC_CH = 64
NEG_BIG = -1e30

VMEM_LIMIT = 56 * 1024 * 1024


def _ada_kernel(c_ref, w_ref, b_ref, o_ref):
    c = c_ref[...]
    sc = c * jax.nn.sigmoid(c)
    o_ref[...] = jnp.dot(sc.astype(BF16), w_ref[...].astype(BF16), preferred_element_type=F32) + b_ref[...]


def _ada(c_pad, w_ada, b_ada):
    rows, d = c_pad.shape
    n = w_ada.shape[1]
    tn = 1536
    return pl.pallas_call(
        _ada_kernel,
        grid=(n // tn,),
        in_specs=[pl.BlockSpec((rows, d), lambda j: (0, 0)),
                  pl.BlockSpec((d, tn), lambda j: (0, j)),
                  pl.BlockSpec((1, tn), lambda j: (0, j))],
        out_specs=pl.BlockSpec((rows, tn), lambda j: (0, j)),
        out_shape=jax.ShapeDtypeStruct((rows, n), F32),
        compiler_params=pltpu.CompilerParams(dimension_semantics=("arbitrary",), vmem_limit_bytes=VMEM_LIMIT),
        name="ada",
    )(c_pad, w_ada, b_ada)


def _rms(x):
    return x * lax.rsqrt(jnp.mean(x * x, axis=-1, keepdims=True) + EPS)


def _gelu_tanh(x):
    return 0.5 * x * (1.0 + jnp.tanh(0.7978845608028654 * (x + 0.044715 * (x * x * x))))


def _log_sigmoid(z):
    return jnp.minimum(z, 0.0) - jnp.log(1.0 + jnp.exp(-jnp.abs(z)))


def _dot(a, b):
    return jnp.dot(a, b, preferred_element_type=F32)


def _dot_nt(a, b):
    return lax.dot_general(a, b, (((1,), (1,)), ((), ())), preferred_element_type=F32)


def _dot_tn(a, b):
    return lax.dot_general(a, b, (((0,), (0,)), ((), ())), preferred_element_type=F32)


def _pack_rows(x):
    n = x.shape[1] // 2
    lo = lax.bitcast_convert_type(x[:, :n].astype(BF16).astype(F32), jnp.uint32)
    hi = lax.bitcast_convert_type(x[:, n:].astype(BF16).astype(F32), jnp.uint32)
    return (lo >> 16) | (hi & jnp.uint32(0xFFFF0000))


def _unpack_rows(p):
    lo = lax.bitcast_convert_type(p << 16, F32)
    hi = lax.bitcast_convert_type(p & jnp.uint32(0xFFFF0000), F32)
    return lo, hi


def _split_dot(l_bf, a):
    hi = a.astype(BF16)
    lo = (a - hi.astype(F32)).astype(BF16)
    return _dot(l_bf, hi) + _dot(l_bf, lo)


def _mixer_kernel(x_ref, mods_ref, g_pre_ref, w_in_ref, gmat_ref, g_sgu_ref, ws_ref, bs_ref, wgk_ref, bgk_ref,
                  g_gla_ref, w_out_ref, g_post_ref, g_ffn_ref, w_r_ref, b_r_ref,
                  h1_ref, n2_ref, topi_ref, probs_ref, rank_ref, cnt_ref,
                  st_ref, carry_ref):
    b = pl.program_id(0)
    j = pl.program_id(1)

    @pl.when(j == 0)
    def _():
        st_ref[...] = jnp.zeros_like(st_ref)

    @pl.when((b == 0) & (j == 0))
    def _():
        carry_ref[...] = jnp.zeros_like(carry_ref)

    carry_box = [carry_ref[...]]
    tiles = [_mixer_tile(slice(s * MIX_TILE, (s + 1) * MIX_TILE), carry_box,
                         x_ref, mods_ref, g_pre_ref, w_in_ref, gmat_ref, g_sgu_ref, ws_ref, bs_ref, wgk_ref, bgk_ref,
                         g_gla_ref, w_out_ref, g_post_ref, g_ffn_ref, w_r_ref, b_r_ref,
                         h1_ref, n2_ref, topi_ref, probs_ref, rank_ref, st_ref)
             for s in range(x_ref.shape[1] // MIX_TILE)]
    alive = [True] * len(tiles)
    t = 0
    while any(alive):
        for s, tile in enumerate(tiles):
            if alive[s] and t >= s * MIX_SKEW:
                alive[s] = next(tile, None) is not None
        t += 1
    carry_ref[...] = carry_box[0]
    cnt_ref[...] = jnp.broadcast_to(carry_box[0], cnt_ref.shape)


def _mixer_tile(rows, carry_box, x_ref, mods_ref, g_pre_ref, w_in_ref, gmat_ref, g_sgu_ref, ws_ref, bs_ref, wgk_ref,
                bgk_ref, g_gla_ref, w_out_ref, g_post_ref, g_ffn_ref, w_r_ref, b_r_ref,
                h1_ref, n2_ref, topi_ref, probs_ref, rank_ref, st_ref):
    tm = rows.stop - rows.start
    mods = mods_ref[0]
    shift1, scale1, gate1 = mods[0:1], mods[1:2], mods[2:3]
    shift2, scale2 = mods[3:4], mods[4:5]

    x = x_ref[0, rows, :]
    n = _rms(x) * (g_pre_ref[...] * (1.0 + scale1)) + shift1
    nb = n.astype(BF16)
    yield True

    pu = _dot(nb, w_in_ref[:, 0:D_SGU])
    pv = _dot(nb, w_in_ref[:, D_SGU:2 * D_SGU])
    yield True

    u = _gelu_tanh(pu)
    v = _gelu_tanh(pv)
    msv = _dot((v * v).astype(BF16), gmat_ref[...])
    vh = v * lax.rsqrt(msv + EPS) * g_sgu_ref[...]
    yield True

    qk = _dot(nb, w_in_ref[:, 2 * D_SGU:2 * D_SGU + 2 * GLA_DK])
    vv = _dot(nb, w_in_ref[:, 2 * D_SGU + 2 * GLA_DK:2 * D_SGU + 2 * GLA_DK + D_GLA]).astype(BF16)
    r = _dot(nb, w_in_ref[:, D_IN_MAIN - D_GLA:D_IN_MAIN])
    g_low = _dot(nb, w_in_ref[:, D_IN_MAIN:D_IN_PAD])
    yield True

    n_sc = tm // SGU_CHUNK
    lane_c = lax.broadcasted_iota(jnp.int32, (SGU_CHUNK, LANES), 1)
    low_half = lane_c < SGU_HEAD_DIM
    w_row = lax.broadcasted_iota(jnp.int32, (SGU_CHUNK, 2 * SGU_CHUNK), 0)
    w_col = lax.broadcasted_iota(jnp.int32, (SGU_CHUNK, 2 * SGU_CHUNK), 1) & (SGU_CHUNK - 1)
    a_cols = []
    for p in range(SGU_HEADS // 2):
        wcat = jnp.where(w_row >= w_col, ws_ref[p], 0.0).astype(BF16)
        rhs = []
        for ci in range(n_sc):
            vp = vh[ci * SGU_CHUNK:(ci + 1) * SGU_CHUNK, p * LANES:(p + 1) * LANES]
            rhs.append(jnp.concatenate([jnp.where(low_half, vp, 0.0), jnp.where(low_half, 0.0, vp)],
                                       axis=0).astype(BF16))
        res = _dot(wcat, jnp.concatenate(rhs, axis=1))
        bias = bs_ref[:, p * LANES:(p + 1) * LANES]
        a_cols.append(jnp.concatenate(
            [res[:, ci * LANES:(ci + 1) * LANES] + bias for ci in range(n_sc)], axis=0))
    a_out = u * jnp.concatenate(a_cols, axis=1)

    q = qk[:, :GLA_DK] * (GLA_HEAD_K ** -0.5)
    k = qk[:, GLA_DK:]
    z = _dot(g_low.astype(BF16), wgk_ref[...]) + bgk_ref[...]
    log_a = _log_sigmoid(z) * (1.0 / GLA_TAU)

    sub = min(tm, MIX_SUB)
    subs = [slice(s * sub, (s + 1) * sub) for s in range(tm // sub)]
    t_row = lax.broadcasted_iota(jnp.int32, (sub, sub), 0)
    t_col = lax.broadcasted_iota(jnp.int32, (sub, sub), 1)
    same_chunk = (t_row >> 6) == (t_col >> 6)
    l_cum = jnp.where(same_chunk & (t_row >= t_col), 1.0, 0.0).astype(BF16)
    l_all = jnp.where(same_chunk, 1.0, 0.0).astype(BF16)
    bcum = jnp.concatenate([_split_dot(l_cum, log_a[s]) for s in subs], axis=0)
    blast = jnp.concatenate([_split_dot(l_all, log_a[s]) for s in subs], axis=0)
    q_s = q * jnp.exp(bcum)
    k_s = (k * jnp.exp(-bcum)).astype(BF16)
    k_dec = (k * jnp.exp(blast - bcum)).astype(BF16)
    decay = jnp.exp(blast)
    yield True

    lane_g = lax.broadcasted_iota(jnp.int32, (GLA_CHUNK, LANES), 1)
    causal = (lax.broadcasted_iota(jnp.int32, (GLA_CHUNK, GLA_CHUNK), 0)
              >= lax.broadcasted_iota(jnp.int32, (GLA_CHUNK, GLA_CHUNK), 1))
    g_gla = g_gla_ref[...]
    o_rows = []
    for ci in range(tm // GLA_CHUNK):
        rs = slice(ci * GLA_CHUNK, (ci + 1) * GLA_CHUNK)
        o_heads = []
        for h in range(GLA_HEADS):
            pl_ = slice((h // 2) * LANES, (h // 2 + 1) * LANES)
            own = (lane_g >= GLA_HEAD_K) if (h % 2) else (lane_g < GLA_HEAD_K)
            qm = jnp.where(own, q_s[rs, pl_], 0.0).astype(BF16)
            v_h = vv[rs, h * GLA_HEAD_V:(h + 1) * GLA_HEAD_V]
            scores = jnp.where(causal, _dot_nt(qm, k_s[rs, pl_]), 0.0)
            intra = _dot(scores.astype(BF16), v_h)
            st = st_ref[h]
            inter = _dot_nt(qm, st.astype(BF16))
            st_ref[h] = st * decay[ci * GLA_CHUNK:ci * GLA_CHUNK + 1, pl_] + _dot_tn(v_h, k_dec[rs, pl_])
            o_heads.append(_rms(intra + inter) * g_gla)
        o_rows.append(jnp.concatenate(o_heads, axis=1))
    o = jnp.concatenate(o_rows, axis=0) * (r * jax.nn.sigmoid(r))
    mix_in = jnp.concatenate([a_out, o], axis=1).astype(BF16)
    yield True

    mix = _dot(mix_in, w_out_ref[...])
    yield True

    h1 = x + gate1 * (_rms(mix) * g_post_ref[...])
    h1_ref[0, rows, :] = h1
    n2 = _rms(h1) * (g_ffn_ref[...] * (1.0 + scale2)) + shift2
    n2_ref[rows, :] = _pack_rows(n2)
    n2b = n2.astype(BF16)
    yield True

    logits = _dot(n2b, w_r_ref[...]) + b_r_ref[...]
    lane = lax.broadcasted_iota(jnp.int32, (tm, LANES), 1)
    lane_f = lane.astype(F32)
    vals = logits
    sels, tops, idxs = [], [], []
    for _ in range(TOP_K):
        m = jnp.max(vals, axis=-1, keepdims=True)
        idx = jnp.min(jnp.where(vals == m, lane_f, float(LANES)), axis=-1, keepdims=True)
        sel = lane_f == idx
        sels.append(sel)
        tops.append(m)
        idxs.append(idx)
        vals = jnp.where(sel, -jnp.inf, vals)
    es = [jnp.exp(t - tops[0]) for t in tops]
    inv = 1.0 / (es[0] + es[1] + es[2] + es[3])

    multi = jnp.zeros((tm, LANES), F32)
    for sel in sels:
        multi = multi + jnp.where(sel, 1.0, 0.0)
    l_strict = jnp.where(t_row > t_col, 1.0, 0.0).astype(BF16)
    carry = carry_box[0]
    before = []
    for s in subs:
        before.append(_dot(l_strict, multi[s].astype(BF16)) + carry)
        carry = carry + jnp.sum(multi[s], axis=0, keepdims=True)
    before = jnp.concatenate(before, axis=0)
    carry_box[0] = carry

    topi = jnp.zeros((tm, LANES), F32)
    probs = jnp.zeros((tm, LANES), F32)
    rank = jnp.zeros((tm, LANES), F32)
    for kk in range(TOP_K):
        here = lane == kk
        topi = jnp.where(here, idxs[kk], topi)
        probs = jnp.where(here, es[kk] * inv, probs)
        rank = jnp.where(here, jnp.sum(jnp.where(sels[kk], before, 0.0), axis=-1, keepdims=True), rank)
    topi_ref[rows, :] = topi.astype(jnp.int32)
    probs_ref[rows, :] = probs
    rank_ref[rows, :] = rank.astype(jnp.int32)


def _const_spec(shape):
    return pl.BlockSpec(shape, lambda b, j: (0,) * len(shape))


def _mixer(b0, B, x, mods, g_pre, w_in_p, gmat, g_sgu, ws_cat, bs_exp, wgk_p, bgk, g_gla, w_out_b, g_post, g_ffn, w_r_p,
           b_r_p):
    _, S, D = x.shape
    T = B * S
    tm = MIX_TM
    nj = S // tm
    tok_spec = pl.BlockSpec((tm, LANES), lambda b, j: (b * nj + j, 0))
    return pl.pallas_call(
        _mixer_kernel,
        grid=(B, nj),
        in_specs=[pl.BlockSpec((1, tm, D), lambda b, j: (b + b0, j, 0)),
                  pl.BlockSpec((1, 6, D), lambda b, j: (b + b0, 0, 0)),
                  _const_spec((1, D)), _const_spec(w_in_p.shape), _const_spec(gmat.shape), _const_spec((1, D_SGU)),
                  _const_spec(ws_cat.shape), _const_spec(bs_exp.shape), _const_spec(wgk_p.shape),
                  _const_spec((1, GLA_DK)), _const_spec((1, GLA_HEAD_V)), _const_spec(w_out_b.shape),
                  _const_spec((1, D)), _const_spec((1, D)), _const_spec(w_r_p.shape), _const_spec((1, LANES))],
        out_specs=[pl.BlockSpec((1, tm, D), lambda b, j: (b, j, 0)),
                   pl.BlockSpec((tm, D // 2), lambda b, j: (b * nj + j, 0)),
                   tok_spec, tok_spec, tok_spec,
                   pl.BlockSpec((8, LANES), lambda b, j: (0, 0))],
        out_shape=[jax.ShapeDtypeStruct((B, S, D), F32),
                   jax.ShapeDtypeStruct((T, D // 2), jnp.uint32),
                   jax.ShapeDtypeStruct((T, LANES), jnp.int32),
                   jax.ShapeDtypeStruct((T, LANES), F32),
                   jax.ShapeDtypeStruct((T, LANES), jnp.int32),
                   jax.ShapeDtypeStruct((8, LANES), F32)],
        scratch_shapes=[pltpu.VMEM((GLA_HEADS, GLA_HEAD_V, LANES), F32),
                        pltpu.VMEM((1, LANES), F32)],
        compiler_params=pltpu.CompilerParams(dimension_semantics=("arbitrary", "arbitrary"),
                                             vmem_limit_bytes=VMEM_LIMIT),
        name="mixer",
    )(x, mods, g_pre, w_in_p, gmat, g_sgu, ws_cat, bs_exp, wgk_p, bgk, g_gla, w_out_b, g_post, g_ffn, w_r_p, b_r_p)


def _sc_dispatch(rows, dest3, n_out):
    T, D = rows.shape
    n_chunks, top_k, ch = dest3.shape
    cpw = n_chunks // SC_WORKERS
    mesh = plsc.VectorSubcoreMesh(core_axis_name="c", subcore_axis_name="s")

    @functools.partial(
        pl.kernel, mesh=mesh,
        out_type=jax.ShapeDtypeStruct((n_out, D), rows.dtype),
        scratch_types=[pltpu.VMEM((top_k, ch), jnp.int32), pltpu.VMEM((ch, D), rows.dtype)],
    )
    def k(rows_hbm, dest_hbm, out_hbm, idx_v, rows_v):
        wid = lax.axis_index("s") * SC_CORES + lax.axis_index("c")

        @pl.loop(0, cpw)
        def _(i):
            c = wid * cpw + i
            pltpu.sync_copy(dest_hbm.at[c], idx_v)
            pltpu.sync_copy(rows_hbm.at[pl.ds(c * ch, ch)], rows_v)
            for kk in range(top_k):
                pltpu.sync_copy(rows_v, out_hbm.at[idx_v.at[kk]])

    return k(rows, dest3)


def _sc_gather(src, idx2):
    _, D = src.shape
    n_chunks, ch = idx2.shape
    cpw = n_chunks // SC_WORKERS
    mesh = plsc.VectorSubcoreMesh(core_axis_name="c", subcore_axis_name="s")

    @functools.partial(
        pl.kernel, mesh=mesh,
        out_type=jax.ShapeDtypeStruct((n_chunks * ch, D), src.dtype),
        scratch_types=[pltpu.VMEM((ch,), jnp.int32), pltpu.VMEM((ch, D), src.dtype)],
    )
    def k(src_hbm, idx_hbm, out_hbm, idx_v, rows_v):
        wid = lax.axis_index("s") * SC_CORES + lax.axis_index("c")

        @pl.loop(0, cpw)
        def _(i):
            c = wid * cpw + i
            pltpu.sync_copy(idx_hbm.at[c], idx_v)
            pltpu.sync_copy(src_hbm.at[idx_v], rows_v)
            pltpu.sync_copy(rows_v, out_hbm.at[pl.ds(c * ch, ch)])

    return k(src, idx2)


def _expert_kernel(be_ref, first_ref, last_ref, nxt_ref, slot_ref, short_ref, nu_ref,
                   x_ref, wg_hbm, bg_ref, wu_hbm, bu_ref, wd_hbm, bd_ref, o_ref,
                   stage_ref, wbf_ref, sem):
    i = pl.program_id(0)
    active = i < nu_ref[0]
    has_next = nxt_ref[i] >= 0

    def weight_copies(e):
        return [pltpu.make_async_copy(w.at[e], stage_ref.at[m], sem.at[m])
                for m, w in enumerate((wg_hbm, wu_hbm, wd_hbm))]

    def cast_stage_into(slot):
        for m in range(3):
            wbf_ref[slot, m] = stage_ref[m].astype(BF16)

    @pl.when(i == 0)
    def _():
        for cp in weight_copies(be_ref[0]):
            cp.start()
        for cp in weight_copies(be_ref[0]):
            cp.wait()
        cast_stage_into(0)

    @pl.when(active & (first_ref[i] == 1) & has_next)
    def _():
        for cp in weight_copies(nxt_ref[i]):
            cp.start()

    def ffn(rows):
        slot = slot_ref[i]
        x_lo, x_hi = _unpack_rows(x_ref[rows, :])
        xb = jnp.concatenate([x_lo, x_hi], axis=1).astype(BF16)
        g = jnp.minimum(_dot(xb, wbf_ref[slot, 0]) + bg_ref[0], SWIGLU_LIMIT)
        u = jnp.clip(_dot(xb, wbf_ref[slot, 1]) + bu_ref[0], -SWIGLU_LIMIT, SWIGLU_LIMIT)
        hdn = (u + 1.0) * (g * jax.nn.sigmoid(SWIGLU_ALPHA * g))
        o_ref[rows, :] = _pack_rows(_dot(hdn.astype(BF16), wbf_ref[slot, 2]) + bd_ref[0])

    bm = x_ref.shape[0]

    @pl.when(active & (short_ref[i] == 0))
    def _():
        ffn(slice(0, bm))

    @pl.when(active & (short_ref[i] == 1))
    def _():
        ffn(slice(0, bm // 2))

    @pl.when(active & (last_ref[i] == 1) & has_next)
    def _():
        for cp in weight_copies(nxt_ref[i]):
            cp.wait()
        cast_stage_into(1 - slot_ref[i])


def _expert_schedule(counts, padded, n_blocks, bm):
    pend = jnp.cumsum(padded)
    ids = jnp.arange(N_EXPERTS, dtype=jnp.int32)
    idx = jnp.arange(n_blocks, dtype=jnp.int32)
    n_used = (pend[-1] // bm).astype(jnp.int32)
    block_e = jnp.minimum(jnp.sum((pend[None, :] <= (idx * bm)[:, None]).astype(jnp.int32), axis=1), N_EXPERTS - 1)
    onehot = block_e[:, None] == ids[None, :]

    def per_block(table):
        return jnp.sum(jnp.where(onehot, table[None, :], 0), axis=1)

    start_blk = (pend - padded) // bm
    first = (idx == per_block(start_blk)) & (idx < n_used)
    last = (idx == per_block(pend // bm) - 1) & (idx < n_used)
    later = (ids[None, :] > ids[:, None]) & (padded[None, :] > 0)
    nxt_e = jnp.min(jnp.where(later, ids[None, :], N_EXPERTS), axis=1)
    nxt_e = jnp.where(nxt_e == N_EXPERTS, -1, nxt_e)
    order = jnp.cumsum((padded > 0).astype(jnp.int32)) - 1
    real_rows = per_block(counts) - (idx - per_block(start_blk)) * bm
    short = (idx < n_used) & (real_rows <= bm // 2)
    as_i32 = lambda a: a.astype(jnp.int32)
    return (block_e, as_i32(first), as_i32(last), per_block(nxt_e), per_block(order) & 1, as_i32(short),
            n_used.reshape(1))


def _experts(schedule, xs, w_gate, b_gate, w_up, b_up, w_down, b_down):
    P, half = xs.shape
    bm = MOE_BM
    n_blocks = P // bm
    E, D, DE = w_gate.shape

    def row_map(i, be, first, last, nxt, slot, short, nu):
        return (jnp.minimum(i, nu[0] - 1), 0)

    def b_map(i, be, first, last, nxt, slot, short, nu):
        return (be[jnp.minimum(i, nu[0] - 1)], 0, 0)

    hbm = pl.BlockSpec(memory_space=pl.ANY)
    grid_spec = pltpu.PrefetchScalarGridSpec(
        num_scalar_prefetch=len(schedule),
        grid=(n_blocks,),
        in_specs=[pl.BlockSpec((bm, half), row_map),
                  hbm, pl.BlockSpec((1, 1, DE), b_map),
                  hbm, pl.BlockSpec((1, 1, DE), b_map),
                  hbm, pl.BlockSpec((1, 1, D), b_map)],
        out_specs=pl.BlockSpec((bm, half), row_map),
        scratch_shapes=[pltpu.VMEM((3, D, DE), F32),
                        pltpu.VMEM((2, 3, D, DE), BF16),
                        pltpu.SemaphoreType.DMA((3,))],
    )
    return pl.pallas_call(
        _expert_kernel,
        grid_spec=grid_spec,
        out_shape=jax.ShapeDtypeStruct((P, half), jnp.uint32),
        compiler_params=pltpu.CompilerParams(dimension_semantics=("arbitrary",), vmem_limit_bytes=VMEM_LIMIT),
        name="experts",
    )(*schedule, xs, w_gate, b_gate.reshape(E, 1, DE), w_up, b_up.reshape(E, 1, DE),
      w_down, b_down.reshape(E, 1, D))


def _combine_kernel(h1_ref, og_ref, probs_ref, mods_ref, g_ref, *rest):
    o_ref = rest[-1]
    tm = h1_ref.shape[1]
    probs = probs_ref[...]
    lane = lax.broadcasted_iota(jnp.int32, (tm, LANES), 1)
    half = og_ref.shape[2]
    f_lo = jnp.zeros((tm, half), F32)
    f_hi = jnp.zeros((tm, half), F32)
    for kk in range(TOP_K):
        pk = jnp.sum(jnp.where(lane == kk, probs, 0.0), axis=-1, keepdims=True)
        o_lo, o_hi = _unpack_rows(og_ref[kk])
        f_lo = f_lo + pk * o_lo
        f_hi = f_hi + pk * o_hi
    f = jnp.concatenate([f_lo, f_hi], axis=1)
    gate2 = mods_ref[0][5:6]
    o_ref[0] = h1_ref[0] + gate2 * (_rms(f) * g_ref[...])


def _combine(b0, b_total, h1, og, probs, mods, g_post_ffn, out_so_far):
    B, S, D = h1.shape
    tm = CMB_TM
    nj = S // tm
    in_specs = [pl.BlockSpec((1, tm, D), lambda b, j: (b, j, 0)),
                pl.BlockSpec((TOP_K, tm, D // 2), lambda b, j: (0, b * nj + j, 0)),
                pl.BlockSpec((tm, LANES), lambda b, j: (b * nj + j, 0)),
                pl.BlockSpec((1, 6, D), lambda b, j: (b + b0, 0, 0)),
                pl.BlockSpec((1, D), lambda b, j: (0, 0))]
    args = [h1, og, probs, mods, g_post_ffn]
    aliases = {}
    if out_so_far is not None:
        in_specs.append(pl.BlockSpec(memory_space=pl.ANY))
        args.append(out_so_far)
        aliases = {len(args) - 1: 0}
    return pl.pallas_call(
        _combine_kernel,
        grid=(B, nj),
        in_specs=in_specs,
        out_specs=pl.BlockSpec((1, tm, D), lambda b, j: (b + b0, j, 0)),
        out_shape=jax.ShapeDtypeStruct((b_total, S, D), F32),
        input_output_aliases=aliases,
        compiler_params=pltpu.CompilerParams(dimension_semantics=("arbitrary", "arbitrary"),
                                             vmem_limit_bytes=VMEM_LIMIT),
        name="combine",
    )(*args)


def _layer(h, c_pad, w_ada, b_ada, g_pre_mix, w_in, g_sgu_v, w_s, b_s, w_gk2, b_gk, g_gla_out, w_out, g_post_mix,
           g_pre_ffn, w_router, b_router, w_gate, b_gate, w_up, b_up, w_down, b_down, g_post_ffn):
    B, S, D = h.shape

    mods = _ada(c_pad, w_ada, b_ada.reshape(1, -1))[:B].reshape(B, 6, D)

    w_in_p = jnp.pad(w_in, ((0, 0), (0, D_IN_PAD - w_in.shape[1]))).astype(BF16)
    head_of = jnp.arange(D_SGU) // SGU_HEAD_DIM
    gmat = jnp.where(head_of[:, None] == head_of[None, :], 1.0 / SGU_HEAD_DIM, 0.0).astype(BF16)
    ws_cat = w_s.reshape(SGU_HEADS // 2, 2, SGU_CHUNK, SGU_CHUNK).transpose(0, 2, 1, 3).reshape(
        SGU_HEADS // 2, SGU_CHUNK, 2 * SGU_CHUNK)
    bs_exp = jnp.repeat(b_s.T, SGU_HEAD_DIM, axis=1)
    wgk_p = jnp.pad(w_gk2, ((0, LANES - GLA_GATE_RANK), (0, 0))).astype(BF16)
    w_r_p = jnp.pad(w_router, ((0, 0), (0, LANES - N_EXPERTS))).astype(BF16)
    b_r_p = jnp.concatenate([b_router, jnp.full((LANES - N_EXPERTS,), NEG_BIG, F32)]).reshape(1, LANES)

    w_out_b = w_out.astype(BF16)

    n_groups = N_GROUPS if B % N_GROUPS == 0 else 1
    bg = B // n_groups
    tg = bg * S
    result = None
    for gi in range(n_groups):
        b0 = gi * bg
        h1, n2, topi, probs, rank, cnt = _mixer(
            b0, bg, h, mods, g_pre_mix.reshape(1, D), w_in_p, gmat, g_sgu_v.reshape(1, D_SGU), ws_cat, bs_exp, wgk_p,
            b_gk.reshape(1, GLA_DK), g_gla_out.reshape(1, GLA_HEAD_V), w_out_b, g_post_mix.reshape(1, D),
            g_pre_ffn.reshape(1, D), w_r_p, b_r_p)

        bm = MOE_BM
        counts = cnt[0, :N_EXPERTS].astype(jnp.int32)
        padded = ((counts + bm - 1) // bm) * bm
        pend = jnp.cumsum(padded)
        pstart = pend - padded
        e_kt = topi[:, :TOP_K].T
        onehot = e_kt[:, :, None] == jnp.arange(N_EXPERTS, dtype=jnp.int32)[None, None, :]
        dest_kt = rank[:, :TOP_K].T + jnp.sum(jnp.where(onehot, pstart[None, None, :], 0), axis=-1)
        n_blocks = tg * TOP_K // bm + N_EXPERTS
        schedule = _expert_schedule(counts, padded, n_blocks, bm)

        dest3 = dest_kt.reshape(TOP_K, tg // SC_CH, SC_CH).transpose(1, 0, 2)
        xs = _sc_dispatch(n2, dest3, n_blocks * bm)
        out = _experts(schedule, xs, w_gate, b_gate, w_up, b_up, w_down, b_down)
        og = _sc_gather(out, dest_kt.reshape(tg * TOP_K // SC_CH, SC_CH)).reshape(TOP_K, tg, D // 2)
        result = _combine(b0, B, h1, og, probs, mods, g_post_ffn.reshape(1, D), result)
    return result


def kernel(x, c, w_ada, b_ada, g_pre_mix, w_in, g_sgu_v, w_s, b_s, w_gk2, b_gk, g_gla_out, w_out, g_post_mix, g_pre_ffn, w_router, b_router, w_gate, b_gate, w_up, b_up, w_down, b_down, g_post_ffn):
    B = x.shape[0]
    c_pad = jnp.pad(c, ((0, -B % 8), (0, 0)))
    h = x
    for l in range(w_ada.shape[0]):
        h = _layer(h, c_pad, w_ada[l], b_ada[l], g_pre_mix[l], w_in[l], g_sgu_v[l], w_s[l], b_s[l], w_gk2[l], b_gk[l],
                   g_gla_out[l], w_out[l], g_post_mix[l], g_pre_ffn[l], w_router[l], b_router[l], w_gate[l], b_gate[l],
                   w_up[l], b_up[l], w_down[l], b_down[l], g_post_ffn[l])
    return h
```

```python
import functools

import jax
import jax.numpy as jnp
from jax import lax
from jax.experimental import pallas as pl
from jax.experimental.pallas import tpu as pltpu
from jax.experimental.pallas import tpu_sc as plsc

F32 = jnp.float32
BF16 = jnp.bfloat16

D_MODEL = 1024
D_SGU = 512
SGU_HEADS = 8
SGU_HEAD_DIM = 64
SGU_CHUNK = 128
D_GLA = 512
GLA_HEADS = 4
GLA_DK = 256
GLA_HEAD_K = 64
GLA_HEAD_V = 128
GLA_GATE_RANK = 16
GLA_TAU = 16.0
GLA_CHUNK = 64
N_EXPERTS = 32
TOP_K = 4
SWIGLU_LIMIT = 7.0
SWIGLU_ALPHA = 1.702
EPS = 1e-6

LANES = 128
D_IN_MAIN = 2 * D_SGU + 2 * GLA_DK + 2 * D_GLA
D_IN_PAD = D_IN_MAIN + LANES

MIX_TM = 512
MIX_TILE = 256
MIX_SKEW = 1
MIX_SUB = 256
MOE_BM = 512
CMB_TM = 256
N_GROUPS = 2
SC_CORES = 2
SC_SUBCORES = 16
SC_WORKERS = SC_CORES * SC_SUBCORES
SC_CH = 64
NEG_BIG = -1e30

VMEM_LIMIT = 56 * 1024 * 1024


def _ada_kernel(c_ref, w_ref, b_ref, o_ref):
    c = c_ref[...]
    sc = c * jax.nn.sigmoid(c)
    o_ref[...] = jnp.dot(sc.astype(BF16), w_ref[...].astype(BF16), preferred_element_type=F32) + b_ref[...]


def _ada(c_pad, w_ada, b_ada):
    rows, d = c_pad.shape
    n = w_ada.shape[1]
    tn = 1536
    return pl.pallas_call(
        _ada_kernel,
        grid=(n // tn,),
        in_specs=[pl.BlockSpec((rows, d), lambda j: (0, 0)),
                  pl.BlockSpec((d, tn), lambda j: (0, j)),
                  pl.BlockSpec((1, tn), lambda j: (0, j))],
        out_specs=pl.BlockSpec((rows, tn), lambda j: (0, j)),
        out_shape=jax.ShapeDtypeStruct((rows, n), F32),
        compiler_params=pltpu.CompilerParams(dimension_semantics=("arbitrary",), vmem_limit_bytes=VMEM_LIMIT),
        name="ada",
    )(c_pad, w_ada, b_ada)


def _rms(x):
    return x * lax.rsqrt(jnp.mean(x * x, axis=-1, keepdims=True) + EPS)


def _gelu_tanh(x):
    return 0.5 * x * (1.0 + jnp.tanh(0.7978845608028654 * (x + 0.044715 * (x * x * x))))


def _log_sigmoid(z):
    return jnp.minimum(z, 0.0) - jnp.log(1.0 + jnp.exp(-jnp.abs(z)))


def _dot(a, b):
    return jnp.dot(a, b, preferred_element_type=F32)


def _dot_nt(a, b):
    return lax.dot_general(a, b, (((1,), (1,)), ((), ())), preferred_element_type=F32)


def _dot_tn(a, b):
    return lax.dot_general(a, b, (((0,), (0,)), ((), ())), preferred_element_type=F32)


def _pack_rows(x):
    n = x.shape[1] // 2
    lo = lax.bitcast_convert_type(x[:, :n].astype(BF16).astype(F32), jnp.uint32)
    hi = lax.bitcast_convert_type(x[:, n:].astype(BF16).astype(F32), jnp.uint32)
    return (lo >> 16) | (hi & jnp.uint32(0xFFFF0000))


def _unpack_rows(p):
    lo = lax.bitcast_convert_type(p << 16, F32)
    hi = lax.bitcast_convert_type(p & jnp.uint32(0xFFFF0000), F32)
    return lo, hi


def _split_dot(l_bf, a):
    hi = a.astype(BF16)
    lo = (a - hi.astype(F32)).astype(BF16)
    return _dot(l_bf, hi) + _dot(l_bf, lo)


def _mixer_kernel(x_ref, mods_ref, g_pre_ref, w_in_ref, gmat_ref, g_sgu_ref, ws_ref, bs_ref, wgk_ref, bgk_ref,
                  g_gla_ref, w_out_ref, g_post_ref, g_ffn_ref, w_r_ref, b_r_ref,
                  h1_ref, n2_ref, route_ref, probs_ref, cnt_ref,
                  st_ref, carry_ref):
    b = pl.program_id(0)
    j = pl.program_id(1)

    @pl.when(j == 0)
    def _():
        st_ref[...] = jnp.zeros_like(st_ref)

    @pl.when((b == 0) & (j == 0))
    def _():
        carry_ref[...] = jnp.zeros_like(carry_ref)

    carry_box = [carry_ref[...]]
    tiles = [_mixer_tile(slice(s * MIX_TILE, (s + 1) * MIX_TILE), carry_box,
                         x_ref, mods_ref, g_pre_ref, w_in_ref, gmat_ref, g_sgu_ref, ws_ref, bs_ref, wgk_ref, bgk_ref,
                         g_gla_ref, w_out_ref, g_post_ref, g_ffn_ref, w_r_ref, b_r_ref,
                         h1_ref, n2_ref, route_ref, probs_ref, st_ref)
             for s in range(x_ref.shape[1] // MIX_TILE)]
    alive = [True] * len(tiles)
    t = 0
    while any(alive):
        for s, tile in enumerate(tiles):
            if alive[s] and t >= s * MIX_SKEW:
                alive[s] = next(tile, None) is not None
        t += 1
    carry_ref[...] = carry_box[0]
    cnt_ref[...] = jnp.broadcast_to(carry_box[0], cnt_ref.shape)


def _mixer_tile(rows, carry_box, x_ref, mods_ref, g_pre_ref, w_in_ref, gmat_ref, g_sgu_ref, ws_ref, bs_ref, wgk_ref,
                bgk_ref, g_gla_ref, w_out_ref, g_post_ref, g_ffn_ref, w_r_ref, b_r_ref,
                h1_ref, n2_ref, route_ref, probs_ref, st_ref):
    tm = rows.stop - rows.start
    mods = mods_ref[0]
    shift1, scale1, gate1 = mods[0:1], mods[1:2], mods[2:3]
    shift2, scale2 = mods[3:4], mods[4:5]

    x = x_ref[0, rows, :]
    n = _rms(x) * (g_pre_ref[...] * (1.0 + scale1)) + shift1
    nb = n.astype(BF16)
    yield True

    pu = _dot(nb, w_in_ref[:, 0:D_SGU])
    pv = _dot(nb, w_in_ref[:, D_SGU:2 * D_SGU])
    yield True

    u = _gelu_tanh(pu)
    v = _gelu_tanh(pv)
    msv = _dot((v * v).astype(BF16), gmat_ref[...])
    vh = v * lax.rsqrt(msv + EPS) * g_sgu_ref[...]
    yield True

    qk = _dot(nb, w_in_ref[:, 2 * D_SGU:2 * D_SGU + 2 * GLA_DK])
    vv = _dot(nb, w_in_ref[:, 2 * D_SGU + 2 * GLA_DK:2 * D_SGU + 2 * GLA_DK + D_GLA]).astype(BF16)
    r = _dot(nb, w_in_ref[:, D_IN_MAIN - D_GLA:D_IN_MAIN])
    g_low = _dot(nb, w_in_ref[:, D_IN_MAIN:D_IN_PAD])
    yield True

    n_sc = tm // SGU_CHUNK
    lane_c = lax.broadcasted_iota(jnp.int32, (SGU_CHUNK, LANES), 1)
    low_half = lane_c < SGU_HEAD_DIM
    w_row = lax.broadcasted_iota(jnp.int32, (SGU_CHUNK, 2 * SGU_CHUNK), 0)
    w_col = lax.broadcasted_iota(jnp.int32, (SGU_CHUNK, 2 * SGU_CHUNK), 1) & (SGU_CHUNK - 1)
    a_cols = []
    for p in range(SGU_HEADS // 2):
        wcat = jnp.where(w_row >= w_col, ws_ref[p], 0.0).astype(BF16)
        rhs = []
        for ci in range(n_sc):
            vp = vh[ci * SGU_CHUNK:(ci + 1) * SGU_CHUNK, p * LANES:(p + 1) * LANES]
            rhs.append(jnp.concatenate([jnp.where(low_half, vp, 0.0), jnp.where(low_half, 0.0, vp)],
                                       axis=0).astype(BF16))
        res = _dot(wcat, jnp.concatenate(rhs, axis=1))
        bias = bs_ref[:, p * LANES:(p + 1) * LANES]
        a_cols.append(jnp.concatenate(
            [res[:, ci * LANES:(ci + 1) * LANES] + bias for ci in range(n_sc)], axis=0))
    a_out = u * jnp.concatenate(a_cols, axis=1)

    q = qk[:, :GLA_DK] * (GLA_HEAD_K ** -0.5)
    k = qk[:, GLA_DK:]
    z = _dot(g_low.astype(BF16), wgk_ref[...]) + bgk_ref[...]
    log_a = _log_sigmoid(z) * (1.0 / GLA_TAU)

    sub = min(tm, MIX_SUB)
    subs = [slice(s * sub, (s + 1) * sub) for s in range(tm // sub)]
    t_row = lax.broadcasted_iota(jnp.int32, (sub, sub), 0)
    t_col = lax.broadcasted_iota(jnp.int32, (sub, sub), 1)
    same_chunk = (t_row >> 6) == (t_col >> 6)
    l_cum = jnp.where(same_chunk & (t_row >= t_col), 1.0, 0.0).astype(BF16)
    l_all = jnp.where(same_chunk, 1.0, 0.0).astype(BF16)
    bcum = jnp.concatenate([_split_dot(l_cum, log_a[s]) for s in subs], axis=0)
    blast = jnp.concatenate([_split_dot(l_all, log_a[s]) for s in subs], axis=0)
    q_s = q * jnp.exp(bcum)
    k_s = (k * jnp.exp(-bcum)).astype(BF16)
    k_dec = (k * jnp.exp(blast - bcum)).astype(BF16)
    decay = jnp.exp(blast)
    yield True

    lane_g = lax.broadcasted_iota(jnp.int32, (GLA_CHUNK, LANES), 1)
    causal = (lax.broadcasted_iota(jnp.int32, (GLA_CHUNK, GLA_CHUNK), 0)
              >= lax.broadcasted_iota(jnp.int32, (GLA_CHUNK, GLA_CHUNK), 1))
    g_gla = g_gla_ref[...]
    o_rows = []
    for ci in range(tm // GLA_CHUNK):
        rs = slice(ci * GLA_CHUNK, (ci + 1) * GLA_CHUNK)
        o_heads = []
        for h in range(GLA_HEADS):
            pl_ = slice((h // 2) * LANES, (h // 2 + 1) * LANES)
            own = (lane_g >= GLA_HEAD_K) if (h % 2) else (lane_g < GLA_HEAD_K)
            qm = jnp.where(own, q_s[rs, pl_], 0.0).astype(BF16)
            v_h = vv[rs, h * GLA_HEAD_V:(h + 1) * GLA_HEAD_V]
            scores = jnp.where(causal, _dot_nt(qm, k_s[rs, pl_]), 0.0)
            intra = _dot(scores.astype(BF16), v_h)
            st = st_ref[h]
            inter = _dot_nt(qm, st.astype(BF16))
            st_ref[h] = st * decay[ci * GLA_CHUNK:ci * GLA_CHUNK + 1, pl_] + _dot_tn(v_h, k_dec[rs, pl_])
            o_heads.append(_rms(intra + inter) * g_gla)
        o_rows.append(jnp.concatenate(o_heads, axis=1))
    o = jnp.concatenate(o_rows, axis=0) * (r * jax.nn.sigmoid(r))
    mix_in = jnp.concatenate([a_out, o], axis=1).astype(BF16)
    yield True

    mix = _dot(mix_in, w_out_ref[...])
    yield True

    h1 = x + gate1 * (_rms(mix) * g_post_ref[...])
    h1_ref[0, rows, :] = h1
    n2 = _rms(h1) * (g_ffn_ref[...] * (1.0 + scale2)) + shift2
    n2_ref[rows, :] = _pack_rows(n2)
    n2b = n2.astype(BF16)
    yield True

    logits = _dot(n2b, w_r_ref[...]) + b_r_ref[...]
    lane = lax.broadcasted_iota(jnp.int32, (tm, LANES), 1)
    lane_f = lane.astype(F32)
    vals = logits
    sels, tops, idxs = [], [], []
    for _ in range(TOP_K):
        m = jnp.max(vals, axis=-1, keepdims=True)
        idx = jnp.min(jnp.where(vals == m, lane_f, float(LANES)), axis=-1, keepdims=True)
        sel = lane_f == idx
        sels.append(sel)
        tops.append(m)
        idxs.append(idx)
        vals = jnp.where(sel, -jnp.inf, vals)
    es = [jnp.exp(t - tops[0]) for t in tops]
    inv = 1.0 / (es[0] + es[1] + es[2] + es[3])

    multi = jnp.zeros((tm, LANES), F32)
    for sel in sels:
        multi = multi + jnp.where(sel, 1.0, 0.0)
    l_strict = jnp.where(t_row > t_col, 1.0, 0.0).astype(BF16)
    carry = carry_box[0]
    before = []
    for s in subs:
        before.append(_dot(l_strict, multi[s].astype(BF16)) + carry)
        carry = carry + jnp.sum(multi[s], axis=0, keepdims=True)
    before = jnp.concatenate(before, axis=0)
    carry_box[0] = carry

    route = jnp.zeros((tm, LANES), F32)
    probs = jnp.zeros((tm, LANES), F32)
    for kk in range(TOP_K):
        rank_kk = jnp.sum(jnp.where(sels[kk], before, 0.0), axis=-1, keepdims=True)
        route = jnp.where(lane == kk, idxs[kk], route)
        route = jnp.where(lane == TOP_K + kk, rank_kk, route)
        probs = jnp.where(lane == kk, es[kk] * inv, probs)
    route_ref[:, rows] = route.T[0:2 * TOP_K, :].astype(jnp.int32)
    probs_ref[rows, :] = probs


def _const_spec(shape):
    return pl.BlockSpec(shape, lambda b, j: (0,) * len(shape))


def _mixer(b0, B, x, mods, g_pre, w_in_p, gmat, g_sgu, ws_cat, bs_exp, wgk_p, bgk, g_gla, w_out_b, g_post, g_ffn, w_r_p,
           b_r_p):
    _, S, D = x.shape
    T = B * S
    tm = MIX_TM
    nj = S // tm
    return pl.pallas_call(
        _mixer_kernel,
        grid=(B, nj),
        in_specs=[pl.BlockSpec((1, tm, D), lambda b, j: (b + b0, j, 0)),
                  pl.BlockSpec((1, 6, D), lambda b, j: (b + b0, 0, 0)),
                  _const_spec((1, D)), _const_spec(w_in_p.shape), _const_spec(gmat.shape), _const_spec((1, D_SGU)),
                  _const_spec(ws_cat.shape), _const_spec(bs_exp.shape), _const_spec(wgk_p.shape),
                  _const_spec((1, GLA_DK)), _const_spec((1, GLA_HEAD_V)), _const_spec(w_out_b.shape),
                  _const_spec((1, D)), _const_spec((1, D)), _const_spec(w_r_p.shape), _const_spec((1, LANES))],
        out_specs=[pl.BlockSpec((1, tm, D), lambda b, j: (b, j, 0)),
                   pl.BlockSpec((tm, D // 2), lambda b, j: (b * nj + j, 0)),
                   pl.BlockSpec((2 * TOP_K, tm), lambda b, j: (0, b * nj + j)),
                   pl.BlockSpec((tm, LANES), lambda b, j: (b * nj + j, 0)),
                   pl.BlockSpec((8, LANES), lambda b, j: (0, 0))],
        out_shape=[jax.ShapeDtypeStruct((B, S, D), F32),
                   jax.ShapeDtypeStruct((T, D // 2), jnp.uint32),
                   jax.ShapeDtypeStruct((2 * TOP_K, T), jnp.int32),
                   jax.ShapeDtypeStruct((T, LANES), F32),
                   jax.ShapeDtypeStruct((8, LANES), F32)],
        scratch_shapes=[pltpu.VMEM((GLA_HEADS, GLA_HEAD_V, LANES), F32),
                        pltpu.VMEM((1, LANES), F32)],
        compiler_params=pltpu.CompilerParams(dimension_semantics=("arbitrary", "arbitrary"),
                                             vmem_limit_bytes=VMEM_LIMIT),
        name="mixer",
    )(x, mods, g_pre, w_in_p, gmat, g_sgu, ws_cat, bs_exp, wgk_p, bgk, g_gla, w_out_b, g_post, g_ffn, w_r_p, b_r_p)


def _sc_dispatch(rows, dest_kt, n_out):
    T, D = rows.shape
    top_k = dest_kt.shape[0]
    ch = SC_CH
    cpw = T // ch // SC_WORKERS
    mesh = plsc.VectorSubcoreMesh(core_axis_name="c", subcore_axis_name="s")

    @functools.partial(
        pl.kernel, mesh=mesh,
        out_type=jax.ShapeDtypeStruct((n_out, D), rows.dtype),
        scratch_types=[pltpu.VMEM((top_k, ch), jnp.int32), pltpu.VMEM((ch, D), rows.dtype)],
    )
    def k(rows_hbm, dest_hbm, out_hbm, idx_v, rows_v):
        wid = lax.axis_index("s") * SC_CORES + lax.axis_index("c")

        @pl.loop(0, cpw)
        def _(i):
            t0 = (wid * cpw + i) * ch
            for kk in range(top_k):
                pltpu.sync_copy(dest_hbm.at[kk, pl.ds(t0, ch)], idx_v.at[kk])
            pltpu.sync_copy(rows_hbm.at[pl.ds(t0, ch)], rows_v)
            for kk in range(top_k):
                pltpu.sync_copy(rows_v, out_hbm.at[idx_v.at[kk]])

    return k(rows, dest_kt)


def _sc_gather(src, dest_kt):
    _, D = src.shape
    top_k, T = dest_kt.shape
    ch = SC_CH
    cpk = T // ch
    cpw = top_k * cpk // SC_WORKERS
    mesh = plsc.VectorSubcoreMesh(core_axis_name="c", subcore_axis_name="s")

    @functools.partial(
        pl.kernel, mesh=mesh,
        out_type=jax.ShapeDtypeStruct((top_k * T, D), src.dtype),
        scratch_types=[pltpu.VMEM((ch,), jnp.int32), pltpu.VMEM((ch, D), src.dtype)],
    )
    def k(src_hbm, idx_hbm, out_hbm, idx_v, rows_v):
        wid = lax.axis_index("s") * SC_CORES + lax.axis_index("c")

        @pl.loop(0, cpw)
        def _(i):
            c = wid * cpw + i
            kk = c // cpk
            t0 = (c - kk * cpk) * ch
            pltpu.sync_copy(idx_hbm.at[kk, pl.ds(t0, ch)], idx_v)
            pltpu.sync_copy(src_hbm.at[idx_v], rows_v)
            pltpu.sync_copy(rows_v, out_hbm.at[pl.ds(c * ch, ch)])

    return k(src, dest_kt)


def _expert_kernel(be_ref, first_ref, last_ref, nxt_ref, slot_ref, short_ref, nu_ref,
                   x_ref, wg_hbm, bg_ref, wu_hbm, bu_ref, wd_hbm, bd_ref, o_ref,
                   stage_ref, wbf_ref, sem):
    i = pl.program_id(0)
    active = i < nu_ref[0]
    has_next = nxt_ref[i] >= 0

    def weight_copies(e):
        return [pltpu.make_async_copy(w.at[e], stage_ref.at[m], sem.at[m])
                for m, w in enumerate((wg_hbm, wu_hbm, wd_hbm))]

    def cast_stage_into(slot):
        for m in range(3):
            wbf_ref[slot, m] = stage_ref[m].astype(BF16)

    @pl.when(i == 0)
    def _():
        for cp in weight_copies(be_ref[0]):
            cp.start()
        for cp in weight_copies(be_ref[0]):
            cp.wait()
        cast_stage_into(0)

    @pl.when(active & (first_ref[i] == 1) & has_next)
    def _():
        for cp in weight_copies(nxt_ref[i]):
            cp.start()

    def ffn(rows):
        slot = slot_ref[i]
        x_lo, x_hi = _unpack_rows(x_ref[rows, :])
        xb = jnp.concatenate([x_lo, x_hi], axis=1).astype(BF16)
        g = jnp.minimum(_dot(xb, wbf_ref[slot, 0]) + bg_ref[0], SWIGLU_LIMIT)
        u = jnp.clip(_dot(xb, wbf_ref[slot, 1]) + bu_ref[0], -SWIGLU_LIMIT, SWIGLU_LIMIT)
        hdn = (u + 1.0) * (g * jax.nn.sigmoid(SWIGLU_ALPHA * g))
        o_ref[rows, :] = _pack_rows(_dot(hdn.astype(BF16), wbf_ref[slot, 2]) + bd_ref[0])

    bm = x_ref.shape[0]

    @pl.when(active & (short_ref[i] == 0))
    def _():
        ffn(slice(0, bm))

    @pl.when(active & (short_ref[i] == 1))
    def _():
        ffn(slice(0, bm // 2))

    @pl.when(active & (last_ref[i] == 1) & has_next)
    def _():
        for cp in weight_copies(nxt_ref[i]):
            cp.wait()
        cast_stage_into(1 - slot_ref[i])


def _expert_schedule(counts, padded, n_blocks, bm):
    pend = jnp.cumsum(padded)
    ids = jnp.arange(N_EXPERTS, dtype=jnp.int32)
    idx = jnp.arange(n_blocks, dtype=jnp.int32)
    n_used = (pend[-1] // bm).astype(jnp.int32)
    block_e = jnp.minimum(jnp.sum((pend[None, :] <= (idx * bm)[:, None]).astype(jnp.int32), axis=1), N_EXPERTS - 1)
    onehot = block_e[:, None] == ids[None, :]

    def per_block(table):
        return jnp.sum(jnp.where(onehot, table[None, :], 0), axis=1)

    start_blk = (pend - padded) // bm
    first = (idx == per_block(start_blk)) & (idx < n_used)
    last = (idx == per_block(pend // bm) - 1) & (idx < n_used)
    later = (ids[None, :] > ids[:, None]) & (padded[None, :] > 0)
    nxt_e = jnp.min(jnp.where(later, ids[None, :], N_EXPERTS), axis=1)
    nxt_e = jnp.where(nxt_e == N_EXPERTS, -1, nxt_e)
    order = jnp.cumsum((padded > 0).astype(jnp.int32)) - 1
    real_rows = per_block(counts) - (idx - per_block(start_blk)) * bm
    short = (idx < n_used) & (real_rows <= bm // 2)
    as_i32 = lambda a: a.astype(jnp.int32)
    return (block_e, as_i32(first), as_i32(last), per_block(nxt_e), per_block(order) & 1, as_i32(short),
            n_used.reshape(1))


def _experts(schedule, xs, w_gate, b_gate, w_up, b_up, w_down, b_down):
    P, half = xs.shape
    bm = MOE_BM
    n_blocks = P // bm
    E, D, DE = w_gate.shape

    def row_map(i, be, first, last, nxt, slot, short, nu):
        return (jnp.minimum(i, nu[0] - 1), 0)

    def b_map(i, be, first, last, nxt, slot, short, nu):
        return (be[jnp.minimum(i, nu[0] - 1)], 0, 0)

    hbm = pl.BlockSpec(memory_space=pl.ANY)
    grid_spec = pltpu.PrefetchScalarGridSpec(
        num_scalar_prefetch=len(schedule),
        grid=(n_blocks,),
        in_specs=[pl.BlockSpec((bm, half), row_map),
                  hbm, pl.BlockSpec((1, 1, DE), b_map),
                  hbm, pl.BlockSpec((1, 1, DE), b_map),
                  hbm, pl.BlockSpec((1, 1, D), b_map)],
        out_specs=pl.BlockSpec((bm, half), row_map),
        scratch_shapes=[pltpu.VMEM((3, D, DE), F32),
                        pltpu.VMEM((2, 3, D, DE), BF16),
                        pltpu.SemaphoreType.DMA((3,))],
    )
    return pl.pallas_call(
        _expert_kernel,
        grid_spec=grid_spec,
        out_shape=jax.ShapeDtypeStruct((P, half), jnp.uint32),
        compiler_params=pltpu.CompilerParams(dimension_semantics=("arbitrary",), vmem_limit_bytes=VMEM_LIMIT),
        name="experts",
    )(*schedule, xs, w_gate, b_gate.reshape(E, 1, DE), w_up, b_up.reshape(E, 1, DE),
      w_down, b_down.reshape(E, 1, D))


def _combine_kernel(h1_ref, og_ref, probs_ref, mods_ref, g_ref, *rest):
    o_ref = rest[-1]
    tm = h1_ref.shape[1]
    probs = probs_ref[...]
    lane = lax.broadcasted_iota(jnp.int32, (tm, LANES), 1)
    half = og_ref.shape[2]
    f_lo = jnp.zeros((tm, half), F32)
    f_hi = jnp.zeros((tm, half), F32)
    for kk in range(TOP_K):
        pk = jnp.sum(jnp.where(lane == kk, probs, 0.0), axis=-1, keepdims=True)
        o_lo, o_hi = _unpack_rows(og_ref[kk])
        f_lo = f_lo + pk * o_lo
        f_hi = f_hi + pk * o_hi
    f = jnp.concatenate([f_lo, f_hi], axis=1)
    gate2 = mods_ref[0][5:6]
    o_ref[0] = h1_ref[0] + gate2 * (_rms(f) * g_ref[...])


def _combine(b0, b_total, h1, og, probs, mods, g_post_ffn, out_so_far):
    B, S, D = h1.shape
    tm = CMB_TM
    nj = S // tm
    in_specs = [pl.BlockSpec((1, tm, D), lambda b, j: (b, j, 0)),
                pl.BlockSpec((TOP_K, tm, D // 2), lambda b, j: (0, b * nj + j, 0)),
                pl.BlockSpec((tm, LANES), lambda b, j: (b * nj + j, 0)),
                pl.BlockSpec((1, 6, D), lambda b, j: (b + b0, 0, 0)),
                pl.BlockSpec((1, D), lambda b, j: (0, 0))]
    args = [h1, og, probs, mods, g_post_ffn]
    aliases = {}
    if out_so_far is not None:
        in_specs.append(pl.BlockSpec(memory_space=pl.ANY))
        args.append(out_so_far)
        aliases = {len(args) - 1: 0}
    return pl.pallas_call(
        _combine_kernel,
        grid=(B, nj),
        in_specs=in_specs,
        out_specs=pl.BlockSpec((1, tm, D), lambda b, j: (b + b0, j, 0)),
        out_shape=jax.ShapeDtypeStruct((b_total, S, D), F32),
        input_output_aliases=aliases,
        compiler_params=pltpu.CompilerParams(dimension_semantics=("arbitrary", "arbitrary"),
                                             vmem_limit_bytes=VMEM_LIMIT),
        name="combine",
    )(*args)


def _layer(h, c_pad, w_ada, b_ada, g_pre_mix, w_in, g_sgu_v, w_s, b_s, w_gk2, b_gk, g_gla_out, w_out, g_post_mix,
           g_pre_ffn, w_router, b_router, w_gate, b_gate, w_up, b_up, w_down, b_down, g_post_ffn):
    B, S, D = h.shape

    mods = _ada(c_pad, w_ada, b_ada.reshape(1, -1))[:B].reshape(B, 6, D)

    w_in_p = jnp.pad(w_in, ((0, 0), (0, D_IN_PAD - w_in.shape[1]))).astype(BF16)
    head_of = jnp.arange(D_SGU) // SGU_HEAD_DIM
    gmat = jnp.where(head_of[:, None] == head_of[None, :], 1.0 / SGU_HEAD_DIM, 0.0).astype(BF16)
    ws_cat = w_s.reshape(SGU_HEADS // 2, 2, SGU_CHUNK, SGU_CHUNK).transpose(0, 2, 1, 3).reshape(
        SGU_HEADS // 2, SGU_CHUNK, 2 * SGU_CHUNK)
    bs_exp = jnp.repeat(b_s.T, SGU_HEAD_DIM, axis=1)
    wgk_p = jnp.pad(w_gk2, ((0, LANES - GLA_GATE_RANK), (0, 0))).astype(BF16)
    w_r_p = jnp.pad(w_router, ((0, 0), (0, LANES - N_EXPERTS))).astype(BF16)
    b_r_p = jnp.concatenate([b_router, jnp.full((LANES - N_EXPERTS,), NEG_BIG, F32)]).reshape(1, LANES)

    w_out_b = w_out.astype(BF16)

    n_groups = N_GROUPS if B % N_GROUPS == 0 else 1
    bg = B // n_groups
    tg = bg * S
    result = None
    for gi in range(n_groups):
        b0 = gi * bg
        h1, n2, route, probs, cnt = _mixer(
            b0, bg, h, mods, g_pre_mix.reshape(1, D), w_in_p, gmat, g_sgu_v.reshape(1, D_SGU), ws_cat, bs_exp, wgk_p,
            b_gk.reshape(1, GLA_DK), g_gla_out.reshape(1, GLA_HEAD_V), w_out_b, g_post_mix.reshape(1, D),
            g_pre_ffn.reshape(1, D), w_r_p, b_r_p)

        bm = MOE_BM
        counts = cnt[0, :N_EXPERTS].astype(jnp.int32)
        padded = ((counts + bm - 1) // bm) * bm
        pend = jnp.cumsum(padded)
        pstart = pend - padded
        e_kt, rank_kt = route[:TOP_K], route[TOP_K:]
        onehot = e_kt[:, :, None] == jnp.arange(N_EXPERTS, dtype=jnp.int32)[None, None, :]
        dest_kt = rank_kt + jnp.sum(jnp.where(onehot, pstart[None, None, :], 0), axis=-1)
        n_blocks = tg * TOP_K // bm + N_EXPERTS
        schedule = _expert_schedule(counts, padded, n_blocks, bm)

        xs = _sc_dispatch(n2, dest_kt, n_blocks * bm)
        out = _experts(schedule, xs, w_gate, b_gate, w_up, b_up, w_down, b_down)
        og = _sc_gather(out, dest_kt).reshape(TOP_K, tg, D // 2)
        result = _combine(b0, B, h1, og, probs, mods, g_post_ffn.reshape(1, D), result)
    return result


def kernel(x, c, w_ada, b_ada, g_pre_mix, w_in, g_sgu_v, w_s, b_s, w_gk2, b_gk, g_gla_out, w_out, g_post_mix, g_pre_ffn, w_router, b_router, w_gate, b_gate, w_up, b_up, w_down, b_down, g_post_ffn):
    B = x.shape[0]
    c_pad = jnp.pad(c, ((0, -B % 8), (0, 0)))
    h = x
    for l in range(w_ada.shape[0]):
        h = _layer(h, c_pad, w_ada[l], b_ada[l], g_pre_mix[l], w_in[l], g_sgu_v[l], w_s[l], b_s[l], w_gk2[l], b_gk[l],
                   g_gla_out[l], w_out[l], g_post_mix[l], g_pre_ffn[l], w_router[l], b_router[l], w_gate[l], b_gate[l],
                   w_up[l], b_up[l], w_down[l], b_down[l], g_post_ffn[l])
    return h
```

```python
import functools

import jax
import jax.numpy as jnp
from jax import lax
from jax.experimental import pallas as pl
from jax.experimental.pallas import tpu as pltpu
from jax.experimental.pallas import tpu_sc as plsc

F32 = jnp.float32
BF16 = jnp.bfloat16

D_MODEL = 1024
D_SGU = 512
SGU_HEADS = 8
SGU_HEAD_DIM = 64
SGU_CHUNK = 128
D_GLA = 512
GLA_HEADS = 4
GLA_DK = 256
GLA_HEAD_K = 64
GLA_HEAD_V = 128
GLA_GATE_RANK = 16
GLA_TAU = 16.0
GLA_CHUNK = 64
N_EXPERTS = 32
TOP_K = 4
SWIGLU_LIMIT = 7.0
SWIGLU_ALPHA = 1.702
EPS = 1e-6

LANES = 128
D_IN_MAIN = 2 * D_SGU + 2 * GLA_DK + 2 * D_GLA
D_IN_PAD = D_IN_MAIN + LANES

MIX_TM = 512
MIX_TILE = 256
MIX_SKEW = 1
MIX_SUB = 256
MOE_BM = 512
MOE_QUARTERS = 4
CMB_TM = 1024
N_GROUPS = 2
SC_CORES = 2
SC_SUBCORES = 16
SC_WORKERS = SC_CORES * SC_SUBCORES
SC_CH = 64
NEG_BIG = -1e30

VMEM_LIMIT = 56 * 1024 * 1024


def _ada_kernel(c_ref, w_ref, b_ref, o_ref):
    c = c_ref[...]
    sc = c * jax.nn.sigmoid(c)
    o_ref[...] = jnp.dot(sc.astype(BF16), w_ref[...].astype(BF16), preferred_element_type=F32) + b_ref[...]


def _ada(c_pad, w_ada, b_ada):
    rows, d = c_pad.shape
    n = w_ada.shape[1]
    tn = 1536
    return pl.pallas_call(
        _ada_kernel,
        grid=(n // tn,),
        in_specs=[pl.BlockSpec((rows, d), lambda j: (0, 0)),
                  pl.BlockSpec((d, tn), lambda j: (0, j)),
                  pl.BlockSpec((1, tn), lambda j: (0, j))],
        out_specs=pl.BlockSpec((rows, tn), lambda j: (0, j)),
        out_shape=jax.ShapeDtypeStruct((rows, n), F32),
        compiler_params=pltpu.CompilerParams(dimension_semantics=("arbitrary",), vmem_limit_bytes=VMEM_LIMIT),
        name="ada",
    )(c_pad, w_ada, b_ada)


def _rms(x):
    return x * lax.rsqrt(jnp.mean(x * x, axis=-1, keepdims=True) + EPS)


def _gelu_tanh(x):
    return 0.5 * x * (1.0 + jnp.tanh(0.7978845608028654 * (x + 0.044715 * (x * x * x))))


def _log_sigmoid(z):
    return jnp.minimum(z, 0.0) - jnp.log(1.0 + jnp.exp(-jnp.abs(z)))


def _dot(a, b):
    return jnp.dot(a, b, preferred_element_type=F32)


def _dot_nt(a, b):
    return lax.dot_general(a, b, (((1,), (1,)), ((), ())), preferred_element_type=F32)


def _dot_tn(a, b):
    return lax.dot_general(a, b, (((0,), (0,)), ((), ())), preferred_element_type=F32)


def _pack_rows(x):
    n = x.shape[1] // 2
    lo = lax.bitcast_convert_type(x[:, :n].astype(BF16).astype(F32), jnp.uint32)
    hi = lax.bitcast_convert_type(x[:, n:].astype(BF16).astype(F32), jnp.uint32)
    return (lo >> 16) | (hi & jnp.uint32(0xFFFF0000))


def _unpack_rows(p):
    lo = lax.bitcast_convert_type(p << 16, F32)
    hi = lax.bitcast_convert_type(p & jnp.uint32(0xFFFF0000), F32)
    return lo, hi


def _split_dot(l_bf, a):
    hi = a.astype(BF16)
    lo = (a - hi.astype(F32)).astype(BF16)
    return _dot(l_bf, hi) + _dot(l_bf, lo)


def _mixer_kernel(x_ref, mods_ref, g_pre_ref, w_in_ref, gmat_ref, g_sgu_ref, ws_ref, bs_ref, wgk_ref, bgk_ref,
                  g_gla_ref, w_out_ref, g_post_ref, g_ffn_ref, w_r_ref, b_r_ref,
                  h1_ref, n2_ref, route_ref, probs_ref, cnt_ref,
                  st_ref, carry_ref):
    b = pl.program_id(0)
    j = pl.program_id(1)

    @pl.when(j == 0)
    def _():
        st_ref[...] = jnp.zeros_like(st_ref)

    @pl.when((b == 0) & (j == 0))
    def _():
        carry_ref[...] = jnp.zeros_like(carry_ref)

    carry_box = [carry_ref[...]]
    tiles = [_mixer_tile(slice(s * MIX_TILE, (s + 1) * MIX_TILE), carry_box,
                         x_ref, mods_ref, g_pre_ref, w_in_ref, gmat_ref, g_sgu_ref, ws_ref, bs_ref, wgk_ref, bgk_ref,
                         g_gla_ref, w_out_ref, g_post_ref, g_ffn_ref, w_r_ref, b_r_ref,
                         h1_ref, n2_ref, route_ref, probs_ref, st_ref)
             for s in range(x_ref.shape[1] // MIX_TILE)]
    alive = [True] * len(tiles)
    t = 0
    while any(alive):
        for s, tile in enumerate(tiles):
            if alive[s] and t >= s * MIX_SKEW:
                alive[s] = next(tile, None) is not None
        t += 1
    carry_ref[...] = carry_box[0]
    cnt_ref[...] = jnp.broadcast_to(carry_box[0], cnt_ref.shape)


def _mixer_tile(rows, carry_box, x_ref, mods_ref, g_pre_ref, w_in_ref, gmat_ref, g_sgu_ref, ws_ref, bs_ref, wgk_ref,
                bgk_ref, g_gla_ref, w_out_ref, g_post_ref, g_ffn_ref, w_r_ref, b_r_ref,
                h1_ref, n2_ref, route_ref, probs_ref, st_ref):
    tm = rows.stop - rows.start
    mods = mods_ref[0]
    shift1, scale1, gate1 = mods[0:1], mods[1:2], mods[2:3]
    shift2, scale2 = mods[3:4], mods[4:5]

    x = x_ref[0, rows, :]
    n = _rms(x) * (g_pre_ref[...] * (1.0 + scale1)) + shift1
    nb = n.astype(BF16)
    yield True

    pu = _dot(nb, w_in_ref[:, 0:D_SGU])
    pv = _dot(nb, w_in_ref[:, D_SGU:2 * D_SGU])
    yield True

    u = _gelu_tanh(pu)
    v = _gelu_tanh(pv)
    msv = _dot((v * v).astype(BF16), gmat_ref[...])
    vh = v * lax.rsqrt(msv + EPS) * g_sgu_ref[...]
    yield True

    qk = _dot(nb, w_in_ref[:, 2 * D_SGU:2 * D_SGU + 2 * GLA_DK])
    vv = _dot(nb, w_in_ref[:, 2 * D_SGU + 2 * GLA_DK:2 * D_SGU + 2 * GLA_DK + D_GLA]).astype(BF16)
    r = _dot(nb, w_in_ref[:, D_IN_MAIN - D_GLA:D_IN_MAIN])
    g_low = _dot(nb, w_in_ref[:, D_IN_MAIN:D_IN_PAD])
    yield True

    n_sc = tm // SGU_CHUNK
    lane_c = lax.broadcasted_iota(jnp.int32, (SGU_CHUNK, LANES), 1)
    low_half = lane_c < SGU_HEAD_DIM
    w_row = lax.broadcasted_iota(jnp.int32, (SGU_CHUNK, 2 * SGU_CHUNK), 0)
    w_col = lax.broadcasted_iota(jnp.int32, (SGU_CHUNK, 2 * SGU_CHUNK), 1) & (SGU_CHUNK - 1)
    a_cols = []
    for p in range(SGU_HEADS // 2):
        wcat = jnp.where(w_row >= w_col, ws_ref[p], 0.0).astype(BF16)
        rhs = []
        for ci in range(n_sc):
            vp = vh[ci * SGU_CHUNK:(ci + 1) * SGU_CHUNK, p * LANES:(p + 1) * LANES]
            rhs.append(jnp.concatenate([jnp.where(low_half, vp, 0.0), jnp.where(low_half, 0.0, vp)],
                                       axis=0).astype(BF16))
        res = _dot(wcat, jnp.concatenate(rhs, axis=1))
        bias = bs_ref[:, p * LANES:(p + 1) * LANES]
        a_cols.append(jnp.concatenate(
            [res[:, ci * LANES:(ci + 1) * LANES] + bias for ci in range(n_sc)], axis=0))
    a_out = u * jnp.concatenate(a_cols, axis=1)

    q = qk[:, :GLA_DK] * (GLA_HEAD_K ** -0.5)
    k = qk[:, GLA_DK:]
    z = _dot(g_low.astype(BF16), wgk_ref[...]) + bgk_ref[...]
    log_a = _log_sigmoid(z) * (1.0 / GLA_TAU)

    sub = min(tm, MIX_SUB)
    subs = [slice(s * sub, (s + 1) * sub) for s in range(tm // sub)]
    t_row = lax.broadcasted_iota(jnp.int32, (sub, sub), 0)
    t_col = lax.broadcasted_iota(jnp.int32, (sub, sub), 1)
    same_chunk = (t_row >> 6) == (t_col >> 6)
    l_cum = jnp.where(same_chunk & (t_row >= t_col), 1.0, 0.0).astype(BF16)
    l_all = jnp.where(same_chunk, 1.0, 0.0).astype(BF16)
    bcum = jnp.concatenate([_split_dot(l_cum, log_a[s]) for s in subs], axis=0)
    blast = jnp.concatenate([_split_dot(l_all, log_a[s]) for s in subs], axis=0)
    q_s = q * jnp.exp(bcum)
    k_s = (k * jnp.exp(-bcum)).astype(BF16)
    k_dec = (k * jnp.exp(blast - bcum)).astype(BF16)
    decay = jnp.exp(blast)
    yield True

    lane_g = lax.broadcasted_iota(jnp.int32, (GLA_CHUNK, LANES), 1)
    causal = (lax.broadcasted_iota(jnp.int32, (GLA_CHUNK, GLA_CHUNK), 0)
              >= lax.broadcasted_iota(jnp.int32, (GLA_CHUNK, GLA_CHUNK), 1))
    g_gla = g_gla_ref[...]
    o_rows = []
    for ci in range(tm // GLA_CHUNK):
        rs = slice(ci * GLA_CHUNK, (ci + 1) * GLA_CHUNK)
        o_heads = []
        for h in range(GLA_HEADS):
            pl_ = slice((h // 2) * LANES, (h // 2 + 1) * LANES)
            own = (lane_g >= GLA_HEAD_K) if (h % 2) else (lane_g < GLA_HEAD_K)
            qm = jnp.where(own, q_s[rs, pl_], 0.0).astype(BF16)
            v_h = vv[rs, h * GLA_HEAD_V:(h + 1) * GLA_HEAD_V]
            scores = jnp.where(causal, _dot_nt(qm, k_s[rs, pl_]), 0.0)
            intra = _dot(scores.astype(BF16), v_h)
            st = st_ref[h]
            inter = _dot_nt(qm, st.astype(BF16))
            st_ref[h] = st * decay[ci * GLA_CHUNK:ci * GLA_CHUNK + 1, pl_] + _dot_tn(v_h, k_dec[rs, pl_])
            o_heads.append(_rms(intra + inter) * g_gla)
        o_rows.append(jnp.concatenate(o_heads, axis=1))
    o = jnp.concatenate(o_rows, axis=0) * (r * jax.nn.sigmoid(r))
    mix_in = jnp.concatenate([a_out, o], axis=1).astype(BF16)
    yield True

    mix = _dot(mix_in, w_out_ref[...])
    yield True

    h1 = x + gate1 * (_rms(mix) * g_post_ref[...])
    h1_ref[0, rows, :] = h1
    n2 = _rms(h1) * (g_ffn_ref[...] * (1.0 + scale2)) + shift2
    n2_ref[rows, :] = _pack_rows(n2)
    n2b = n2.astype(BF16)
    yield True

    logits = _dot(n2b, w_r_ref[...]) + b_r_ref[...]
    lane = lax.broadcasted_iota(jnp.int32, (tm, LANES), 1)
    lane_f = lane.astype(F32)
    vals = logits
    sels, tops, idxs = [], [], []
    for _ in range(TOP_K):
        m = jnp.max(vals, axis=-1, keepdims=True)
        idx = jnp.min(jnp.where(vals == m, lane_f, float(LANES)), axis=-1, keepdims=True)
        sel = lane_f == idx
        sels.append(sel)
        tops.append(m)
        idxs.append(idx)
        vals = jnp.where(sel, -jnp.inf, vals)
    es = [jnp.exp(t - tops[0]) for t in tops]
    inv = 1.0 / (es[0] + es[1] + es[2] + es[3])

    multi = jnp.zeros((tm, LANES), F32)
    for sel in sels:
        multi = multi + jnp.where(sel, 1.0, 0.0)
    l_strict = jnp.where(t_row > t_col, 1.0, 0.0).astype(BF16)
    carry = carry_box[0]
    before = []
    for s in subs:
        before.append(_dot(l_strict, multi[s].astype(BF16)) + carry)
        carry = carry + jnp.sum(multi[s], axis=0, keepdims=True)
    before = jnp.concatenate(before, axis=0)
    carry_box[0] = carry

    route = jnp.zeros((tm, LANES), F32)
    probs = jnp.zeros((tm, LANES), F32)
    for kk in range(TOP_K):
        rank_kk = jnp.sum(jnp.where(sels[kk], before, 0.0), axis=-1, keepdims=True)
        route = jnp.where(lane == kk, idxs[kk], route)
        route = jnp.where(lane == TOP_K + kk, rank_kk, route)
        probs = jnp.where(lane == kk, es[kk] * inv, probs)
    route_ref[:, rows] = route.T[0:2 * TOP_K, :].astype(jnp.int32)
    probs_ref[rows, :] = probs


def _const_spec(shape):
    return pl.BlockSpec(shape, lambda b, j: (0,) * len(shape))


def _mixer(b0, B, x, mods, g_pre, w_in_p, gmat, g_sgu, ws_cat, bs_exp, wgk_p, bgk, g_gla, w_out_b, g_post, g_ffn, w_r_p,
           b_r_p):
    _, S, D = x.shape
    T = B * S
    tm = MIX_TM
    nj = S // tm
    return pl.pallas_call(
        _mixer_kernel,
        grid=(B, nj),
        in_specs=[pl.BlockSpec((1, tm, D), lambda b, j: (b + b0, j, 0)),
                  pl.BlockSpec((1, 6, D), lambda b, j: (b + b0, 0, 0)),
                  _const_spec((1, D)), _const_spec(w_in_p.shape), _const_spec(gmat.shape), _const_spec((1, D_SGU)),
                  _const_spec(ws_cat.shape), _const_spec(bs_exp.shape), _const_spec(wgk_p.shape),
                  _const_spec((1, GLA_DK)), _const_spec((1, GLA_HEAD_V)), _const_spec(w_out_b.shape),
                  _const_spec((1, D)), _const_spec((1, D)), _const_spec(w_r_p.shape), _const_spec((1, LANES))],
        out_specs=[pl.BlockSpec((1, tm, D), lambda b, j: (b, j, 0)),
                   pl.BlockSpec((tm, D // 2), lambda b, j: (b * nj + j, 0)),
                   pl.BlockSpec((2 * TOP_K, tm), lambda b, j: (0, b * nj + j)),
                   pl.BlockSpec((tm, LANES), lambda b, j: (b * nj + j, 0)),
                   pl.BlockSpec((8, LANES), lambda b, j: (0, 0))],
        out_shape=[jax.ShapeDtypeStruct((B, S, D), F32),
                   jax.ShapeDtypeStruct((T, D // 2), jnp.uint32),
                   jax.ShapeDtypeStruct((2 * TOP_K, T), jnp.int32),
                   jax.ShapeDtypeStruct((T, LANES), F32),
                   jax.ShapeDtypeStruct((8, LANES), F32)],
        scratch_shapes=[pltpu.VMEM((GLA_HEADS, GLA_HEAD_V, LANES), F32),
                        pltpu.VMEM((1, LANES), F32)],
        compiler_params=pltpu.CompilerParams(dimension_semantics=("arbitrary", "arbitrary"),
                                             vmem_limit_bytes=VMEM_LIMIT),
        name="mixer",
    )(x, mods, g_pre, w_in_p, gmat, g_sgu, ws_cat, bs_exp, wgk_p, bgk, g_gla, w_out_b, g_post, g_ffn, w_r_p, b_r_p)


def _sc_dispatch(rows, dest_kt, n_out):
    T, D = rows.shape
    top_k = dest_kt.shape[0]
    ch = SC_CH
    cpw = T // ch // SC_WORKERS
    mesh = plsc.VectorSubcoreMesh(core_axis_name="c", subcore_axis_name="s")

    @functools.partial(
        pl.kernel, mesh=mesh,
        out_type=jax.ShapeDtypeStruct((n_out, D), rows.dtype),
        scratch_types=[pltpu.VMEM((top_k, ch), jnp.int32), pltpu.VMEM((ch, D), rows.dtype)],
    )
    def k(rows_hbm, dest_hbm, out_hbm, idx_v, rows_v):
        wid = lax.axis_index("s") * SC_CORES + lax.axis_index("c")

        @pl.loop(0, cpw)
        def _(i):
            t0 = (wid * cpw + i) * ch
            for kk in range(top_k):
                pltpu.sync_copy(dest_hbm.at[kk, pl.ds(t0, ch)], idx_v.at[kk])
            pltpu.sync_copy(rows_hbm.at[pl.ds(t0, ch)], rows_v)
            for kk in range(top_k):
                pltpu.sync_copy(rows_v, out_hbm.at[idx_v.at[kk]])

    return k(rows, dest_kt)


def _sc_gather(src, dest_kt):
    _, D = src.shape
    top_k, T = dest_kt.shape
    ch = SC_CH
    cpk = T // ch
    cpw = top_k * cpk // SC_WORKERS
    mesh = plsc.VectorSubcoreMesh(core_axis_name="c", subcore_axis_name="s")

    @functools.partial(
        pl.kernel, mesh=mesh,
        out_type=jax.ShapeDtypeStruct((top_k * T, D), src.dtype),
        scratch_types=[pltpu.VMEM((ch,), jnp.int32), pltpu.VMEM((ch, D), src.dtype)],
    )
    def k(src_hbm, idx_hbm, out_hbm, idx_v, rows_v):
        wid = lax.axis_index("s") * SC_CORES + lax.axis_index("c")

        @pl.loop(0, cpw)
        def _(i):
            c = wid * cpw + i
            kk = c // cpk
            t0 = (c - kk * cpk) * ch
            pltpu.sync_copy(idx_hbm.at[kk, pl.ds(t0, ch)], idx_v)
            pltpu.sync_copy(src_hbm.at[idx_v], rows_v)
            pltpu.sync_copy(rows_v, out_hbm.at[pl.ds(c * ch, ch)])

    return k(src, dest_kt)


def _expert_kernel(be_ref, first_ref, last_ref, nxt_ref, slot_ref, quarters_ref, nu_ref,
                   x_ref, wg_hbm, bg_ref, wu_hbm, bu_ref, wd_hbm, bd_ref, o_ref,
                   stage_ref, wbf_ref, sem):
    i = pl.program_id(0)
    active = i < nu_ref[0]
    has_next = nxt_ref[i] >= 0

    def weight_copies(e):
        return [pltpu.make_async_copy(w.at[e], stage_ref.at[m], sem.at[m])
                for m, w in enumerate((wg_hbm, wu_hbm, wd_hbm))]

    def cast_stage_into(slot):
        for m in range(3):
            wbf_ref[slot, m] = stage_ref[m].astype(BF16)

    @pl.when(i == 0)
    def _():
        for cp in weight_copies(be_ref[0]):
            cp.start()
        for cp in weight_copies(be_ref[0]):
            cp.wait()
        cast_stage_into(0)

    @pl.when(active & (first_ref[i] == 1) & has_next)
    def _():
        for cp in weight_copies(nxt_ref[i]):
            cp.start()

    def ffn(rows):
        slot = slot_ref[i]
        x_lo, x_hi = _unpack_rows(x_ref[rows, :])
        xb = jnp.concatenate([x_lo, x_hi], axis=1).astype(BF16)
        g = jnp.minimum(_dot(xb, wbf_ref[slot, 0]) + bg_ref[0], SWIGLU_LIMIT)
        u = jnp.clip(_dot(xb, wbf_ref[slot, 1]) + bu_ref[0], -SWIGLU_LIMIT, SWIGLU_LIMIT)
        hdn = (u + 1.0) * (g * jax.nn.sigmoid(SWIGLU_ALPHA * g))
        o_ref[rows, :] = _pack_rows(_dot(hdn.astype(BF16), wbf_ref[slot, 2]) + bd_ref[0])

    bm = x_ref.shape[0]
    for nq in range(1, MOE_QUARTERS + 1):
        @pl.when(active & (quarters_ref[i] == nq))
        def _(nq=nq):
            ffn(slice(0, nq * bm // MOE_QUARTERS))

    @pl.when(active & (last_ref[i] == 1) & has_next)
    def _():
        for cp in weight_copies(nxt_ref[i]):
            cp.wait()
        cast_stage_into(1 - slot_ref[i])


def _expert_schedule(counts, padded, n_blocks, bm):
    pend = jnp.cumsum(padded)
    ids = jnp.arange(N_EXPERTS, dtype=jnp.int32)
    idx = jnp.arange(n_blocks, dtype=jnp.int32)
    n_used = (pend[-1] // bm).astype(jnp.int32)
    block_e = jnp.minimum(jnp.sum((pend[None, :] <= (idx * bm)[:, None]).astype(jnp.int32), axis=1), N_EXPERTS - 1)
    onehot = block_e[:, None] == ids[None, :]

    def per_block(table):
        return jnp.sum(jnp.where(onehot, table[None, :], 0), axis=1)

    start_blk = (pend - padded) // bm
    first = (idx == per_block(start_blk)) & (idx < n_used)
    last = (idx == per_block(pend // bm) - 1) & (idx < n_used)
    later = (ids[None, :] > ids[:, None]) & (padded[None, :] > 0)
    nxt_e = jnp.min(jnp.where(later, ids[None, :], N_EXPERTS), axis=1)
    nxt_e = jnp.where(nxt_e == N_EXPERTS, -1, nxt_e)
    order = jnp.cumsum((padded > 0).astype(jnp.int32)) - 1
    real_rows = per_block(counts) - (idx - per_block(start_blk)) * bm
    qrows = bm // MOE_QUARTERS
    quarters = jnp.clip((real_rows + qrows - 1) // qrows, 1, MOE_QUARTERS)
    as_i32 = lambda a: a.astype(jnp.int32)
    return (block_e, as_i32(first), as_i32(last), per_block(nxt_e), per_block(order) & 1, as_i32(quarters),
            n_used.reshape(1))


def _experts(schedule, xs, w_gate, b_gate, w_up, b_up, w_down, b_down):
    P, half = xs.shape
    bm = MOE_BM
    n_blocks = P // bm
    E, D, DE = w_gate.shape

    def row_map(i, be, first, last, nxt, slot, quarters, nu):
        return (jnp.minimum(i, nu[0] - 1), 0)

    def b_map(i, be, first, last, nxt, slot, quarters, nu):
        return (be[jnp.minimum(i, nu[0] - 1)], 0, 0)

    hbm = pl.BlockSpec(memory_space=pl.ANY)
    grid_spec = pltpu.PrefetchScalarGridSpec(
        num_scalar_prefetch=len(schedule),
        grid=(n_blocks,),
        in_specs=[pl.BlockSpec((bm, half), row_map),
                  hbm, pl.BlockSpec((1, 1, DE), b_map),
                  hbm, pl.BlockSpec((1, 1, DE), b_map),
                  hbm, pl.BlockSpec((1, 1, D), b_map)],
        out_specs=pl.BlockSpec((bm, half), row_map),
        scratch_shapes=[pltpu.VMEM((3, D, DE), F32),
                        pltpu.VMEM((2, 3, D, DE), BF16),
                        pltpu.SemaphoreType.DMA((3,))],
    )
    return pl.pallas_call(
        _expert_kernel,
        grid_spec=grid_spec,
        out_shape=jax.ShapeDtypeStruct((P, half), jnp.uint32),
        compiler_params=pltpu.CompilerParams(dimension_semantics=("arbitrary",), vmem_limit_bytes=VMEM_LIMIT),
        name="experts",
    )(*schedule, xs, w_gate, b_gate.reshape(E, 1, DE), w_up, b_up.reshape(E, 1, DE),
      w_down, b_down.reshape(E, 1, D))


def _combine_kernel(h1_ref, og_ref, probs_ref, mods_ref, g_ref, *rest):
    o_ref = rest[-1]
    tm = h1_ref.shape[1]
    probs = probs_ref[...]
    lane = lax.broadcasted_iota(jnp.int32, (tm, LANES), 1)
    half = og_ref.shape[2]
    f_lo = jnp.zeros((tm, half), F32)
    f_hi = jnp.zeros((tm, half), F32)
    for kk in range(TOP_K):
        pk = jnp.sum(jnp.where(lane == kk, probs, 0.0), axis=-1, keepdims=True)
        o_lo, o_hi = _unpack_rows(og_ref[kk])
        f_lo = f_lo + pk * o_lo
        f_hi = f_hi + pk * o_hi
    f = jnp.concatenate([f_lo, f_hi], axis=1)
    gate2 = mods_ref[0][5:6]
    o_ref[0] = h1_ref[0] + gate2 * (_rms(f) * g_ref[...])


def _combine(b0, b_total, h1, og, probs, mods, g_post_ffn, out_so_far):
    B, S, D = h1.shape
    tm = CMB_TM
    nj = S // tm
    in_specs = [pl.BlockSpec((1, tm, D), lambda b, j: (b, j, 0)),
                pl.BlockSpec((TOP_K, tm, D // 2), lambda b, j: (0, b * nj + j, 0)),
                pl.BlockSpec((tm, LANES), lambda b, j: (b * nj + j, 0)),
                pl.BlockSpec((1, 6, D), lambda b, j: (b + b0, 0, 0)),
                pl.BlockSpec((1, D), lambda b, j: (0, 0))]
    args = [h1, og, probs, mods, g_post_ffn]
    aliases = {}
    if out_so_far is not None:
        in_specs.append(pl.BlockSpec(memory_space=pl.ANY))
        args.append(out_so_far)
        aliases = {len(args) - 1: 0}
    return pl.pallas_call(
        _combine_kernel,
        grid=(B, nj),
        in_specs=in_specs,
        out_specs=pl.BlockSpec((1, tm, D), lambda b, j: (b + b0, j, 0)),
        out_shape=jax.ShapeDtypeStruct((b_total, S, D), F32),
        input_output_aliases=aliases,
        compiler_params=pltpu.CompilerParams(dimension_semantics=("arbitrary", "arbitrary"),
                                             vmem_limit_bytes=VMEM_LIMIT),
        name="combine",
    )(*args)


def _layer(h, c_pad, w_ada, b_ada, g_pre_mix, w_in, g_sgu_v, w_s, b_s, w_gk2, b_gk, g_gla_out, w_out, g_post_mix,
           g_pre_ffn, w_router, b_router, w_gate, b_gate, w_up, b_up, w_down, b_down, g_post_ffn):
    B, S, D = h.shape

    mods = _ada(c_pad, w_ada, b_ada.reshape(1, -1))[:B].reshape(B, 6, D)

    w_in_p = jnp.pad(w_in, ((0, 0), (0, D_IN_PAD - w_in.shape[1]))).astype(BF16)
    head_of = jnp.arange(D_SGU) // SGU_HEAD_DIM
    gmat = jnp.where(head_of[:, None] == head_of[None, :], 1.0 / SGU_HEAD_DIM, 0.0).astype(BF16)
    ws_cat = w_s.reshape(SGU_HEADS // 2, 2, SGU_CHUNK, SGU_CHUNK).transpose(0, 2, 1, 3).reshape(
        SGU_HEADS // 2, SGU_CHUNK, 2 * SGU_CHUNK)
    bs_exp = jnp.repeat(b_s.T, SGU_HEAD_DIM, axis=1)
    wgk_p = jnp.pad(w_gk2, ((0, LANES - GLA_GATE_RANK), (0, 0))).astype(BF16)
    w_r_p = jnp.pad(w_router, ((0, 0), (0, LANES - N_EXPERTS))).astype(BF16)
    b_r_p = jnp.concatenate([b_router, jnp.full((LANES - N_EXPERTS,), NEG_BIG, F32)]).reshape(1, LANES)

    w_out_b = w_out.astype(BF16)

    n_groups = N_GROUPS if B % N_GROUPS == 0 else 1
    bg = B // n_groups
    tg = bg * S
    result = None
    for gi in range(n_groups):
        b0 = gi * bg
        h1, n2, route, probs, cnt = _mixer(
            b0, bg, h, mods, g_pre_mix.reshape(1, D), w_in_p, gmat, g_sgu_v.reshape(1, D_SGU), ws_cat, bs_exp, wgk_p,
            b_gk.reshape(1, GLA_DK), g_gla_out.reshape(1, GLA_HEAD_V), w_out_b, g_post_mix.reshape(1, D),
            g_pre_ffn.reshape(1, D), w_r_p, b_r_p)

        bm = MOE_BM
        counts = cnt[0, :N_EXPERTS].astype(jnp.int32)
        padded = ((counts + bm - 1) // bm) * bm
        pend = jnp.cumsum(padded)
        pstart = pend - padded
        e_kt, rank_kt = route[:TOP_K], route[TOP_K:]
        onehot = e_kt[:, :, None] == jnp.arange(N_EXPERTS, dtype=jnp.int32)[None, None, :]
        dest_kt = rank_kt + jnp.sum(jnp.where(onehot, pstart[None, None, :], 0), axis=-1)
        n_blocks = tg * TOP_K // bm + N_EXPERTS
        schedule = _expert_schedule(counts, padded, n_blocks, bm)

        xs = _sc_dispatch(n2, dest_kt, n_blocks * bm)
        out = _experts(schedule, xs, w_gate, b_gate, w_up, b_up, w_down, b_down)
        og = _sc_gather(out, dest_kt).reshape(TOP_K, tg, D // 2)
        result = _combine(b0, B, h1, og, probs, mods, g_post_ffn.reshape(1, D), result)
    return result


def kernel(x, c, w_ada, b_ada, g_pre_mix, w_in, g_sgu_v, w_s, b_s, w_gk2, b_gk, g_gla_out, w_out, g_post_mix, g_pre_ffn, w_router, b_router, w_gate, b_gate, w_up, b_up, w_down, b_down, g_post_ffn):
    B = x.shape[0]
    c_pad = jnp.pad(c, ((0, -B % 8), (0, 0)))
    h = x
    for l in range(w_ada.shape[0]):
        h = _layer(h, c_pad, w_ada[l], b_ada[l], g_pre_mix[l], w_in[l], g_sgu_v[l], w_s[l], b_s[l], w_gk2[l], b_gk[l],
                   g_gla_out[l], w_out[l], g_post_mix[l], g_pre_ffn[l], w_router[l], b_router[l], w_gate[l], b_gate[l],
                   w_up[l], b_up[l], w_down[l], b_down[l], g_post_ffn[l])
    return h
```

```python
import functools

import jax
import jax.numpy as jnp
from jax import lax
from jax.experimental import pallas as pl
from jax.experimental.pallas import tpu as pltpu
from jax.experimental.pallas import tpu_sc as plsc

F32 = jnp.float32
BF16 = jnp.bfloat16

D_MODEL = 1024
D_SGU = 512
SGU_HEADS = 8
SGU_HEAD_DIM = 64
SGU_CHUNK = 128
D_GLA = 512
GLA_HEADS = 4
GLA_DK = 256
GLA_HEAD_K = 64
GLA_HEAD_V = 128
GLA_GATE_RANK = 16
GLA_TAU = 16.0
GLA_CHUNK = 64
N_EXPERTS = 32
TOP_K = 4
SWIGLU_LIMIT = 7.0
SWIGLU_ALPHA = 1.702
EPS = 1e-6

LANES = 128
D_IN_MAIN = 2 * D_SGU + 2 * GLA_DK + 2 * D_GLA
D_IN_PAD = D_IN_MAIN + LANES

MIX_TM = 512
MIX_TILE = 256
MIX_SKEW = 1
MIX_SUB = 256
MOE_BM = 512
MOE_QUARTERS = 4
CMB_TM = 1024
N_GROUPS = 2
SC_CORES = 2
SC_SUBCORES = 16
SC_WORKERS = SC_CORES * SC_SUBCORES
SC_CH = 64
NEG_BIG = -1e30

VMEM_LIMIT = 56 * 1024 * 1024


def _ada_kernel(c_ref, w_ref, b_ref, o_ref):
    c = c_ref[...]
    sc = c * jax.nn.sigmoid(c)
    o_ref[...] = jnp.dot(sc.astype(BF16), w_ref[...].astype(BF16), preferred_element_type=F32) + b_ref[...]


def _ada(c_pad, w_ada, b_ada):
    rows, d = c_pad.shape
    n = w_ada.shape[1]
    tn = 1536
    return pl.pallas_call(
        _ada_kernel,
        grid=(n // tn,),
        in_specs=[pl.BlockSpec((rows, d), lambda j: (0, 0)),
                  pl.BlockSpec((d, tn), lambda j: (0, j)),
                  pl.BlockSpec((1, tn), lambda j: (0, j))],
        out_specs=pl.BlockSpec((rows, tn), lambda j: (0, j)),
        out_shape=jax.ShapeDtypeStruct((rows, n), F32),
        compiler_params=pltpu.CompilerParams(dimension_semantics=("arbitrary",), vmem_limit_bytes=VMEM_LIMIT),
        name="ada",
    )(c_pad, w_ada, b_ada)


def _rms(x):
    return x * lax.rsqrt(jnp.mean(x * x, axis=-1, keepdims=True) + EPS)


def _gelu_tanh(x):
    return 0.5 * x * (1.0 + jnp.tanh(0.7978845608028654 * (x + 0.044715 * (x * x * x))))


def _log_sigmoid(z):
    return jnp.minimum(z, 0.0) - jnp.log(1.0 + jnp.exp(-jnp.abs(z)))


def _dot(a, b):
    return jnp.dot(a, b, preferred_element_type=F32)


def _dot_nt(a, b):
    return lax.dot_general(a, b, (((1,), (1,)), ((), ())), preferred_element_type=F32)


def _dot_tn(a, b):
    return lax.dot_general(a, b, (((0,), (0,)), ((), ())), preferred_element_type=F32)


def _pack_rows(x):
    n = x.shape[1] // 2
    lo = lax.bitcast_convert_type(x[:, :n].astype(BF16).astype(F32), jnp.uint32)
    hi = lax.bitcast_convert_type(x[:, n:].astype(BF16).astype(F32), jnp.uint32)
    return (lo >> 16) | (hi & jnp.uint32(0xFFFF0000))


def _unpack_rows(p):
    lo = lax.bitcast_convert_type(p << 16, F32)
    hi = lax.bitcast_convert_type(p & jnp.uint32(0xFFFF0000), F32)
    return lo, hi


def _split_dot(l_bf, a):
    hi = a.astype(BF16)
    lo = (a - hi.astype(F32)).astype(BF16)
    return _dot(l_bf, hi) + _dot(l_bf, lo)


def _mixer_kernel(x_ref, mods_ref, g_pre_ref, w_in_ref, gmat_ref, g_sgu_ref, ws_ref, bs_ref, wgk_ref, bgk_ref,
                  g_gla_ref, w_out_ref, g_post_ref, g_ffn_ref, w_r_ref, b_r_ref,
                  h1_ref, n2_ref, route_ref, probs_ref, cnt_ref,
                  st_ref, carry_ref):
    b = pl.program_id(0)
    j = pl.program_id(1)

    @pl.when(j == 0)
    def _():
        st_ref[...] = jnp.zeros_like(st_ref)

    @pl.when((b == 0) & (j == 0))
    def _():
        carry_ref[...] = jnp.zeros_like(carry_ref)

    carry_box = [carry_ref[...]]
    tiles = [_mixer_tile(slice(s * MIX_TILE, (s + 1) * MIX_TILE), carry_box,
                         x_ref, mods_ref, g_pre_ref, w_in_ref, gmat_ref, g_sgu_ref, ws_ref, bs_ref, wgk_ref, bgk_ref,
                         g_gla_ref, w_out_ref, g_post_ref, g_ffn_ref, w_r_ref, b_r_ref,
                         h1_ref, n2_ref, route_ref, probs_ref, st_ref)
             for s in range(x_ref.shape[1] // MIX_TILE)]
    alive = [True] * len(tiles)
    t = 0
    while any(alive):
        for s, tile in enumerate(tiles):
            if alive[s] and t >= s * MIX_SKEW:
                alive[s] = next(tile, None) is not None
        t += 1
    carry_ref[...] = carry_box[0]
    cnt_ref[...] = jnp.broadcast_to(carry_box[0], cnt_ref.shape)


def _mixer_tile(rows, carry_box, x_ref, mods_ref, g_pre_ref, w_in_ref, gmat_ref, g_sgu_ref, ws_ref, bs_ref, wgk_ref,
                bgk_ref, g_gla_ref, w_out_ref, g_post_ref, g_ffn_ref, w_r_ref, b_r_ref,
                h1_ref, n2_ref, route_ref, probs_ref, st_ref):
    tm = rows.stop - rows.start
    mods = mods_ref[0]
    shift1, scale1, gate1 = mods[0:1], mods[1:2], mods[2:3]
    shift2, scale2 = mods[3:4], mods[4:5]

    x = x_ref[0, rows, :]
    n = _rms(x) * (g_pre_ref[...] * (1.0 + scale1)) + shift1
    nb = n.astype(BF16)
    yield True

    pu = _dot(nb, w_in_ref[:, 0:D_SGU])
    pv = _dot(nb, w_in_ref[:, D_SGU:2 * D_SGU])
    yield True

    u = _gelu_tanh(pu)
    v = _gelu_tanh(pv)
    msv = _dot((v * v).astype(BF16), gmat_ref[...])
    vh = v * lax.rsqrt(msv + EPS) * g_sgu_ref[...]
    yield True

    qk = _dot(nb, w_in_ref[:, 2 * D_SGU:2 * D_SGU + 2 * GLA_DK])
    vv = _dot(nb, w_in_ref[:, 2 * D_SGU + 2 * GLA_DK:2 * D_SGU + 2 * GLA_DK + D_GLA]).astype(BF16)
    r = _dot(nb, w_in_ref[:, D_IN_MAIN - D_GLA:D_IN_MAIN])
    g_low = _dot(nb, w_in_ref[:, D_IN_MAIN:D_IN_PAD])
    yield True

    n_sc = tm // SGU_CHUNK
    lane_c = lax.broadcasted_iota(jnp.int32, (SGU_CHUNK, LANES), 1)
    low_half = lane_c < SGU_HEAD_DIM
    w_row = lax.broadcasted_iota(jnp.int32, (SGU_CHUNK, 2 * SGU_CHUNK), 0)
    w_col = lax.broadcasted_iota(jnp.int32, (SGU_CHUNK, 2 * SGU_CHUNK), 1) & (SGU_CHUNK - 1)
    a_cols = []
    for p in range(SGU_HEADS // 2):
        wcat = jnp.where(w_row >= w_col, ws_ref[p], 0.0).astype(BF16)
        rhs = []
        for ci in range(n_sc):
            vp = vh[ci * SGU_CHUNK:(ci + 1) * SGU_CHUNK, p * LANES:(p + 1) * LANES]
            rhs.append(jnp.concatenate([jnp.where(low_half, vp, 0.0), jnp.where(low_half, 0.0, vp)],
                                       axis=0).astype(BF16))
        res = _dot(wcat, jnp.concatenate(rhs, axis=1))
        bias = bs_ref[:, p * LANES:(p + 1) * LANES]
        a_cols.append(jnp.concatenate(
            [res[:, ci * LANES:(ci + 1) * LANES] + bias for ci in range(n_sc)], axis=0))
    a_out = u * jnp.concatenate(a_cols, axis=1)

    q = qk[:, :GLA_DK] * (GLA_HEAD_K ** -0.5)
    k = qk[:, GLA_DK:]
    z = _dot(g_low.astype(BF16), wgk_ref[...]) + bgk_ref[...]
    log_a = _log_sigmoid(z) * (1.0 / GLA_TAU)

    sub = min(tm, MIX_SUB)
    subs = [slice(s * sub, (s + 1) * sub) for s in range(tm // sub)]
    t_row = lax.broadcasted_iota(jnp.int32, (sub, sub), 0)
    t_col = lax.broadcasted_iota(jnp.int32, (sub, sub), 1)
    same_chunk = (t_row >> 6) == (t_col >> 6)
    l_cum = jnp.where(same_chunk & (t_row >= t_col), 1.0, 0.0).astype(BF16)
    l_all = jnp.where(same_chunk, 1.0, 0.0).astype(BF16)
    bcum = jnp.concatenate([_split_dot(l_cum, log_a[s]) for s in subs], axis=0)
    blast = jnp.concatenate([_split_dot(l_all, log_a[s]) for s in subs], axis=0)
    q_s = q * jnp.exp(bcum)
    k_s = (k * jnp.exp(-bcum)).astype(BF16)
    k_dec = (k * jnp.exp(blast - bcum)).astype(BF16)
    decay = jnp.exp(blast)
    yield True

    lane_g = lax.broadcasted_iota(jnp.int32, (GLA_CHUNK, LANES), 1)
    causal = (lax.broadcasted_iota(jnp.int32, (GLA_CHUNK, GLA_CHUNK), 0)
              >= lax.broadcasted_iota(jnp.int32, (GLA_CHUNK, GLA_CHUNK), 1))
    g_gla = g_gla_ref[...]
    o_rows = []
    for ci in range(tm // GLA_CHUNK):
        rs = slice(ci * GLA_CHUNK, (ci + 1) * GLA_CHUNK)
        o_heads = []
        for h in range(GLA_HEADS):
            pl_ = slice((h // 2) * LANES, (h // 2 + 1) * LANES)
            own = (lane_g >= GLA_HEAD_K) if (h % 2) else (lane_g < GLA_HEAD_K)
            qm = jnp.where(own, q_s[rs, pl_], 0.0).astype(BF16)
            v_h = vv[rs, h * GLA_HEAD_V:(h + 1) * GLA_HEAD_V]
            scores = jnp.where(causal, _dot_nt(qm, k_s[rs, pl_]), 0.0)
            intra = _dot(scores.astype(BF16), v_h)
            st = st_ref[h]
            inter = _dot_nt(qm, st.astype(BF16))
            st_ref[h] = st * decay[ci * GLA_CHUNK:ci * GLA_CHUNK + 1, pl_] + _dot_tn(v_h, k_dec[rs, pl_])
            o_heads.append(_rms(intra + inter) * g_gla)
        o_rows.append(jnp.concatenate(o_heads, axis=1))
    o = jnp.concatenate(o_rows, axis=0) * (r * jax.nn.sigmoid(r))
    mix_in = jnp.concatenate([a_out, o], axis=1).astype(BF16)
    yield True

    mix = _dot(mix_in, w_out_ref[...])
    yield True

    h1 = x + gate1 * (_rms(mix) * g_post_ref[...])
    h1_ref[0, rows, :] = h1
    n2 = _rms(h1) * (g_ffn_ref[...] * (1.0 + scale2)) + shift2
    n2_ref[rows, :] = _pack_rows(n2)
    n2b = n2.astype(BF16)
    yield True

    logits = _dot(n2b, w_r_ref[...]) + b_r_ref[...]
    lane = lax.broadcasted_iota(jnp.int32, (tm, LANES), 1)
    lane_f = lane.astype(F32)
    vals = logits
    sels, tops, idxs = [], [], []
    for _ in range(TOP_K):
        m = jnp.max(vals, axis=-1, keepdims=True)
        idx = jnp.min(jnp.where(vals == m, lane_f, float(LANES)), axis=-1, keepdims=True)
        sel = lane_f == idx
        sels.append(sel)
        tops.append(m)
        idxs.append(idx)
        vals = jnp.where(sel, -jnp.inf, vals)
    es = [jnp.exp(t - tops[0]) for t in tops]
    inv = 1.0 / (es[0] + es[1] + es[2] + es[3])

    multi = jnp.zeros((tm, LANES), F32)
    for sel in sels:
        multi = multi + jnp.where(sel, 1.0, 0.0)
    l_strict = jnp.where(t_row > t_col, 1.0, 0.0).astype(BF16)
    carry = carry_box[0]
    before = []
    for s in subs:
        before.append(_dot(l_strict, multi[s].astype(BF16)) + carry)
        carry = carry + jnp.sum(multi[s], axis=0, keepdims=True)
    before = jnp.concatenate(before, axis=0)
    carry_box[0] = carry

    route = jnp.zeros((tm, LANES), F32)
    probs = jnp.zeros((tm, LANES), F32)
    for kk in range(TOP_K):
        rank_kk = jnp.sum(jnp.where(sels[kk], before, 0.0), axis=-1, keepdims=True)
        route = jnp.where(lane == kk, idxs[kk], route)
        route = jnp.where(lane == TOP_K + kk, rank_kk, route)
        probs = jnp.where(lane == kk, es[kk] * inv, probs)
    route_ref[:, rows] = route.T[0:2 * TOP_K, :].astype(jnp.int32)
    probs_ref[rows, :] = probs


def _const_spec(shape):
    return pl.BlockSpec(shape, lambda b, j: (0,) * len(shape))


def _mixer(b0, B, x, mods, g_pre, w_in_p, gmat, g_sgu, ws_cat, bs_exp, wgk_p, bgk, g_gla, w_out_b, g_post, g_ffn, w_r_p,
           b_r_p):
    _, S, D = x.shape
    T = B * S
    tm = MIX_TM
    nj = S // tm
    return pl.pallas_call(
        _mixer_kernel,
        grid=(B, nj),
        in_specs=[pl.BlockSpec((1, tm, D), lambda b, j: (b + b0, j, 0)),
                  pl.BlockSpec((1, 6, D), lambda b, j: (b + b0, 0, 0)),
                  _const_spec((1, D)), _const_spec(w_in_p.shape), _const_spec(gmat.shape), _const_spec((1, D_SGU)),
                  _const_spec(ws_cat.shape), _const_spec(bs_exp.shape), _const_spec(wgk_p.shape),
                  _const_spec((1, GLA_DK)), _const_spec((1, GLA_HEAD_V)), _const_spec(w_out_b.shape),
                  _const_spec((1, D)), _const_spec((1, D)), _const_spec(w_r_p.shape), _const_spec((1, LANES))],
        out_specs=[pl.BlockSpec((1, tm, D), lambda b, j: (b, j, 0)),
                   pl.BlockSpec((tm, D // 2), lambda b, j: (b * nj + j, 0)),
                   pl.BlockSpec((2 * TOP_K, tm), lambda b, j: (0, b * nj + j)),
                   pl.BlockSpec((tm, LANES), lambda b, j: (b * nj + j, 0)),
                   pl.BlockSpec((8, LANES), lambda b, j: (0, 0))],
        out_shape=[jax.ShapeDtypeStruct((B, S, D), F32),
                   jax.ShapeDtypeStruct((T, D // 2), jnp.uint32),
                   jax.ShapeDtypeStruct((2 * TOP_K, T), jnp.int32),
                   jax.ShapeDtypeStruct((T, LANES), F32),
                   jax.ShapeDtypeStruct((8, LANES), F32)],
        scratch_shapes=[pltpu.VMEM((GLA_HEADS, GLA_HEAD_V, LANES), F32),
                        pltpu.VMEM((1, LANES), F32)],
        compiler_params=pltpu.CompilerParams(dimension_semantics=("arbitrary", "arbitrary"),
                                             vmem_limit_bytes=VMEM_LIMIT),
        name="mixer",
    )(x, mods, g_pre, w_in_p, gmat, g_sgu, ws_cat, bs_exp, wgk_p, bgk, g_gla, w_out_b, g_post, g_ffn, w_r_p, b_r_p)


def _sc_dispatch(rows, dest_kt, n_out):
    T, D = rows.shape
    top_k = dest_kt.shape[0]
    ch = SC_CH
    cpw = T // ch // SC_WORKERS
    mesh = plsc.VectorSubcoreMesh(core_axis_name="c", subcore_axis_name="s")

    @functools.partial(
        pl.kernel, mesh=mesh,
        out_type=jax.ShapeDtypeStruct((n_out, D), rows.dtype),
        scratch_types=[pltpu.VMEM((top_k, ch), jnp.int32), pltpu.VMEM((ch, D), rows.dtype)],
    )
    def k(rows_hbm, dest_hbm, out_hbm, idx_v, rows_v):
        wid = lax.axis_index("s") * SC_CORES + lax.axis_index("c")

        @pl.loop(0, cpw)
        def _(i):
            t0 = (wid * cpw + i) * ch
            for kk in range(top_k):
                pltpu.sync_copy(dest_hbm.at[kk, pl.ds(t0, ch)], idx_v.at[kk])
            pltpu.sync_copy(rows_hbm.at[pl.ds(t0, ch)], rows_v)
            for kk in range(top_k):
                pltpu.sync_copy(rows_v, out_hbm.at[idx_v.at[kk]])

    return k(rows, dest_kt)


def _sc_gather(src, dest_kt):
    _, D = src.shape
    top_k, T = dest_kt.shape
    ch = SC_CH
    cpk = T // ch
    cpw = top_k * cpk // SC_WORKERS
    mesh = plsc.VectorSubcoreMesh(core_axis_name="c", subcore_axis_name="s")

    assert cpw % 2 == 0
    n_pairs = cpw // 2

    @functools.partial(
        pl.kernel, mesh=mesh,
        out_type=jax.ShapeDtypeStruct((top_k * T, D), src.dtype),
        scratch_types=[pltpu.VMEM((2, ch), jnp.int32), pltpu.VMEM((2, ch, D), src.dtype),
                       pltpu.SemaphoreType.DMA, pltpu.SemaphoreType.DMA,
                       pltpu.SemaphoreType.DMA, pltpu.SemaphoreType.DMA],
    )
    def k(src_hbm, idx_hbm, out_hbm, idx_v, rows_v, gsem0, gsem1, wsem0, wsem1):
        wid = lax.axis_index("s") * SC_CORES + lax.axis_index("c")
        gsem = (gsem0, gsem1)
        wsem = (wsem0, wsem1)

        def gather_copy(b):
            return pltpu.make_async_copy(src_hbm.at[idx_v.at[b]], rows_v.at[b], gsem[b])

        def write_copy(c, b):
            return pltpu.make_async_copy(rows_v.at[b], out_hbm.at[pl.ds(c * ch, ch)], wsem[b])

        def start_gather(c, b):
            kk = c // cpk
            t0 = (c - kk * cpk) * ch
            pltpu.sync_copy(idx_hbm.at[kk, pl.ds(t0, ch)], idx_v.at[b])
            gather_copy(b).start()

        c_first = wid * cpw
        start_gather(c_first, 0)

        @pl.loop(0, n_pairs)
        def _(j):
            c0 = c_first + 2 * j
            c1 = c0 + 1

            @pl.when(j > 0)
            def _():
                write_copy(c0 - 1, 1).wait()

            start_gather(c1, 1)
            gather_copy(0).wait()
            write_copy(c0, 0).start()
            write_copy(c0, 0).wait()

            @pl.when(j < n_pairs - 1)
            def _():
                start_gather(c0 + 2, 0)

            gather_copy(1).wait()
            write_copy(c1, 1).start()

        write_copy(c_first + cpw - 1, 1).wait()

    return k(src, dest_kt)


def _expert_kernel(be_ref, first_ref, last_ref, nxt_ref, slot_ref, quarters_ref, nu_ref,
                   x_ref, wg_hbm, bg_ref, wu_hbm, bu_ref, wd_hbm, bd_ref, o_ref,
                   stage_ref, wbf_ref, sem):
    i = pl.program_id(0)
    active = i < nu_ref[0]
    has_next = nxt_ref[i] >= 0

    def weight_copies(e):
        return [pltpu.make_async_copy(w.at[e], stage_ref.at[m], sem.at[m])
                for m, w in enumerate((wg_hbm, wu_hbm, wd_hbm))]

    def cast_stage_into(slot):
        for m in range(3):
            wbf_ref[slot, m] = stage_ref[m].astype(BF16)

    @pl.when(i == 0)
    def _():
        for cp in weight_copies(be_ref[0]):
            cp.start()
        for cp in weight_copies(be_ref[0]):
            cp.wait()
        cast_stage_into(0)

    @pl.when(active & (first_ref[i] == 1) & has_next)
    def _():
        for cp in weight_copies(nxt_ref[i]):
            cp.start()

    def ffn(rows):
        slot = slot_ref[i]
        x_lo, x_hi = _unpack_rows(x_ref[rows, :])
        xb = jnp.concatenate([x_lo, x_hi], axis=1).astype(BF16)
        g = jnp.minimum(_dot(xb, wbf_ref[slot, 0]) + bg_ref[0], SWIGLU_LIMIT)
        u = jnp.clip(_dot(xb, wbf_ref[slot, 1]) + bu_ref[0], -SWIGLU_LIMIT, SWIGLU_LIMIT)
        hdn = (u + 1.0) * (g * jax.nn.sigmoid(SWIGLU_ALPHA * g))
        o_ref[rows, :] = _pack_rows(_dot(hdn.astype(BF16), wbf_ref[slot, 2]) + bd_ref[0])

    bm = x_ref.shape[0]
    for nq in range(1, MOE_QUARTERS + 1):
        @pl.when(active & (quarters_ref[i] == nq))
        def _(nq=nq):
            ffn(slice(0, nq * bm // MOE_QUARTERS))

    @pl.when(active & (last_ref[i] == 1) & has_next)
    def _():
        for cp in weight_copies(nxt_ref[i]):
            cp.wait()
        cast_stage_into(1 - slot_ref[i])


def _expert_schedule(counts, padded, n_blocks, bm):
    pend = jnp.cumsum(padded)
    ids = jnp.arange(N_EXPERTS, dtype=jnp.int32)
    idx = jnp.arange(n_blocks, dtype=jnp.int32)
    n_used = (pend[-1] // bm).astype(jnp.int32)
    block_e = jnp.minimum(jnp.sum((pend[None, :] <= (idx * bm)[:, None]).astype(jnp.int32), axis=1), N_EXPERTS - 1)
    onehot = block_e[:, None] == ids[None, :]

    def per_block(table):
        return jnp.sum(jnp.where(onehot, table[None, :], 0), axis=1)

    start_blk = (pend - padded) // bm
    first = (idx == per_block(start_blk)) & (idx < n_used)
    last = (idx == per_block(pend // bm) - 1) & (idx < n_used)
    later = (ids[None, :] > ids[:, None]) & (padded[None, :] > 0)
    nxt_e = jnp.min(jnp.where(later, ids[None, :], N_EXPERTS), axis=1)
    nxt_e = jnp.where(nxt_e == N_EXPERTS, -1, nxt_e)
    order = jnp.cumsum((padded > 0).astype(jnp.int32)) - 1
    real_rows = per_block(counts) - (idx - per_block(start_blk)) * bm
    qrows = bm // MOE_QUARTERS
    quarters = jnp.clip((real_rows + qrows - 1) // qrows, 1, MOE_QUARTERS)
    as_i32 = lambda a: a.astype(jnp.int32)
    return (block_e, as_i32(first), as_i32(last), per_block(nxt_e), per_block(order) & 1, as_i32(quarters),
            n_used.reshape(1))


def _experts(schedule, xs, w_gate, b_gate, w_up, b_up, w_down, b_down):
    P, half = xs.shape
    bm = MOE_BM
    n_blocks = P // bm
    E, D, DE = w_gate.shape

    def row_map(i, be, first, last, nxt, slot, quarters, nu):
        return (jnp.minimum(i, nu[0] - 1), 0)

    def b_map(i, be, first, last, nxt, slot, quarters, nu):
        return (be[jnp.minimum(i, nu[0] - 1)], 0, 0)

    hbm = pl.BlockSpec(memory_space=pl.ANY)
    grid_spec = pltpu.PrefetchScalarGridSpec(
        num_scalar_prefetch=len(schedule),
        grid=(n_blocks,),
        in_specs=[pl.BlockSpec((bm, half), row_map),
                  hbm, pl.BlockSpec((1, 1, DE), b_map),
                  hbm, pl.BlockSpec((1, 1, DE), b_map),
                  hbm, pl.BlockSpec((1, 1, D), b_map)],
        out_specs=pl.BlockSpec((bm, half), row_map),
        scratch_shapes=[pltpu.VMEM((3, D, DE), F32),
                        pltpu.VMEM((2, 3, D, DE), BF16),
                        pltpu.SemaphoreType.DMA((3,))],
    )
    return pl.pallas_call(
        _expert_kernel,
        grid_spec=grid_spec,
        out_shape=jax.ShapeDtypeStruct((P, half), jnp.uint32),
        compiler_params=pltpu.CompilerParams(dimension_semantics=("arbitrary",), vmem_limit_bytes=VMEM_LIMIT),
        name="experts",
    )(*schedule, xs, w_gate, b_gate.reshape(E, 1, DE), w_up, b_up.reshape(E, 1, DE),
      w_down, b_down.reshape(E, 1, D))


def _combine_kernel(h1_ref, og_ref, probs_ref, mods_ref, g_ref, *rest):
    o_ref = rest[-1]
    tm = h1_ref.shape[1]
    probs = probs_ref[...]
    lane = lax.broadcasted_iota(jnp.int32, (tm, LANES), 1)
    half = og_ref.shape[2]
    f_lo = jnp.zeros((tm, half), F32)
    f_hi = jnp.zeros((tm, half), F32)
    for kk in range(TOP_K):
        pk = jnp.sum(jnp.where(lane == kk, probs, 0.0), axis=-1, keepdims=True)
        o_lo, o_hi = _unpack_rows(og_ref[kk])
        f_lo = f_lo + pk * o_lo
        f_hi = f_hi + pk * o_hi
    f = jnp.concatenate([f_lo, f_hi], axis=1)
    gate2 = mods_ref[0][5:6]
    o_ref[0] = h1_ref[0] + gate2 * (_rms(f) * g_ref[...])


def _combine(b0, b_total, h1, og, probs, mods, g_post_ffn, out_so_far):
    B, S, D = h1.shape
    tm = CMB_TM
    nj = S // tm
    in_specs = [pl.BlockSpec((1, tm, D), lambda b, j: (b, j, 0)),
                pl.BlockSpec((TOP_K, tm, D // 2), lambda b, j: (0, b * nj + j, 0)),
                pl.BlockSpec((tm, LANES), lambda b, j: (b * nj + j, 0)),
                pl.BlockSpec((1, 6, D), lambda b, j: (b + b0, 0, 0)),
                pl.BlockSpec((1, D), lambda b, j: (0, 0))]
    args = [h1, og, probs, mods, g_post_ffn]
    aliases = {}
    if out_so_far is not None:
        in_specs.append(pl.BlockSpec(memory_space=pl.ANY))
        args.append(out_so_far)
        aliases = {len(args) - 1: 0}
    return pl.pallas_call(
        _combine_kernel,
        grid=(B, nj),
        in_specs=in_specs,
        out_specs=pl.BlockSpec((1, tm, D), lambda b, j: (b + b0, j, 0)),
        out_shape=jax.ShapeDtypeStruct((b_total, S, D), F32),
        input_output_aliases=aliases,
        compiler_params=pltpu.CompilerParams(dimension_semantics=("arbitrary", "arbitrary"),
                                             vmem_limit_bytes=VMEM_LIMIT),
        name="combine",
    )(*args)


def _layer(h, c_pad, w_ada, b_ada, g_pre_mix, w_in, g_sgu_v, w_s, b_s, w_gk2, b_gk, g_gla_out, w_out, g_post_mix,
           g_pre_ffn, w_router, b_router, w_gate, b_gate, w_up, b_up, w_down, b_down, g_post_ffn):
    B, S, D = h.shape

    mods = _ada(c_pad, w_ada, b_ada.reshape(1, -1))[:B].reshape(B, 6, D)

    w_in_p = jnp.pad(w_in, ((0, 0), (0, D_IN_PAD - w_in.shape[1]))).astype(BF16)
    head_of = jnp.arange(D_SGU) // SGU_HEAD_DIM
    gmat = jnp.where(head_of[:, None] == head_of[None, :], 1.0 / SGU_HEAD_DIM, 0.0).astype(BF16)
    ws_cat = w_s.reshape(SGU_HEADS // 2, 2, SGU_CHUNK, SGU_CHUNK).transpose(0, 2, 1, 3).reshape(
        SGU_HEADS // 2, SGU_CHUNK, 2 * SGU_CHUNK)
    bs_exp = jnp.repeat(b_s.T, SGU_HEAD_DIM, axis=1)
    wgk_p = jnp.pad(w_gk2, ((0, LANES - GLA_GATE_RANK), (0, 0))).astype(BF16)
    w_r_p = jnp.pad(w_router, ((0, 0), (0, LANES - N_EXPERTS))).astype(BF16)
    b_r_p = jnp.concatenate([b_router, jnp.full((LANES - N_EXPERTS,), NEG_BIG, F32)]).reshape(1, LANES)

    w_out_b = w_out.astype(BF16)

    n_groups = N_GROUPS if B % N_GROUPS == 0 else 1
    bg = B // n_groups
    tg = bg * S
    result = None
    for gi in range(n_groups):
        b0 = gi * bg
        h1, n2, route, probs, cnt = _mixer(
            b0, bg, h, mods, g_pre_mix.reshape(1, D), w_in_p, gmat, g_sgu_v.reshape(1, D_SGU), ws_cat, bs_exp, wgk_p,
            b_gk.reshape(1, GLA_DK), g_gla_out.reshape(1, GLA_HEAD_V), w_out_b, g_post_mix.reshape(1, D),
            g_pre_ffn.reshape(1, D), w_r_p, b_r_p)

        bm = MOE_BM
        counts = cnt[0, :N_EXPERTS].astype(jnp.int32)
        padded = ((counts + bm - 1) // bm) * bm
        pend = jnp.cumsum(padded)
        pstart = pend - padded
        e_kt, rank_kt = route[:TOP_K], route[TOP_K:]
        onehot = e_kt[:, :, None] == jnp.arange(N_EXPERTS, dtype=jnp.int32)[None, None, :]
        dest_kt = rank_kt + jnp.sum(jnp.where(onehot, pstart[None, None, :], 0), axis=-1)
        n_blocks = tg * TOP_K // bm + N_EXPERTS
        schedule = _expert_schedule(counts, padded, n_blocks, bm)

        xs = _sc_dispatch(n2, dest_kt, n_blocks * bm)
        out = _experts(schedule, xs, w_gate, b_gate, w_up, b_up, w_down, b_down)
        og = _sc_gather(out, dest_kt).reshape(TOP_K, tg, D // 2)
        result = _combine(b0, B, h1, og, probs, mods, g_post_ffn.reshape(1, D), result)
    return result


def kernel(x, c, w_ada, b_ada, g_pre_mix, w_in, g_sgu_v, w_s, b_s, w_gk2, b_gk, g_gla_out, w_out, g_post_mix, g_pre_ffn, w_router, b_router, w_gate, b_gate, w_up, b_up, w_down, b_down, g_post_ffn):
    B = x.shape[0]
    c_pad = jnp.pad(c, ((0, -B % 8), (0, 0)))
    h = x
    for l in range(w_ada.shape[0]):
        h = _layer(h, c_pad, w_ada[l], b_ada[l], g_pre_mix[l], w_in[l], g_sgu_v[l], w_s[l], b_s[l], w_gk2[l], b_gk[l],
                   g_gla_out[l], w_out[l], g_post_mix[l], g_pre_ffn[l], w_router[l], b_router[l], w_gate[l], b_gate[l],
                   w_up[l], b_up[l], w_down[l], b_down[l], g_post_ffn[l])
    return h
```

```python
import functools

import jax
import jax.numpy as jnp
from jax import lax
from jax.experimental import pallas as pl
from jax.experimental.pallas import tpu as pltpu
from jax.experimental.pallas import tpu_sc as plsc

F32 = jnp.float32
BF16 = jnp.bfloat16

D_MODEL = 1024
D_SGU = 512
SGU_HEADS = 8
SGU_HEAD_DIM = 64
SGU_CHUNK = 128
D_GLA = 512
GLA_HEADS = 4
GLA_DK = 256
GLA_HEAD_K = 64
GLA_HEAD_V = 128
GLA_GATE_RANK = 16
GLA_TAU = 16.0
GLA_CHUNK = 64
N_EXPERTS = 32
TOP_K = 4
SWIGLU_LIMIT = 7.0
SWIGLU_ALPHA = 1.702
EPS = 1e-6

LANES = 128
D_IN_MAIN = 2 * D_SGU + 2 * GLA_DK + 2 * D_GLA
D_IN_PAD = D_IN_MAIN + LANES

MIX_TM = 512
MIX_TILE = 256
MIX_SKEW = 1
MIX_SUB = 256
MOE_BM = 512
MOE_QUARTERS = 4
CMB_TM = 1024
LAST_GROUP_FRACTION = 4
SC_CORES = 2
SC_SUBCORES = 16
SC_WORKERS = SC_CORES * SC_SUBCORES
SC_CH = 64
NEG_BIG = -1e30

VMEM_LIMIT = 56 * 1024 * 1024


def _ada_kernel(c_ref, w_ref, b_ref, o_ref):
    c = c_ref[...]
    sc = c * jax.nn.sigmoid(c)
    o_ref[...] = jnp.dot(sc.astype(BF16), w_ref[...].astype(BF16), preferred_element_type=F32) + b_ref[...]


def _ada(c_pad, w_ada, b_ada):
    rows, d = c_pad.shape
    n = w_ada.shape[1]
    tn = 1536
    return pl.pallas_call(
        _ada_kernel,
        grid=(n // tn,),
        in_specs=[pl.BlockSpec((rows, d), lambda j: (0, 0)),
                  pl.BlockSpec((d, tn), lambda j: (0, j)),
                  pl.BlockSpec((1, tn), lambda j: (0, j))],
        out_specs=pl.BlockSpec((rows, tn), lambda j: (0, j)),
        out_shape=jax.ShapeDtypeStruct((rows, n), F32),
        compiler_params=pltpu.CompilerParams(dimension_semantics=("arbitrary",), vmem_limit_bytes=VMEM_LIMIT),
        name="ada",
    )(c_pad, w_ada, b_ada)


def _rms(x):
    return x * lax.rsqrt(jnp.mean(x * x, axis=-1, keepdims=True) + EPS)


def _gelu_tanh(x):
    return 0.5 * x * (1.0 + jnp.tanh(0.7978845608028654 * (x + 0.044715 * (x * x * x))))


def _log_sigmoid(z):
    return jnp.minimum(z, 0.0) - jnp.log(1.0 + jnp.exp(-jnp.abs(z)))


def _dot(a, b):
    return jnp.dot(a, b, preferred_element_type=F32)


def _dot_nt(a, b):
    return lax.dot_general(a, b, (((1,), (1,)), ((), ())), preferred_element_type=F32)


def _dot_tn(a, b):
    return lax.dot_general(a, b, (((0,), (0,)), ((), ())), preferred_element_type=F32)


def _pack_rows(x):
    n = x.shape[1] // 2
    lo = lax.bitcast_convert_type(x[:, :n].astype(BF16).astype(F32), jnp.uint32)
    hi = lax.bitcast_convert_type(x[:, n:].astype(BF16).astype(F32), jnp.uint32)
    return (lo >> 16) | (hi & jnp.uint32(0xFFFF0000))


def _unpack_rows(p):
    lo = lax.bitcast_convert_type(p << 16, F32)
    hi = lax.bitcast_convert_type(p & jnp.uint32(0xFFFF0000), F32)
    return lo, hi


def _split_dot(l_bf, a):
    hi = a.astype(BF16)
    lo = (a - hi.astype(F32)).astype(BF16)
    return _dot(l_bf, hi) + _dot(l_bf, lo)


def _mixer_kernel(x_ref, mods_ref, g_pre_ref, w_in_ref, gmat_ref, g_sgu_ref, ws_ref, bs_ref, wgk_ref, bgk_ref,
                  g_gla_ref, w_out_ref, g_post_ref, g_ffn_ref, w_r_ref, b_r_ref,
                  h1_ref, n2_ref, route_ref, probs_ref, cnt_ref,
                  st_ref, carry_ref):
    b = pl.program_id(0)
    j = pl.program_id(1)

    @pl.when(j == 0)
    def _():
        st_ref[...] = jnp.zeros_like(st_ref)

    @pl.when((b == 0) & (j == 0))
    def _():
        carry_ref[...] = jnp.zeros_like(carry_ref)

    carry_box = [carry_ref[...]]
    tiles = [_mixer_tile(slice(s * MIX_TILE, (s + 1) * MIX_TILE), carry_box,
                         x_ref, mods_ref, g_pre_ref, w_in_ref, gmat_ref, g_sgu_ref, ws_ref, bs_ref, wgk_ref, bgk_ref,
                         g_gla_ref, w_out_ref, g_post_ref, g_ffn_ref, w_r_ref, b_r_ref,
                         h1_ref, n2_ref, route_ref, probs_ref, st_ref)
             for s in range(x_ref.shape[1] // MIX_TILE)]
    alive = [True] * len(tiles)
    t = 0
    while any(alive):
        for s, tile in enumerate(tiles):
            if alive[s] and t >= s * MIX_SKEW:
                alive[s] = next(tile, None) is not None
        t += 1
    carry_ref[...] = carry_box[0]
    cnt_ref[...] = jnp.broadcast_to(carry_box[0], cnt_ref.shape)


def _mixer_tile(rows, carry_box, x_ref, mods_ref, g_pre_ref, w_in_ref, gmat_ref, g_sgu_ref, ws_ref, bs_ref, wgk_ref,
                bgk_ref, g_gla_ref, w_out_ref, g_post_ref, g_ffn_ref, w_r_ref, b_r_ref,
                h1_ref, n2_ref, route_ref, probs_ref, st_ref):
    tm = rows.stop - rows.start
    mods = mods_ref[0]
    shift1, scale1, gate1 = mods[0:1], mods[1:2], mods[2:3]
    shift2, scale2 = mods[3:4], mods[4:5]

    x = x_ref[0, rows, :]
    n = _rms(x) * (g_pre_ref[...] * (1.0 + scale1)) + shift1
    nb = n.astype(BF16)
    yield True

    pu = _dot(nb, w_in_ref[:, 0:D_SGU])
    pv = _dot(nb, w_in_ref[:, D_SGU:2 * D_SGU])
    yield True

    u = _gelu_tanh(pu)
    v = _gelu_tanh(pv)
    msv = _dot((v * v).astype(BF16), gmat_ref[...])
    vh = v * lax.rsqrt(msv + EPS) * g_sgu_ref[...]
    yield True

    qk = _dot(nb, w_in_ref[:, 2 * D_SGU:2 * D_SGU + 2 * GLA_DK])
    vv = _dot(nb, w_in_ref[:, 2 * D_SGU + 2 * GLA_DK:2 * D_SGU + 2 * GLA_DK + D_GLA]).astype(BF16)
    r = _dot(nb, w_in_ref[:, D_IN_MAIN - D_GLA:D_IN_MAIN])
    g_low = _dot(nb, w_in_ref[:, D_IN_MAIN:D_IN_PAD])
    yield True

    n_sc = tm // SGU_CHUNK
    lane_c = lax.broadcasted_iota(jnp.int32, (SGU_CHUNK, LANES), 1)
    low_half = lane_c < SGU_HEAD_DIM
    w_row = lax.broadcasted_iota(jnp.int32, (SGU_CHUNK, 2 * SGU_CHUNK), 0)
    w_col = lax.broadcasted_iota(jnp.int32, (SGU_CHUNK, 2 * SGU_CHUNK), 1) & (SGU_CHUNK - 1)
    a_cols = []
    for p in range(SGU_HEADS // 2):
        wcat = jnp.where(w_row >= w_col, ws_ref[p], 0.0).astype(BF16)
        rhs = []
        for ci in range(n_sc):
            vp = vh[ci * SGU_CHUNK:(ci + 1) * SGU_CHUNK, p * LANES:(p + 1) * LANES]
            rhs.append(jnp.concatenate([jnp.where(low_half, vp, 0.0), jnp.where(low_half, 0.0, vp)],
                                       axis=0).astype(BF16))
        res = _dot(wcat, jnp.concatenate(rhs, axis=1))
        bias = bs_ref[:, p * LANES:(p + 1) * LANES]
        a_cols.append(jnp.concatenate(
            [res[:, ci * LANES:(ci + 1) * LANES] + bias for ci in range(n_sc)], axis=0))
    a_out = u * jnp.concatenate(a_cols, axis=1)

    q = qk[:, :GLA_DK] * (GLA_HEAD_K ** -0.5)
    k = qk[:, GLA_DK:]
    z = _dot(g_low.astype(BF16), wgk_ref[...]) + bgk_ref[...]
    log_a = _log_sigmoid(z) * (1.0 / GLA_TAU)

    sub = min(tm, MIX_SUB)
    subs = [slice(s * sub, (s + 1) * sub) for s in range(tm // sub)]
    t_row = lax.broadcasted_iota(jnp.int32, (sub, sub), 0)
    t_col = lax.broadcasted_iota(jnp.int32, (sub, sub), 1)
    same_chunk = (t_row >> 6) == (t_col >> 6)
    l_cum = jnp.where(same_chunk & (t_row >= t_col), 1.0, 0.0).astype(BF16)
    l_all = jnp.where(same_chunk, 1.0, 0.0).astype(BF16)
    bcum = jnp.concatenate([_split_dot(l_cum, log_a[s]) for s in subs], axis=0)
    blast = jnp.concatenate([_split_dot(l_all, log_a[s]) for s in subs], axis=0)
    q_s = q * jnp.exp(bcum)
    k_s = (k * jnp.exp(-bcum)).astype(BF16)
    k_dec = (k * jnp.exp(blast - bcum)).astype(BF16)
    decay = jnp.exp(blast)
    yield True

    lane_g = lax.broadcasted_iota(jnp.int32, (GLA_CHUNK, LANES), 1)
    causal = (lax.broadcasted_iota(jnp.int32, (GLA_CHUNK, GLA_CHUNK), 0)
              >= lax.broadcasted_iota(jnp.int32, (GLA_CHUNK, GLA_CHUNK), 1))
    g_gla = g_gla_ref[...]
    o_rows = []
    for ci in range(tm // GLA_CHUNK):
        rs = slice(ci * GLA_CHUNK, (ci + 1) * GLA_CHUNK)
        o_heads = []
        for h in range(GLA_HEADS):
            pl_ = slice((h // 2) * LANES, (h // 2 + 1) * LANES)
            own = (lane_g >= GLA_HEAD_K) if (h % 2) else (lane_g < GLA_HEAD_K)
            qm = jnp.where(own, q_s[rs, pl_], 0.0).astype(BF16)
            v_h = vv[rs, h * GLA_HEAD_V:(h + 1) * GLA_HEAD_V]
            scores = jnp.where(causal, _dot_nt(qm, k_s[rs, pl_]), 0.0)
            intra = _dot(scores.astype(BF16), v_h)
            st = st_ref[h]
            inter = _dot_nt(qm, st.astype(BF16))
            st_ref[h] = st * decay[ci * GLA_CHUNK:ci * GLA_CHUNK + 1, pl_] + _dot_tn(v_h, k_dec[rs, pl_])
            o_heads.append(_rms(intra + inter) * g_gla)
        o_rows.append(jnp.concatenate(o_heads, axis=1))
    o = jnp.concatenate(o_rows, axis=0) * (r * jax.nn.sigmoid(r))
    mix_in = jnp.concatenate([a_out, o], axis=1).astype(BF16)
    yield True

    mix = _dot(mix_in, w_out_ref[...])
    yield True

    h1 = x + gate1 * (_rms(mix) * g_post_ref[...])
    h1_ref[0, rows, :] = h1
    n2 = _rms(h1) * (g_ffn_ref[...] * (1.0 + scale2)) + shift2
    n2_ref[rows, :] = _pack_rows(n2)
    n2b = n2.astype(BF16)
    yield True

    logits = _dot(n2b, w_r_ref[...]) + b_r_ref[...]
    lane = lax.broadcasted_iota(jnp.int32, (tm, LANES), 1)
    lane_f = lane.astype(F32)
    vals = logits
    sels, tops, idxs = [], [], []
    for _ in range(TOP_K):
        m = jnp.max(vals, axis=-1, keepdims=True)
        idx = jnp.min(jnp.where(vals == m, lane_f, float(LANES)), axis=-1, keepdims=True)
        sel = lane_f == idx
        sels.append(sel)
        tops.append(m)
        idxs.append(idx)
        vals = jnp.where(sel, -jnp.inf, vals)
    es = [jnp.exp(t - tops[0]) for t in tops]
    inv = 1.0 / (es[0] + es[1] + es[2] + es[3])

    multi = jnp.zeros((tm, LANES), F32)
    for sel in sels:
        multi = multi + jnp.where(sel, 1.0, 0.0)
    l_strict = jnp.where(t_row > t_col, 1.0, 0.0).astype(BF16)
    carry = carry_box[0]
    before = []
    for s in subs:
        before.append(_dot(l_strict, multi[s].astype(BF16)) + carry)
        carry = carry + jnp.sum(multi[s], axis=0, keepdims=True)
    before = jnp.concatenate(before, axis=0)
    carry_box[0] = carry

    route = jnp.zeros((tm, LANES), F32)
    probs = jnp.zeros((tm, LANES), F32)
    for kk in range(TOP_K):
        rank_kk = jnp.sum(jnp.where(sels[kk], before, 0.0), axis=-1, keepdims=True)
        route = jnp.where(lane == kk, idxs[kk], route)
        route = jnp.where(lane == TOP_K + kk, rank_kk, route)
        probs = jnp.where(lane == kk, es[kk] * inv, probs)
    route_ref[:, rows] = route.T[0:2 * TOP_K, :].astype(jnp.int32)
    probs_ref[rows, :] = probs


def _const_spec(shape):
    return pl.BlockSpec(shape, lambda b, j: (0,) * len(shape))


def _mixer(b0, B, x, mods, g_pre, w_in_p, gmat, g_sgu, ws_cat, bs_exp, wgk_p, bgk, g_gla, w_out_b, g_post, g_ffn, w_r_p,
           b_r_p):
    _, S, D = x.shape
    T = B * S
    tm = MIX_TM
    nj = S // tm
    return pl.pallas_call(
        _mixer_kernel,
        grid=(B, nj),
        in_specs=[pl.BlockSpec((1, tm, D), lambda b, j: (b + b0, j, 0)),
                  pl.BlockSpec((1, 6, D), lambda b, j: (b + b0, 0, 0)),
                  _const_spec((1, D)), _const_spec(w_in_p.shape), _const_spec(gmat.shape), _const_spec((1, D_SGU)),
                  _const_spec(ws_cat.shape), _const_spec(bs_exp.shape), _const_spec(wgk_p.shape),
                  _const_spec((1, GLA_DK)), _const_spec((1, GLA_HEAD_V)), _const_spec(w_out_b.shape),
                  _const_spec((1, D)), _const_spec((1, D)), _const_spec(w_r_p.shape), _const_spec((1, LANES))],
        out_specs=[pl.BlockSpec((1, tm, D), lambda b, j: (b, j, 0)),
                   pl.BlockSpec((tm, D // 2), lambda b, j: (b * nj + j, 0)),
                   pl.BlockSpec((2 * TOP_K, tm), lambda b, j: (0, b * nj + j)),
                   pl.BlockSpec((tm, LANES), lambda b, j: (b * nj + j, 0)),
                   pl.BlockSpec((8, LANES), lambda b, j: (0, 0))],
        out_shape=[jax.ShapeDtypeStruct((B, S, D), F32),
                   jax.ShapeDtypeStruct((T, D // 2), jnp.uint32),
                   jax.ShapeDtypeStruct((2 * TOP_K, T), jnp.int32),
                   jax.ShapeDtypeStruct((T, LANES), F32),
                   jax.ShapeDtypeStruct((8, LANES), F32)],
        scratch_shapes=[pltpu.VMEM((GLA_HEADS, GLA_HEAD_V, LANES), F32),
                        pltpu.VMEM((1, LANES), F32)],
        compiler_params=pltpu.CompilerParams(dimension_semantics=("arbitrary", "arbitrary"),
                                             vmem_limit_bytes=VMEM_LIMIT),
        name="mixer",
    )(x, mods, g_pre, w_in_p, gmat, g_sgu, ws_cat, bs_exp, wgk_p, bgk, g_gla, w_out_b, g_post, g_ffn, w_r_p, b_r_p)


def _sc_dispatch(rows, dest_kt, n_out):
    T, D = rows.shape
    top_k = dest_kt.shape[0]
    ch = SC_CH
    cpw = T // ch // SC_WORKERS
    mesh = plsc.VectorSubcoreMesh(core_axis_name="c", subcore_axis_name="s")

    @functools.partial(
        pl.kernel, mesh=mesh,
        out_type=jax.ShapeDtypeStruct((n_out, D), rows.dtype),
        scratch_types=[pltpu.VMEM((top_k, ch), jnp.int32), pltpu.VMEM((ch, D), rows.dtype)],
    )
    def k(rows_hbm, dest_hbm, out_hbm, idx_v, rows_v):
        wid = lax.axis_index("s") * SC_CORES + lax.axis_index("c")

        @pl.loop(0, cpw)
        def _(i):
            t0 = (wid * cpw + i) * ch
            for kk in range(top_k):
                pltpu.sync_copy(dest_hbm.at[kk, pl.ds(t0, ch)], idx_v.at[kk])
            pltpu.sync_copy(rows_hbm.at[pl.ds(t0, ch)], rows_v)
            for kk in range(top_k):
                pltpu.sync_copy(rows_v, out_hbm.at[idx_v.at[kk]])

    return k(rows, dest_kt)


def _sc_gather(src, dest_kt):
    _, D = src.shape
    top_k, T = dest_kt.shape
    ch = SC_CH
    cpk = T // ch
    cpw = top_k * cpk // SC_WORKERS
    mesh = plsc.VectorSubcoreMesh(core_axis_name="c", subcore_axis_name="s")

    assert cpw % 2 == 0
    n_pairs = cpw // 2

    @functools.partial(
        pl.kernel, mesh=mesh,
        out_type=jax.ShapeDtypeStruct((top_k * T, D), src.dtype),
        scratch_types=[pltpu.VMEM((2, ch), jnp.int32), pltpu.VMEM((2, ch, D), src.dtype),
                       pltpu.SemaphoreType.DMA, pltpu.SemaphoreType.DMA,
                       pltpu.SemaphoreType.DMA, pltpu.SemaphoreType.DMA],
    )
    def k(src_hbm, idx_hbm, out_hbm, idx_v, rows_v, gsem0, gsem1, wsem0, wsem1):
        wid = lax.axis_index("s") * SC_CORES + lax.axis_index("c")
        gsem = (gsem0, gsem1)
        wsem = (wsem0, wsem1)

        def gather_copy(b):
            return pltpu.make_async_copy(src_hbm.at[idx_v.at[b]], rows_v.at[b], gsem[b])

        def write_copy(c, b):
            return pltpu.make_async_copy(rows_v.at[b], out_hbm.at[pl.ds(c * ch, ch)], wsem[b])

        def start_gather(c, b):
            kk = c // cpk
            t0 = (c - kk * cpk) * ch
            pltpu.sync_copy(idx_hbm.at[kk, pl.ds(t0, ch)], idx_v.at[b])
            gather_copy(b).start()

        c_first = wid * cpw
        start_gather(c_first, 0)

        @pl.loop(0, n_pairs)
        def _(j):
            c0 = c_first + 2 * j
            c1 = c0 + 1

            @pl.when(j > 0)
            def _():
                write_copy(c0 - 1, 1).wait()

            start_gather(c1, 1)
            gather_copy(0).wait()
            write_copy(c0, 0).start()
            write_copy(c0, 0).wait()

            @pl.when(j < n_pairs - 1)
            def _():
                start_gather(c0 + 2, 0)

            gather_copy(1).wait()
            write_copy(c1, 1).start()

        write_copy(c_first + cpw - 1, 1).wait()

    return k(src, dest_kt)


def _expert_kernel(be_ref, first_ref, last_ref, nxt_ref, slot_ref, quarters_ref, nu_ref,
                   x_ref, wg_hbm, bg_ref, wu_hbm, bu_ref, wd_hbm, bd_ref, o_ref,
                   stage_ref, wbf_ref, sem):
    i = pl.program_id(0)
    active = i < nu_ref[0]
    has_next = nxt_ref[i] >= 0

    def weight_copies(e):
        return [pltpu.make_async_copy(w.at[e], stage_ref.at[m], sem.at[m])
                for m, w in enumerate((wg_hbm, wu_hbm, wd_hbm))]

    def cast_stage_into(slot):
        for m in range(3):
            wbf_ref[slot, m] = stage_ref[m].astype(BF16)

    @pl.when(i == 0)
    def _():
        for cp in weight_copies(be_ref[0]):
            cp.start()
        for cp in weight_copies(be_ref[0]):
            cp.wait()
        cast_stage_into(0)

    @pl.when(active & (first_ref[i] == 1) & has_next)
    def _():
        for cp in weight_copies(nxt_ref[i]):
            cp.start()

    def ffn(rows):
        slot = slot_ref[i]
        x_lo, x_hi = _unpack_rows(x_ref[rows, :])
        xb = jnp.concatenate([x_lo, x_hi], axis=1).astype(BF16)
        g = jnp.minimum(_dot(xb, wbf_ref[slot, 0]) + bg_ref[0], SWIGLU_LIMIT)
        u = jnp.clip(_dot(xb, wbf_ref[slot, 1]) + bu_ref[0], -SWIGLU_LIMIT, SWIGLU_LIMIT)
        hdn = (u + 1.0) * (g * jax.nn.sigmoid(SWIGLU_ALPHA * g))
        o_ref[rows, :] = _pack_rows(_dot(hdn.astype(BF16), wbf_ref[slot, 2]) + bd_ref[0])

    bm = x_ref.shape[0]
    for nq in range(1, MOE_QUARTERS + 1):
        @pl.when(active & (quarters_ref[i] == nq))
        def _(nq=nq):
            ffn(slice(0, nq * bm // MOE_QUARTERS))

    @pl.when(active & (last_ref[i] == 1) & has_next)
    def _():
        for cp in weight_copies(nxt_ref[i]):
            cp.wait()
        cast_stage_into(1 - slot_ref[i])


def _expert_schedule(counts, padded, n_blocks, bm):
    pend = jnp.cumsum(padded)
    ids = jnp.arange(N_EXPERTS, dtype=jnp.int32)
    idx = jnp.arange(n_blocks, dtype=jnp.int32)
    n_used = (pend[-1] // bm).astype(jnp.int32)
    block_e = jnp.minimum(jnp.sum((pend[None, :] <= (idx * bm)[:, None]).astype(jnp.int32), axis=1), N_EXPERTS - 1)
    onehot = block_e[:, None] == ids[None, :]

    def per_block(table):
        return jnp.sum(jnp.where(onehot, table[None, :], 0), axis=1)

    start_blk = (pend - padded) // bm
    first = (idx == per_block(start_blk)) & (idx < n_used)
    last = (idx == per_block(pend // bm) - 1) & (idx < n_used)
    later = (ids[None, :] > ids[:, None]) & (padded[None, :] > 0)
    nxt_e = jnp.min(jnp.where(later, ids[None, :], N_EXPERTS), axis=1)
    nxt_e = jnp.where(nxt_e == N_EXPERTS, -1, nxt_e)
    order = jnp.cumsum((padded > 0).astype(jnp.int32)) - 1
    real_rows = per_block(counts) - (idx - per_block(start_blk)) * bm
    qrows = bm // MOE_QUARTERS
    quarters = jnp.clip((real_rows + qrows - 1) // qrows, 1, MOE_QUARTERS)
    as_i32 = lambda a: a.astype(jnp.int32)
    return (block_e, as_i32(first), as_i32(last), per_block(nxt_e), per_block(order) & 1, as_i32(quarters),
            n_used.reshape(1))


def _experts(schedule, xs, w_gate, b_gate, w_up, b_up, w_down, b_down):
    P, half = xs.shape
    bm = MOE_BM
    n_blocks = P // bm
    E, D, DE = w_gate.shape

    def row_map(i, be, first, last, nxt, slot, quarters, nu):
        return (jnp.minimum(i, nu[0] - 1), 0)

    def b_map(i, be, first, last, nxt, slot, quarters, nu):
        return (be[jnp.minimum(i, nu[0] - 1)], 0, 0)

    hbm = pl.BlockSpec(memory_space=pl.ANY)
    grid_spec = pltpu.PrefetchScalarGridSpec(
        num_scalar_prefetch=len(schedule),
        grid=(n_blocks,),
        in_specs=[pl.BlockSpec((bm, half), row_map),
                  hbm, pl.BlockSpec((1, 1, DE), b_map),
                  hbm, pl.BlockSpec((1, 1, DE), b_map),
                  hbm, pl.BlockSpec((1, 1, D), b_map)],
        out_specs=pl.BlockSpec((bm, half), row_map),
        scratch_shapes=[pltpu.VMEM((3, D, DE), F32),
                        pltpu.VMEM((2, 3, D, DE), BF16),
                        pltpu.SemaphoreType.DMA((3,))],
    )
    return pl.pallas_call(
        _expert_kernel,
        grid_spec=grid_spec,
        out_shape=jax.ShapeDtypeStruct((P, half), jnp.uint32),
        compiler_params=pltpu.CompilerParams(dimension_semantics=("arbitrary",), vmem_limit_bytes=VMEM_LIMIT),
        name="experts",
    )(*schedule, xs, w_gate, b_gate.reshape(E, 1, DE), w_up, b_up.reshape(E, 1, DE),
      w_down, b_down.reshape(E, 1, D))


def _combine_kernel(h1_ref, og_ref, probs_ref, mods_ref, g_ref, *rest):
    o_ref = rest[-1]
    tm = h1_ref.shape[1]
    probs = probs_ref[...]
    lane = lax.broadcasted_iota(jnp.int32, (tm, LANES), 1)
    half = og_ref.shape[2]
    f_lo = jnp.zeros((tm, half), F32)
    f_hi = jnp.zeros((tm, half), F32)
    for kk in range(TOP_K):
        pk = jnp.sum(jnp.where(lane == kk, probs, 0.0), axis=-1, keepdims=True)
        o_lo, o_hi = _unpack_rows(og_ref[kk])
        f_lo = f_lo + pk * o_lo
        f_hi = f_hi + pk * o_hi
    f = jnp.concatenate([f_lo, f_hi], axis=1)
    gate2 = mods_ref[0][5:6]
    o_ref[0] = h1_ref[0] + gate2 * (_rms(f) * g_ref[...])


def _combine(b0, b_total, h1, og, probs, mods, g_post_ffn, out_so_far):
    B, S, D = h1.shape
    tm = CMB_TM
    nj = S // tm
    in_specs = [pl.BlockSpec((1, tm, D), lambda b, j: (b, j, 0)),
                pl.BlockSpec((TOP_K, tm, D // 2), lambda b, j: (0, b * nj + j, 0)),
                pl.BlockSpec((tm, LANES), lambda b, j: (b * nj + j, 0)),
                pl.BlockSpec((1, 6, D), lambda b, j: (b + b0, 0, 0)),
                pl.BlockSpec((1, D), lambda b, j: (0, 0))]
    args = [h1, og, probs, mods, g_post_ffn]
    aliases = {}
    if out_so_far is not None:
        in_specs.append(pl.BlockSpec(memory_space=pl.ANY))
        args.append(out_so_far)
        aliases = {len(args) - 1: 0}
    return pl.pallas_call(
        _combine_kernel,
        grid=(B, nj),
        in_specs=in_specs,
        out_specs=pl.BlockSpec((1, tm, D), lambda b, j: (b + b0, j, 0)),
        out_shape=jax.ShapeDtypeStruct((b_total, S, D), F32),
        input_output_aliases=aliases,
        compiler_params=pltpu.CompilerParams(dimension_semantics=("arbitrary", "arbitrary"),
                                             vmem_limit_bytes=VMEM_LIMIT),
        name="combine",
    )(*args)


def _layer(h, c_pad, w_ada, b_ada, g_pre_mix, w_in, g_sgu_v, w_s, b_s, w_gk2, b_gk, g_gla_out, w_out, g_post_mix,
           g_pre_ffn, w_router, b_router, w_gate, b_gate, w_up, b_up, w_down, b_down, g_post_ffn):
    B, S, D = h.shape

    mods = _ada(c_pad, w_ada, b_ada.reshape(1, -1))[:B].reshape(B, 6, D)

    w_in_p = jnp.pad(w_in, ((0, 0), (0, D_IN_PAD - w_in.shape[1]))).astype(BF16)
    head_of = jnp.arange(D_SGU) // SGU_HEAD_DIM
    gmat = jnp.where(head_of[:, None] == head_of[None, :], 1.0 / SGU_HEAD_DIM, 0.0).astype(BF16)
    ws_cat = w_s.reshape(SGU_HEADS // 2, 2, SGU_CHUNK, SGU_CHUNK).transpose(0, 2, 1, 3).reshape(
        SGU_HEADS // 2, SGU_CHUNK, 2 * SGU_CHUNK)
    bs_exp = jnp.repeat(b_s.T, SGU_HEAD_DIM, axis=1)
    wgk_p = jnp.pad(w_gk2, ((0, LANES - GLA_GATE_RANK), (0, 0))).astype(BF16)
    w_r_p = jnp.pad(w_router, ((0, 0), (0, LANES - N_EXPERTS))).astype(BF16)
    b_r_p = jnp.concatenate([b_router, jnp.full((LANES - N_EXPERTS,), NEG_BIG, F32)]).reshape(1, LANES)

    w_out_b = w_out.astype(BF16)

    last = max(1, B // LAST_GROUP_FRACTION)
    group_sizes = [B - last, last] if B > 1 else [B]
    result = None
    b0 = 0
    for bg in group_sizes:
        tg = bg * S
        h1, n2, route, probs, cnt = _mixer(
            b0, bg, h, mods, g_pre_mix.reshape(1, D), w_in_p, gmat, g_sgu_v.reshape(1, D_SGU), ws_cat, bs_exp, wgk_p,
            b_gk.reshape(1, GLA_DK), g_gla_out.reshape(1, GLA_HEAD_V), w_out_b, g_post_mix.reshape(1, D),
            g_pre_ffn.reshape(1, D), w_r_p, b_r_p)

        bm = MOE_BM
        counts = cnt[0, :N_EXPERTS].astype(jnp.int32)
        padded = ((counts + bm - 1) // bm) * bm
        pend = jnp.cumsum(padded)
        pstart = pend - padded
        e_kt, rank_kt = route[:TOP_K], route[TOP_K:]
        onehot = e_kt[:, :, None] == jnp.arange(N_EXPERTS, dtype=jnp.int32)[None, None, :]
        dest_kt = rank_kt + jnp.sum(jnp.where(onehot, pstart[None, None, :], 0), axis=-1)
        n_blocks = tg * TOP_K // bm + N_EXPERTS
        schedule = _expert_schedule(counts, padded, n_blocks, bm)

        xs = _sc_dispatch(n2, dest_kt, n_blocks * bm)
        out = _experts(schedule, xs, w_gate, b_gate, w_up, b_up, w_down, b_down)
        og = _sc_gather(out, dest_kt).reshape(TOP_K, tg, D // 2)
        result = _combine(b0, B, h1, og, probs, mods, g_post_ffn.reshape(1, D), result)
        b0 += bg
    return result


def kernel(x, c, w_ada, b_ada, g_pre_mix, w_in, g_sgu_v, w_s, b_s, w_gk2, b_gk, g_gla_out, w_out, g_post_mix, g_pre_ffn, w_router, b_router, w_gate, b_gate, w_up, b_up, w_down, b_down, g_post_ffn):
    B = x.shape[0]
    c_pad = jnp.pad(c, ((0, -B % 8), (0, 0)))
    h = x
    for l in range(w_ada.shape[0]):
        h = _layer(h, c_pad, w_ada[l], b_ada[l], g_pre_mix[l], w_in[l], g_sgu_v[l], w_s[l], b_s[l], w_gk2[l], b_gk[l],
                   g_gla_out[l], w_out[l], g_post_mix[l], g_pre_ffn[l], w_router[l], b_router[l], w_gate[l], b_gate[l],
                   w_up[l], b_up[l], w_down[l], b_down[l], g_post_ffn[l])
    return h
```

```python
import functools

import jax
import jax.numpy as jnp
from jax import lax
from jax.experimental import pallas as pl
from jax.experimental.pallas import tpu as pltpu
from jax.experimental.pallas import tpu_sc as plsc

F32 = jnp.float32
BF16 = jnp.bfloat16

D_MODEL = 1024
D_SGU = 512
SGU_HEADS = 8
SGU_HEAD_DIM = 64
SGU_CHUNK = 128
D_GLA = 512
GLA_HEADS = 4
GLA_DK = 256
GLA_HEAD_K = 64
GLA_HEAD_V = 128
GLA_GATE_RANK = 16
GLA_TAU = 16.0
GLA_CHUNK = 64
N_EXPERTS = 32
TOP_K = 4
SWIGLU_LIMIT = 7.0
SWIGLU_ALPHA = 1.702
EPS = 1e-6

LANES = 128
D_IN_MAIN = 2 * D_SGU + 2 * GLA_DK + 2 * D_GLA
D_IN_PAD = D_IN_MAIN + LANES

MIX_TM = 512
MIX_TILE = 256
MIX_SKEW = 1
MIX_SUB = 256
MOE_BM = 1024
MOE_QUARTERS = 4
CMB_TM = 1024
LAST_GROUP_FRACTION = 2
SC_CORES = 2
SC_SUBCORES = 16
SC_WORKERS = SC_CORES * SC_SUBCORES
SC_CH = 64
NEG_BIG = -1e30

VMEM_LIMIT = 56 * 1024 * 1024


def _ada_kernel(c_ref, w_ref, b_ref, o_ref):
    c = c_ref[...]
    sc = c * jax.nn.sigmoid(c)
    o_ref[...] = jnp.dot(sc.astype(BF16), w_ref[...].astype(BF16), preferred_element_type=F32) + b_ref[...]


def _ada(c_pad, w_ada, b_ada):
    rows, d = c_pad.shape
    n = w_ada.shape[1]
    tn = 1536
    return pl.pallas_call(
        _ada_kernel,
        grid=(n // tn,),
        in_specs=[pl.BlockSpec((rows, d), lambda j: (0, 0)),
                  pl.BlockSpec((d, tn), lambda j: (0, j)),
                  pl.BlockSpec((1, tn), lambda j: (0, j))],
        out_specs=pl.BlockSpec((rows, tn), lambda j: (0, j)),
        out_shape=jax.ShapeDtypeStruct((rows, n), F32),
        compiler_params=pltpu.CompilerParams(dimension_semantics=("arbitrary",), vmem_limit_bytes=VMEM_LIMIT),
        name="ada",
    )(c_pad, w_ada, b_ada)


def _rms(x):
    return x * lax.rsqrt(jnp.mean(x * x, axis=-1, keepdims=True) + EPS)


def _gelu_tanh(x):
    return 0.5 * x * (1.0 + jnp.tanh(0.7978845608028654 * (x + 0.044715 * (x * x * x))))


def _log_sigmoid(z):
    return jnp.minimum(z, 0.0) - jnp.log(1.0 + jnp.exp(-jnp.abs(z)))


def _dot(a, b):
    return jnp.dot(a, b, preferred_element_type=F32)


def _dot_nt(a, b):
    return lax.dot_general(a, b, (((1,), (1,)), ((), ())), preferred_element_type=F32)


def _dot_tn(a, b):
    return lax.dot_general(a, b, (((0,), (0,)), ((), ())), preferred_element_type=F32)


def _pack_rows(x):
    n = x.shape[1] // 2
    lo = lax.bitcast_convert_type(x[:, :n].astype(BF16).astype(F32), jnp.uint32)
    hi = lax.bitcast_convert_type(x[:, n:].astype(BF16).astype(F32), jnp.uint32)
    return (lo >> 16) | (hi & jnp.uint32(0xFFFF0000))


def _unpack_rows(p):
    lo = lax.bitcast_convert_type(p << 16, F32)
    hi = lax.bitcast_convert_type(p & jnp.uint32(0xFFFF0000), F32)
    return lo, hi


def _split_dot(l_bf, a):
    hi = a.astype(BF16)
    lo = (a - hi.astype(F32)).astype(BF16)
    return _dot(l_bf, hi) + _dot(l_bf, lo)


def _mixer_kernel(x_ref, mods_ref, g_pre_ref, w_in_ref, gmat_ref, g_sgu_ref, ws_ref, bs_ref, wgk_ref, bgk_ref,
                  g_gla_ref, w_out_ref, g_post_ref, g_ffn_ref, w_r_ref, b_r_ref,
                  h1_ref, n2_ref, route_ref, probs_ref, cnt_ref,
                  st_ref, carry_ref):
    b = pl.program_id(0)
    j = pl.program_id(1)

    @pl.when(j == 0)
    def _():
        st_ref[...] = jnp.zeros_like(st_ref)

    @pl.when((b == 0) & (j == 0))
    def _():
        carry_ref[...] = jnp.zeros_like(carry_ref)

    carry_box = [carry_ref[...]]
    tiles = [_mixer_tile(slice(s * MIX_TILE, (s + 1) * MIX_TILE), carry_box,
                         x_ref, mods_ref, g_pre_ref, w_in_ref, gmat_ref, g_sgu_ref, ws_ref, bs_ref, wgk_ref, bgk_ref,
                         g_gla_ref, w_out_ref, g_post_ref, g_ffn_ref, w_r_ref, b_r_ref,
                         h1_ref, n2_ref, route_ref, probs_ref, st_ref)
             for s in range(x_ref.shape[1] // MIX_TILE)]
    alive = [True] * len(tiles)
    t = 0
    while any(alive):
        for s, tile in enumerate(tiles):
            if alive[s] and t >= s * MIX_SKEW:
                alive[s] = next(tile, None) is not None
        t += 1
    carry_ref[...] = carry_box[0]
    cnt_ref[...] = jnp.broadcast_to(carry_box[0], cnt_ref.shape)


def _mixer_tile(rows, carry_box, x_ref, mods_ref, g_pre_ref, w_in_ref, gmat_ref, g_sgu_ref, ws_ref, bs_ref, wgk_ref,
                bgk_ref, g_gla_ref, w_out_ref, g_post_ref, g_ffn_ref, w_r_ref, b_r_ref,
                h1_ref, n2_ref, route_ref, probs_ref, st_ref):
    tm = rows.stop - rows.start
    mods = mods_ref[0]
    shift1, scale1, gate1 = mods[0:1], mods[1:2], mods[2:3]
    shift2, scale2 = mods[3:4], mods[4:5]

    x = x_ref[0, rows, :]
    n = _rms(x) * (g_pre_ref[...] * (1.0 + scale1)) + shift1
    nb = n.astype(BF16)
    yield True

    pu = _dot(nb, w_in_ref[:, 0:D_SGU])
    pv = _dot(nb, w_in_ref[:, D_SGU:2 * D_SGU])
    yield True

    u = _gelu_tanh(pu)
    v = _gelu_tanh(pv)
    msv = _dot((v * v).astype(BF16), gmat_ref[...])
    vh = v * lax.rsqrt(msv + EPS) * g_sgu_ref[...]
    yield True

    qk = _dot(nb, w_in_ref[:, 2 * D_SGU:2 * D_SGU + 2 * GLA_DK])
    vv = _dot(nb, w_in_ref[:, 2 * D_SGU + 2 * GLA_DK:2 * D_SGU + 2 * GLA_DK + D_GLA]).astype(BF16)
    r = _dot(nb, w_in_ref[:, D_IN_MAIN - D_GLA:D_IN_MAIN])
    g_low = _dot(nb, w_in_ref[:, D_IN_MAIN:D_IN_PAD])
    yield True

    n_sc = tm // SGU_CHUNK
    lane_c = lax.broadcasted_iota(jnp.int32, (SGU_CHUNK, LANES), 1)
    low_half = lane_c < SGU_HEAD_DIM
    w_row = lax.broadcasted_iota(jnp.int32, (SGU_CHUNK, 2 * SGU_CHUNK), 0)
    w_col = lax.broadcasted_iota(jnp.int32, (SGU_CHUNK, 2 * SGU_CHUNK), 1) & (SGU_CHUNK - 1)
    a_cols = []
    for p in range(SGU_HEADS // 2):
        wcat = jnp.where(w_row >= w_col, ws_ref[p], 0.0).astype(BF16)
        rhs = []
        for ci in range(n_sc):
            vp = vh[ci * SGU_CHUNK:(ci + 1) * SGU_CHUNK, p * LANES:(p + 1) * LANES]
            rhs.append(jnp.concatenate([jnp.where(low_half, vp, 0.0), jnp.where(low_half, 0.0, vp)],
                                       axis=0).astype(BF16))
        res = _dot(wcat, jnp.concatenate(rhs, axis=1))
        bias = bs_ref[:, p * LANES:(p + 1) * LANES]
        a_cols.append(jnp.concatenate(
            [res[:, ci * LANES:(ci + 1) * LANES] + bias for ci in range(n_sc)], axis=0))
    a_out = u * jnp.concatenate(a_cols, axis=1)

    q = qk[:, :GLA_DK] * (GLA_HEAD_K ** -0.5)
    k = qk[:, GLA_DK:]
    z = _dot(g_low.astype(BF16), wgk_ref[...]) + bgk_ref[...]
    log_a = _log_sigmoid(z) * (1.0 / GLA_TAU)

    sub = min(tm, MIX_SUB)
    subs = [slice(s * sub, (s + 1) * sub) for s in range(tm // sub)]
    t_row = lax.broadcasted_iota(jnp.int32, (sub, sub), 0)
    t_col = lax.broadcasted_iota(jnp.int32, (sub, sub), 1)
    same_chunk = (t_row >> 6) == (t_col >> 6)
    l_cum = jnp.where(same_chunk & (t_row >= t_col), 1.0, 0.0).astype(BF16)
    l_all = jnp.where(same_chunk, 1.0, 0.0).astype(BF16)
    bcum = jnp.concatenate([_split_dot(l_cum, log_a[s]) for s in subs], axis=0)
    blast = jnp.concatenate([_split_dot(l_all, log_a[s]) for s in subs], axis=0)
    q_s = q * jnp.exp(bcum)
    k_s = (k * jnp.exp(-bcum)).astype(BF16)
    k_dec = (k * jnp.exp(blast - bcum)).astype(BF16)
    decay = jnp.exp(blast)
    yield True

    lane_g = lax.broadcasted_iota(jnp.int32, (GLA_CHUNK, LANES), 1)
    causal = (lax.broadcasted_iota(jnp.int32, (GLA_CHUNK, GLA_CHUNK), 0)
              >= lax.broadcasted_iota(jnp.int32, (GLA_CHUNK, GLA_CHUNK), 1))
    g_gla = g_gla_ref[...]
    o_rows = []
    for ci in range(tm // GLA_CHUNK):
        rs = slice(ci * GLA_CHUNK, (ci + 1) * GLA_CHUNK)
        o_heads = []
        for h in range(GLA_HEADS):
            pl_ = slice((h // 2) * LANES, (h // 2 + 1) * LANES)
            own = (lane_g >= GLA_HEAD_K) if (h % 2) else (lane_g < GLA_HEAD_K)
            qm = jnp.where(own, q_s[rs, pl_], 0.0).astype(BF16)
            v_h = vv[rs, h * GLA_HEAD_V:(h + 1) * GLA_HEAD_V]
            scores = jnp.where(causal, _dot_nt(qm, k_s[rs, pl_]), 0.0)
            intra = _dot(scores.astype(BF16), v_h)
            st = st_ref[h]
            inter = _dot_nt(qm, st.astype(BF16))
            st_ref[h] = st * decay[ci * GLA_CHUNK:ci * GLA_CHUNK + 1, pl_] + _dot_tn(v_h, k_dec[rs, pl_])
            o_heads.append(_rms(intra + inter) * g_gla)
        o_rows.append(jnp.concatenate(o_heads, axis=1))
    o = jnp.concatenate(o_rows, axis=0) * (r * jax.nn.sigmoid(r))
    mix_in = jnp.concatenate([a_out, o], axis=1).astype(BF16)
    yield True

    mix = _dot(mix_in, w_out_ref[...])
    yield True

    h1 = x + gate1 * (_rms(mix) * g_post_ref[...])
    h1_ref[0, rows, :] = h1
    n2 = _rms(h1) * (g_ffn_ref[...] * (1.0 + scale2)) + shift2
    n2_ref[rows, :] = _pack_rows(n2)
    n2b = n2.astype(BF16)
    yield True

    logits = _dot(n2b, w_r_ref[...]) + b_r_ref[...]
    lane = lax.broadcasted_iota(jnp.int32, (tm, LANES), 1)
    lane_f = lane.astype(F32)
    vals = logits
    sels, tops, idxs = [], [], []
    for _ in range(TOP_K):
        m = jnp.max(vals, axis=-1, keepdims=True)
        idx = jnp.min(jnp.where(vals == m, lane_f, float(LANES)), axis=-1, keepdims=True)
        sel = lane_f == idx
        sels.append(sel)
        tops.append(m)
        idxs.append(idx)
        vals = jnp.where(sel, -jnp.inf, vals)
    es = [jnp.exp(t - tops[0]) for t in tops]
    inv = 1.0 / (es[0] + es[1] + es[2] + es[3])

    multi = jnp.zeros((tm, LANES), F32)
    for sel in sels:
        multi = multi + jnp.where(sel, 1.0, 0.0)
    l_strict = jnp.where(t_row > t_col, 1.0, 0.0).astype(BF16)
    carry = carry_box[0]
    before = []
    for s in subs:
        before.append(_dot(l_strict, multi[s].astype(BF16)) + carry)
        carry = carry + jnp.sum(multi[s], axis=0, keepdims=True)
    before = jnp.concatenate(before, axis=0)
    carry_box[0] = carry

    route = jnp.zeros((tm, LANES), F32)
    probs = jnp.zeros((tm, LANES), F32)
    for kk in range(TOP_K):
        rank_kk = jnp.sum(jnp.where(sels[kk], before, 0.0), axis=-1, keepdims=True)
        route = jnp.where(lane == kk, idxs[kk], route)
        route = jnp.where(lane == TOP_K + kk, rank_kk, route)
        probs = jnp.where(lane == kk, es[kk] * inv, probs)
    route_ref[:, rows] = route.T[0:2 * TOP_K, :].astype(jnp.int32)
    probs_ref[rows, :] = probs


def _const_spec(shape):
    return pl.BlockSpec(shape, lambda b, j: (0,) * len(shape))


def _mixer(b0, B, x, mods, g_pre, w_in_p, gmat, g_sgu, ws_cat, bs_exp, wgk_p, bgk, g_gla, w_out_b, g_post, g_ffn, w_r_p,
           b_r_p):
    _, S, D = x.shape
    T = B * S
    tm = MIX_TM
    nj = S // tm
    return pl.pallas_call(
        _mixer_kernel,
        grid=(B, nj),
        in_specs=[pl.BlockSpec((1, tm, D), lambda b, j: (b + b0, j, 0)),
                  pl.BlockSpec((1, 6, D), lambda b, j: (b + b0, 0, 0)),
                  _const_spec((1, D)), _const_spec(w_in_p.shape), _const_spec(gmat.shape), _const_spec((1, D_SGU)),
                  _const_spec(ws_cat.shape), _const_spec(bs_exp.shape), _const_spec(wgk_p.shape),
                  _const_spec((1, GLA_DK)), _const_spec((1, GLA_HEAD_V)), _const_spec(w_out_b.shape),
                  _const_spec((1, D)), _const_spec((1, D)), _const_spec(w_r_p.shape), _const_spec((1, LANES))],
        out_specs=[pl.BlockSpec((1, tm, D), lambda b, j: (b, j, 0)),
                   pl.BlockSpec((tm, D // 2), lambda b, j: (b * nj + j, 0)),
                   pl.BlockSpec((2 * TOP_K, tm), lambda b, j: (0, b * nj + j)),
                   pl.BlockSpec((tm, LANES), lambda b, j: (b * nj + j, 0)),
                   pl.BlockSpec((8, LANES), lambda b, j: (0, 0))],
        out_shape=[jax.ShapeDtypeStruct((B, S, D), F32),
                   jax.ShapeDtypeStruct((T, D // 2), jnp.uint32),
                   jax.ShapeDtypeStruct((2 * TOP_K, T), jnp.int32),
                   jax.ShapeDtypeStruct((T, LANES), F32),
                   jax.ShapeDtypeStruct((8, LANES), F32)],
        scratch_shapes=[pltpu.VMEM((GLA_HEADS, GLA_HEAD_V, LANES), F32),
                        pltpu.VMEM((1, LANES), F32)],
        compiler_params=pltpu.CompilerParams(dimension_semantics=("arbitrary", "arbitrary"),
                                             vmem_limit_bytes=VMEM_LIMIT),
        name="mixer",
    )(x, mods, g_pre, w_in_p, gmat, g_sgu, ws_cat, bs_exp, wgk_p, bgk, g_gla, w_out_b, g_post, g_ffn, w_r_p, b_r_p)


def _sc_dispatch(rows, dest_kt, n_out):
    T, D = rows.shape
    top_k = dest_kt.shape[0]
    ch = SC_CH
    cpw = T // ch // SC_WORKERS
    mesh = plsc.VectorSubcoreMesh(core_axis_name="c", subcore_axis_name="s")

    @functools.partial(
        pl.kernel, mesh=mesh,
        out_type=jax.ShapeDtypeStruct((n_out, D), rows.dtype),
        scratch_types=[pltpu.VMEM((top_k, ch), jnp.int32), pltpu.VMEM((ch, D), rows.dtype)],
    )
    def k(rows_hbm, dest_hbm, out_hbm, idx_v, rows_v):
        wid = lax.axis_index("s") * SC_CORES + lax.axis_index("c")

        @pl.loop(0, cpw)
        def _(i):
            t0 = (wid * cpw + i) * ch
            for kk in range(top_k):
                pltpu.sync_copy(dest_hbm.at[kk, pl.ds(t0, ch)], idx_v.at[kk])
            pltpu.sync_copy(rows_hbm.at[pl.ds(t0, ch)], rows_v)
            for kk in range(top_k):
                pltpu.sync_copy(rows_v, out_hbm.at[idx_v.at[kk]])

    return k(rows, dest_kt)


def _sc_gather(src, dest_kt):
    _, D = src.shape
    top_k, T = dest_kt.shape
    ch = SC_CH
    cpk = T // ch
    cpw = top_k * cpk // SC_WORKERS
    mesh = plsc.VectorSubcoreMesh(core_axis_name="c", subcore_axis_name="s")

    assert cpw % 2 == 0
    n_pairs = cpw // 2

    @functools.partial(
        pl.kernel, mesh=mesh,
        out_type=jax.ShapeDtypeStruct((top_k * T, D), src.dtype),
        scratch_types=[pltpu.VMEM((2, ch), jnp.int32), pltpu.VMEM((2, ch, D), src.dtype),
                       pltpu.SemaphoreType.DMA, pltpu.SemaphoreType.DMA,
                       pltpu.SemaphoreType.DMA, pltpu.SemaphoreType.DMA],
    )
    def k(src_hbm, idx_hbm, out_hbm, idx_v, rows_v, gsem0, gsem1, wsem0, wsem1):
        wid = lax.axis_index("s") * SC_CORES + lax.axis_index("c")
        gsem = (gsem0, gsem1)
        wsem = (wsem0, wsem1)

        def gather_copy(b):
            return pltpu.make_async_copy(src_hbm.at[idx_v.at[b]], rows_v.at[b], gsem[b])

        def write_copy(c, b):
            return pltpu.make_async_copy(rows_v.at[b], out_hbm.at[pl.ds(c * ch, ch)], wsem[b])

        def start_gather(c, b):
            kk = c // cpk
            t0 = (c - kk * cpk) * ch
            pltpu.sync_copy(idx_hbm.at[kk, pl.ds(t0, ch)], idx_v.at[b])
            gather_copy(b).start()

        c_first = wid * cpw
        start_gather(c_first, 0)

        @pl.loop(0, n_pairs)
        def _(j):
            c0 = c_first + 2 * j
            c1 = c0 + 1

            @pl.when(j > 0)
            def _():
                write_copy(c0 - 1, 1).wait()

            start_gather(c1, 1)
            gather_copy(0).wait()
            write_copy(c0, 0).start()
            write_copy(c0, 0).wait()

            @pl.when(j < n_pairs - 1)
            def _():
                start_gather(c0 + 2, 0)

            gather_copy(1).wait()
            write_copy(c1, 1).start()

        write_copy(c_first + cpw - 1, 1).wait()

    return k(src, dest_kt)


def _expert_kernel(be_ref, first_ref, last_ref, nxt_ref, slot_ref, quarters_ref, nu_ref,
                   x_ref, wg_hbm, bg_ref, wu_hbm, bu_ref, wd_hbm, bd_ref, o_ref,
                   stage_ref, wbf_ref, sem):
    i = pl.program_id(0)
    active = i < nu_ref[0]
    has_next = nxt_ref[i] >= 0

    def weight_copies(e):
        return [pltpu.make_async_copy(w.at[e], stage_ref.at[m], sem.at[m])
                for m, w in enumerate((wg_hbm, wu_hbm, wd_hbm))]

    def cast_stage_into(slot):
        for m in range(3):
            wbf_ref[slot, m] = stage_ref[m].astype(BF16)

    @pl.when(i == 0)
    def _():
        for cp in weight_copies(be_ref[0]):
            cp.start()
        for cp in weight_copies(be_ref[0]):
            cp.wait()
        cast_stage_into(0)

    @pl.when(active & (first_ref[i] == 1) & has_next)
    def _():
        for cp in weight_copies(nxt_ref[i]):
            cp.start()

    def ffn(rows):
        slot = slot_ref[i]
        x_lo, x_hi = _unpack_rows(x_ref[rows, :])
        xb = jnp.concatenate([x_lo, x_hi], axis=1).astype(BF16)
        g = jnp.minimum(_dot(xb, wbf_ref[slot, 0]) + bg_ref[0], SWIGLU_LIMIT)
        u = jnp.clip(_dot(xb, wbf_ref[slot, 1]) + bu_ref[0], -SWIGLU_LIMIT, SWIGLU_LIMIT)
        hdn = (u + 1.0) * (g * jax.nn.sigmoid(SWIGLU_ALPHA * g))
        o_ref[rows, :] = _pack_rows(_dot(hdn.astype(BF16), wbf_ref[slot, 2]) + bd_ref[0])

    bm = x_ref.shape[0]
    for nq in range(1, MOE_QUARTERS + 1):
        @pl.when(active & (quarters_ref[i] == nq))
        def _(nq=nq):
            ffn(slice(0, nq * bm // MOE_QUARTERS))

    @pl.when(active & (last_ref[i] == 1) & has_next)
    def _():
        for cp in weight_copies(nxt_ref[i]):
            cp.wait()
        cast_stage_into(1 - slot_ref[i])


def _expert_schedule(counts, padded, n_blocks, bm):
    pend = jnp.cumsum(padded)
    ids = jnp.arange(N_EXPERTS, dtype=jnp.int32)
    idx = jnp.arange(n_blocks, dtype=jnp.int32)
    n_used = (pend[-1] // bm).astype(jnp.int32)
    block_e = jnp.minimum(jnp.sum((pend[None, :] <= (idx * bm)[:, None]).astype(jnp.int32), axis=1), N_EXPERTS - 1)
    onehot = block_e[:, None] == ids[None, :]

    def per_block(table):
        return jnp.sum(jnp.where(onehot, table[None, :], 0), axis=1)

    start_blk = (pend - padded) // bm
    first = (idx == per_block(start_blk)) & (idx < n_used)
    last = (idx == per_block(pend // bm) - 1) & (idx < n_used)
    later = (ids[None, :] > ids[:, None]) & (padded[None, :] > 0)
    nxt_e = jnp.min(jnp.where(later, ids[None, :], N_EXPERTS), axis=1)
    nxt_e = jnp.where(nxt_e == N_EXPERTS, -1, nxt_e)
    order = jnp.cumsum((padded > 0).astype(jnp.int32)) - 1
    real_rows = per_block(counts) - (idx - per_block(start_blk)) * bm
    qrows = bm // MOE_QUARTERS
    quarters = jnp.clip((real_rows + qrows - 1) // qrows, 1, MOE_QUARTERS)
    as_i32 = lambda a: a.astype(jnp.int32)
    return (block_e, as_i32(first), as_i32(last), per_block(nxt_e), per_block(order) & 1, as_i32(quarters),
            n_used.reshape(1))


def _experts(schedule, xs, w_gate, b_gate, w_up, b_up, w_down, b_down):
    P, half = xs.shape
    bm = MOE_BM
    n_blocks = P // bm
    E, D, DE = w_gate.shape

    def row_map(i, be, first, last, nxt, slot, quarters, nu):
        return (jnp.minimum(i, nu[0] - 1), 0)

    def b_map(i, be, first, last, nxt, slot, quarters, nu):
        return (be[jnp.minimum(i, nu[0] - 1)], 0, 0)

    hbm = pl.BlockSpec(memory_space=pl.ANY)
    grid_spec = pltpu.PrefetchScalarGridSpec(
        num_scalar_prefetch=len(schedule),
        grid=(n_blocks,),
        in_specs=[pl.BlockSpec((bm, half), row_map),
                  hbm, pl.BlockSpec((1, 1, DE), b_map),
                  hbm, pl.BlockSpec((1, 1, DE), b_map),
                  hbm, pl.BlockSpec((1, 1, D), b_map)],
        out_specs=pl.BlockSpec((bm, half), row_map),
        scratch_shapes=[pltpu.VMEM((3, D, DE), F32),
                        pltpu.VMEM((2, 3, D, DE), BF16),
                        pltpu.SemaphoreType.DMA((3,))],
    )
    return pl.pallas_call(
        _expert_kernel,
        grid_spec=grid_spec,
        out_shape=jax.ShapeDtypeStruct((P, half), jnp.uint32),
        compiler_params=pltpu.CompilerParams(dimension_semantics=("arbitrary",), vmem_limit_bytes=VMEM_LIMIT),
        name="experts",
    )(*schedule, xs, w_gate, b_gate.reshape(E, 1, DE), w_up, b_up.reshape(E, 1, DE),
      w_down, b_down.reshape(E, 1, D))


def _combine_kernel(h1_ref, og_ref, probs_ref, mods_ref, g_ref, *rest):
    o_ref = rest[-1]
    tm = h1_ref.shape[1]
    probs = probs_ref[...]
    lane = lax.broadcasted_iota(jnp.int32, (tm, LANES), 1)
    half = og_ref.shape[2]
    f_lo = jnp.zeros((tm, half), F32)
    f_hi = jnp.zeros((tm, half), F32)
    for kk in range(TOP_K):
        pk = jnp.sum(jnp.where(lane == kk, probs, 0.0), axis=-1, keepdims=True)
        o_lo, o_hi = _unpack_rows(og_ref[kk])
        f_lo = f_lo + pk * o_lo
        f_hi = f_hi + pk * o_hi
    f = jnp.concatenate([f_lo, f_hi], axis=1)
    gate2 = mods_ref[0][5:6]
    o_ref[0] = h1_ref[0] + gate2 * (_rms(f) * g_ref[...])


def _combine(b0, b_total, h1, og, probs, mods, g_post_ffn, out_so_far):
    B, S, D = h1.shape
    tm = CMB_TM
    nj = S // tm
    in_specs = [pl.BlockSpec((1, tm, D), lambda b, j: (b, j, 0)),
                pl.BlockSpec((TOP_K, tm, D // 2), lambda b, j: (0, b * nj + j, 0)),
                pl.BlockSpec((tm, LANES), lambda b, j: (b * nj + j, 0)),
                pl.BlockSpec((1, 6, D), lambda b, j: (b + b0, 0, 0)),
                pl.BlockSpec((1, D), lambda b, j: (0, 0))]
    args = [h1, og, probs, mods, g_post_ffn]
    aliases = {}
    if out_so_far is not None:
        in_specs.append(pl.BlockSpec(memory_space=pl.ANY))
        args.append(out_so_far)
        aliases = {len(args) - 1: 0}
    return pl.pallas_call(
        _combine_kernel,
        grid=(B, nj),
        in_specs=in_specs,
        out_specs=pl.BlockSpec((1, tm, D), lambda b, j: (b + b0, j, 0)),
        out_shape=jax.ShapeDtypeStruct((b_total, S, D), F32),
        input_output_aliases=aliases,
        compiler_params=pltpu.CompilerParams(dimension_semantics=("arbitrary", "arbitrary"),
                                             vmem_limit_bytes=VMEM_LIMIT),
        name="combine",
    )(*args)


def _layer(h, c_pad, w_ada, b_ada, g_pre_mix, w_in, g_sgu_v, w_s, b_s, w_gk2, b_gk, g_gla_out, w_out, g_post_mix,
           g_pre_ffn, w_router, b_router, w_gate, b_gate, w_up, b_up, w_down, b_down, g_post_ffn):
    B, S, D = h.shape

    mods = _ada(c_pad, w_ada, b_ada.reshape(1, -1))[:B].reshape(B, 6, D)

    w_in_p = jnp.pad(w_in, ((0, 0), (0, D_IN_PAD - w_in.shape[1]))).astype(BF16)
    head_of = jnp.arange(D_SGU) // SGU_HEAD_DIM
    gmat = jnp.where(head_of[:, None] == head_of[None, :], 1.0 / SGU_HEAD_DIM, 0.0).astype(BF16)
    ws_cat = w_s.reshape(SGU_HEADS // 2, 2, SGU_CHUNK, SGU_CHUNK).transpose(0, 2, 1, 3).reshape(
        SGU_HEADS // 2, SGU_CHUNK, 2 * SGU_CHUNK)
    bs_exp = jnp.repeat(b_s.T, SGU_HEAD_DIM, axis=1)
    wgk_p = jnp.pad(w_gk2, ((0, LANES - GLA_GATE_RANK), (0, 0))).astype(BF16)
    w_r_p = jnp.pad(w_router, ((0, 0), (0, LANES - N_EXPERTS))).astype(BF16)
    b_r_p = jnp.concatenate([b_router, jnp.full((LANES - N_EXPERTS,), NEG_BIG, F32)]).reshape(1, LANES)

    w_out_b = w_out.astype(BF16)

    last = max(1, B // LAST_GROUP_FRACTION)
    group_sizes = [B - last, last] if B > 1 else [B]
    result = None
    b0 = 0
    for bg in group_sizes:
        tg = bg * S
        h1, n2, route, probs, cnt = _mixer(
            b0, bg, h, mods, g_pre_mix.reshape(1, D), w_in_p, gmat, g_sgu_v.reshape(1, D_SGU), ws_cat, bs_exp, wgk_p,
            b_gk.reshape(1, GLA_DK), g_gla_out.reshape(1, GLA_HEAD_V), w_out_b, g_post_mix.reshape(1, D),
            g_pre_ffn.reshape(1, D), w_r_p, b_r_p)

        bm = MOE_BM
        counts = cnt[0, :N_EXPERTS].astype(jnp.int32)
        padded = ((counts + bm - 1) // bm) * bm
        pend = jnp.cumsum(padded)
        pstart = pend - padded
        e_kt, rank_kt = route[:TOP_K], route[TOP_K:]
        onehot = e_kt[:, :, None] == jnp.arange(N_EXPERTS, dtype=jnp.int32)[None, None, :]
        dest_kt = rank_kt + jnp.sum(jnp.where(onehot, pstart[None, None, :], 0), axis=-1)
        n_blocks = tg * TOP_K // bm + N_EXPERTS
        schedule = _expert_schedule(counts, padded, n_blocks, bm)

        xs = _sc_dispatch(n2, dest_kt, n_blocks * bm)
        out = _experts(schedule, xs, w_gate, b_gate, w_up, b_up, w_down, b_down)
        og = _sc_gather(out, dest_kt).reshape(TOP_K, tg, D // 2)
        result = _combine(b0, B, h1, og, probs, mods, g_post_ffn.reshape(1, D), result)
        b0 += bg
    return result


def kernel(x, c, w_ada, b_ada, g_pre_mix, w_in, g_sgu_v, w_s, b_s, w_gk2, b_gk, g_gla_out, w_out, g_post_mix, g_pre_ffn, w_router, b_router, w_gate, b_gate, w_up, b_up, w_down, b_down, g_post_ffn):
    B = x.shape[0]
    c_pad = jnp.pad(c, ((0, -B % 8), (0, 0)))
    h = x
    for l in range(w_ada.shape[0]):
        h = _layer(h, c_pad, w_ada[l], b_ada[l], g_pre_mix[l], w_in[l], g_sgu_v[l], w_s[l], b_s[l], w_gk2[l], b_gk[l],
                   g_gla_out[l], w_out[l], g_post_mix[l], g_pre_ffn[l], w_router[l], b_router[l], w_gate[l], b_gate[l],
                   w_up[l], b_up[l], w_down[l], b_down[l], g_post_ffn[l])
    return h
```

```python
import functools

import jax
import jax.numpy as jnp
from jax import lax
from jax.experimental import pallas as pl
from jax.experimental.pallas import tpu as pltpu
from jax.experimental.pallas import tpu_sc as plsc

F32 = jnp.float32
BF16 = jnp.bfloat16

D_MODEL = 1024
D_SGU = 512
SGU_HEADS = 8
SGU_HEAD_DIM = 64
SGU_CHUNK = 128
D_GLA = 512
GLA_HEADS = 4
GLA_DK = 256
GLA_HEAD_K = 64
GLA_HEAD_V = 128
GLA_GATE_RANK = 16
GLA_TAU = 16.0
GLA_CHUNK = 64
N_EXPERTS = 32
TOP_K = 4
SWIGLU_LIMIT = 7.0
SWIGLU_ALPHA = 1.702
EPS = 1e-6

LANES = 128
D_IN_MAIN = 2 * D_SGU + 2 * GLA_DK + 2 * D_GLA
D_IN_PAD = D_IN_MAIN + LANES

MIX_TM = 1024
MIX_TILE = 512
MIX_SKEW = 1
MIX_SUB = 256
MOE_BM = 1024
MOE_QUARTERS = 4
CMB_TM = 1024
LAST_GROUP_FRACTION = 2
SC_CORES = 2
SC_SUBCORES = 16
SC_WORKERS = SC_CORES * SC_SUBCORES
SC_CH = 64
NEG_BIG = -1e30

VMEM_LIMIT = 56 * 1024 * 1024


def _ada_kernel(c_ref, w_ref, b_ref, o_ref):
    c = c_ref[...]
    sc = c * jax.nn.sigmoid(c)
    o_ref[...] = jnp.dot(sc.astype(BF16), w_ref[...].astype(BF16), preferred_element_type=F32) + b_ref[...]


def _ada(c_pad, w_ada, b_ada):
    rows, d = c_pad.shape
    n = w_ada.shape[1]
    tn = 1536
    return pl.pallas_call(
        _ada_kernel,
        grid=(n // tn,),
        in_specs=[pl.BlockSpec((rows, d), lambda j: (0, 0)),
                  pl.BlockSpec((d, tn), lambda j: (0, j)),
                  pl.BlockSpec((1, tn), lambda j: (0, j))],
        out_specs=pl.BlockSpec((rows, tn), lambda j: (0, j)),
        out_shape=jax.ShapeDtypeStruct((rows, n), F32),
        compiler_params=pltpu.CompilerParams(dimension_semantics=("arbitrary",), vmem_limit_bytes=VMEM_LIMIT),
        name="ada",
    )(c_pad, w_ada, b_ada)


def _rms(x):
    return x * lax.rsqrt(jnp.mean(x * x, axis=-1, keepdims=True) + EPS)


def _gelu_tanh(x):
    return 0.5 * x * (1.0 + jnp.tanh(0.7978845608028654 * (x + 0.044715 * (x * x * x))))


def _log_sigmoid(z):
    return jnp.minimum(z, 0.0) - jnp.log(1.0 + jnp.exp(-jnp.abs(z)))


def _dot(a, b):
    return jnp.dot(a, b, preferred_element_type=F32)


def _dot_nt(a, b):
    return lax.dot_general(a, b, (((1,), (1,)), ((), ())), preferred_element_type=F32)


def _dot_tn(a, b):
    return lax.dot_general(a, b, (((0,), (0,)), ((), ())), preferred_element_type=F32)


def _pack_rows(x):
    n = x.shape[1] // 2
    lo = lax.bitcast_convert_type(x[:, :n].astype(BF16).astype(F32), jnp.uint32)
    hi = lax.bitcast_convert_type(x[:, n:].astype(BF16).astype(F32), jnp.uint32)
    return (lo >> 16) | (hi & jnp.uint32(0xFFFF0000))


def _unpack_rows(p):
    lo = lax.bitcast_convert_type(p << 16, F32)
    hi = lax.bitcast_convert_type(p & jnp.uint32(0xFFFF0000), F32)
    return lo, hi


def _split_dot(l_bf, a):
    hi = a.astype(BF16)
    lo = (a - hi.astype(F32)).astype(BF16)
    return _dot(l_bf, hi) + _dot(l_bf, lo)


def _mixer_kernel(x_ref, mods_ref, g_pre_ref, w_in_ref, gmat_ref, g_sgu_ref, ws_ref, bs_ref, wgk_ref, bgk_ref,
                  g_gla_ref, w_out_ref, g_post_ref, g_ffn_ref, w_r_ref, b_r_ref,
                  h1_ref, n2_ref, route_ref, probs_ref, cnt_ref,
                  st_ref, carry_ref):
    b = pl.program_id(0)
    j = pl.program_id(1)

    @pl.when(j == 0)
    def _():
        st_ref[...] = jnp.zeros_like(st_ref)

    @pl.when((b == 0) & (j == 0))
    def _():
        carry_ref[...] = jnp.zeros_like(carry_ref)

    carry_box = [carry_ref[...]]
    tiles = [_mixer_tile(slice(s * MIX_TILE, (s + 1) * MIX_TILE), carry_box,
                         x_ref, mods_ref, g_pre_ref, w_in_ref, gmat_ref, g_sgu_ref, ws_ref, bs_ref, wgk_ref, bgk_ref,
                         g_gla_ref, w_out_ref, g_post_ref, g_ffn_ref, w_r_ref, b_r_ref,
                         h1_ref, n2_ref, route_ref, probs_ref, st_ref)
             for s in range(x_ref.shape[1] // MIX_TILE)]
    alive = [True] * len(tiles)
    t = 0
    while any(alive):
        for s, tile in enumerate(tiles):
            if alive[s] and t >= s * MIX_SKEW:
                alive[s] = next(tile, None) is not None
        t += 1
    carry_ref[...] = carry_box[0]
    cnt_ref[...] = jnp.broadcast_to(carry_box[0], cnt_ref.shape)


def _mixer_tile(rows, carry_box, x_ref, mods_ref, g_pre_ref, w_in_ref, gmat_ref, g_sgu_ref, ws_ref, bs_ref, wgk_ref,
                bgk_ref, g_gla_ref, w_out_ref, g_post_ref, g_ffn_ref, w_r_ref, b_r_ref,
                h1_ref, n2_ref, route_ref, probs_ref, st_ref):
    tm = rows.stop - rows.start
    mods = mods_ref[0]
    shift1, scale1, gate1 = mods[0:1], mods[1:2], mods[2:3]
    shift2, scale2 = mods[3:4], mods[4:5]

    x = x_ref[0, rows, :]
    n = _rms(x) * (g_pre_ref[...] * (1.0 + scale1)) + shift1
    nb = n.astype(BF16)
    yield True

    pu = _dot(nb, w_in_ref[:, 0:D_SGU])
    pv = _dot(nb, w_in_ref[:, D_SGU:2 * D_SGU])
    yield True

    u = _gelu_tanh(pu)
    v = _gelu_tanh(pv)
    msv = _dot((v * v).astype(BF16), gmat_ref[...])
    vh = v * lax.rsqrt(msv + EPS) * g_sgu_ref[...]
    yield True

    qk = _dot(nb, w_in_ref[:, 2 * D_SGU:2 * D_SGU + 2 * GLA_DK])
    vv = _dot(nb, w_in_ref[:, 2 * D_SGU + 2 * GLA_DK:2 * D_SGU + 2 * GLA_DK + D_GLA]).astype(BF16)
    r = _dot(nb, w_in_ref[:, D_IN_MAIN - D_GLA:D_IN_MAIN])
    g_low = _dot(nb, w_in_ref[:, D_IN_MAIN:D_IN_PAD])
    yield True

    n_sc = tm // SGU_CHUNK
    lane_c = lax.broadcasted_iota(jnp.int32, (SGU_CHUNK, LANES), 1)
    low_half = lane_c < SGU_HEAD_DIM
    w_row = lax.broadcasted_iota(jnp.int32, (SGU_CHUNK, 2 * SGU_CHUNK), 0)
    w_col = lax.broadcasted_iota(jnp.int32, (SGU_CHUNK, 2 * SGU_CHUNK), 1) & (SGU_CHUNK - 1)
    a_cols = []
    for p in range(SGU_HEADS // 2):
        wcat = jnp.where(w_row >= w_col, ws_ref[p], 0.0).astype(BF16)
        rhs = []
        for ci in range(n_sc):
            vp = vh[ci * SGU_CHUNK:(ci + 1) * SGU_CHUNK, p * LANES:(p + 1) * LANES]
            rhs.append(jnp.concatenate([jnp.where(low_half, vp, 0.0), jnp.where(low_half, 0.0, vp)],
                                       axis=0).astype(BF16))
        res = _dot(wcat, jnp.concatenate(rhs, axis=1))
        bias = bs_ref[:, p * LANES:(p + 1) * LANES]
        a_cols.append(jnp.concatenate(
            [res[:, ci * LANES:(ci + 1) * LANES] + bias for ci in range(n_sc)], axis=0))
    a_out = u * jnp.concatenate(a_cols, axis=1)

    q = qk[:, :GLA_DK] * (GLA_HEAD_K ** -0.5)
    k = qk[:, GLA_DK:]
    z = _dot(g_low.astype(BF16), wgk_ref[...]) + bgk_ref[...]
    log_a = _log_sigmoid(z) * (1.0 / GLA_TAU)

    sub = min(tm, MIX_SUB)
    subs = [slice(s * sub, (s + 1) * sub) for s in range(tm // sub)]
    t_row = lax.broadcasted_iota(jnp.int32, (sub, sub), 0)
    t_col = lax.broadcasted_iota(jnp.int32, (sub, sub), 1)
    same_chunk = (t_row >> 6) == (t_col >> 6)
    l_cum = jnp.where(same_chunk & (t_row >= t_col), 1.0, 0.0).astype(BF16)
    l_all = jnp.where(same_chunk, 1.0, 0.0).astype(BF16)
    bcum = jnp.concatenate([_split_dot(l_cum, log_a[s]) for s in subs], axis=0)
    blast = jnp.concatenate([_split_dot(l_all, log_a[s]) for s in subs], axis=0)
    q_s = q * jnp.exp(bcum)
    k_s = (k * jnp.exp(-bcum)).astype(BF16)
    k_dec = (k * jnp.exp(blast - bcum)).astype(BF16)
    decay = jnp.exp(blast)
    yield True

    lane_g = lax.broadcasted_iota(jnp.int32, (GLA_CHUNK, LANES), 1)
    causal = (lax.broadcasted_iota(jnp.int32, (GLA_CHUNK, GLA_CHUNK), 0)
              >= lax.broadcasted_iota(jnp.int32, (GLA_CHUNK, GLA_CHUNK), 1))
    g_gla = g_gla_ref[...]
    o_rows = []
    for ci in range(tm // GLA_CHUNK):
        rs = slice(ci * GLA_CHUNK, (ci + 1) * GLA_CHUNK)
        o_heads = []
        for h in range(GLA_HEADS):
            pl_ = slice((h // 2) * LANES, (h // 2 + 1) * LANES)
            own = (lane_g >= GLA_HEAD_K) if (h % 2) else (lane_g < GLA_HEAD_K)
            qm = jnp.where(own, q_s[rs, pl_], 0.0).astype(BF16)
            v_h = vv[rs, h * GLA_HEAD_V:(h + 1) * GLA_HEAD_V]
            scores = jnp.where(causal, _dot_nt(qm, k_s[rs, pl_]), 0.0)
            intra = _dot(scores.astype(BF16), v_h)
            st = st_ref[h]
            inter = _dot_nt(qm, st.astype(BF16))
            st_ref[h] = st * decay[ci * GLA_CHUNK:ci * GLA_CHUNK + 1, pl_] + _dot_tn(v_h, k_dec[rs, pl_])
            o_heads.append(_rms(intra + inter) * g_gla)
        o_rows.append(jnp.concatenate(o_heads, axis=1))
    o = jnp.concatenate(o_rows, axis=0) * (r * jax.nn.sigmoid(r))
    mix_in = jnp.concatenate([a_out, o], axis=1).astype(BF16)
    yield True

    mix = _dot(mix_in, w_out_ref[...])
    yield True

    h1 = x + gate1 * (_rms(mix) * g_post_ref[...])
    h1_ref[0, rows, :] = h1
    n2 = _rms(h1) * (g_ffn_ref[...] * (1.0 + scale2)) + shift2
    n2_ref[rows, :] = _pack_rows(n2)
    n2b = n2.astype(BF16)
    yield True

    logits = _dot(n2b, w_r_ref[...]) + b_r_ref[...]
    lane = lax.broadcasted_iota(jnp.int32, (tm, LANES), 1)
    lane_f = lane.astype(F32)
    vals = logits
    sels, tops, idxs = [], [], []
    for _ in range(TOP_K):
        m = jnp.max(vals, axis=-1, keepdims=True)
        idx = jnp.min(jnp.where(vals == m, lane_f, float(LANES)), axis=-1, keepdims=True)
        sel = lane_f == idx
        sels.append(sel)
        tops.append(m)
        idxs.append(idx)
        vals = jnp.where(sel, -jnp.inf, vals)
    es = [jnp.exp(t - tops[0]) for t in tops]
    inv = 1.0 / (es[0] + es[1] + es[2] + es[3])

    multi = jnp.zeros((tm, LANES), F32)
    for sel in sels:
        multi = multi + jnp.where(sel, 1.0, 0.0)
    l_strict = jnp.where(t_row > t_col, 1.0, 0.0).astype(BF16)
    carry = carry_box[0]
    before = []
    for s in subs:
        before.append(_dot(l_strict, multi[s].astype(BF16)) + carry)
        carry = carry + jnp.sum(multi[s], axis=0, keepdims=True)
    before = jnp.concatenate(before, axis=0)
    carry_box[0] = carry

    route = jnp.zeros((tm, LANES), F32)
    probs = jnp.zeros((tm, LANES), F32)
    for kk in range(TOP_K):
        rank_kk = jnp.sum(jnp.where(sels[kk], before, 0.0), axis=-1, keepdims=True)
        route = jnp.where(lane == kk, idxs[kk], route)
        route = jnp.where(lane == TOP_K + kk, rank_kk, route)
        probs = jnp.where(lane == kk, es[kk] * inv, probs)
    route_ref[:, rows] = route.T[0:2 * TOP_K, :].astype(jnp.int32)
    probs_ref[rows, :] = probs


def _const_spec(shape):
    return pl.BlockSpec(shape, lambda b, j: (0,) * len(shape))


def _mixer(b0, B, x, mods, g_pre, w_in_p, gmat, g_sgu, ws_cat, bs_exp, wgk_p, bgk, g_gla, w_out_b, g_post, g_ffn, w_r_p,
           b_r_p):
    _, S, D = x.shape
    T = B * S
    tm = MIX_TM
    nj = S // tm
    return pl.pallas_call(
        _mixer_kernel,
        grid=(B, nj),
        in_specs=[pl.BlockSpec((1, tm, D), lambda b, j: (b + b0, j, 0)),
                  pl.BlockSpec((1, 6, D), lambda b, j: (b + b0, 0, 0)),
                  _const_spec((1, D)), _const_spec(w_in_p.shape), _const_spec(gmat.shape), _const_spec((1, D_SGU)),
                  _const_spec(ws_cat.shape), _const_spec(bs_exp.shape), _const_spec(wgk_p.shape),
                  _const_spec((1, GLA_DK)), _const_spec((1, GLA_HEAD_V)), _const_spec(w_out_b.shape),
                  _const_spec((1, D)), _const_spec((1, D)), _const_spec(w_r_p.shape), _const_spec((1, LANES))],
        out_specs=[pl.BlockSpec((1, tm, D), lambda b, j: (b, j, 0)),
                   pl.BlockSpec((tm, D // 2), lambda b, j: (b * nj + j, 0)),
                   pl.BlockSpec((2 * TOP_K, tm), lambda b, j: (0, b * nj + j)),
                   pl.BlockSpec((tm, LANES), lambda b, j: (b * nj + j, 0)),
                   pl.BlockSpec((8, LANES), lambda b, j: (0, 0))],
        out_shape=[jax.ShapeDtypeStruct((B, S, D), F32),
                   jax.ShapeDtypeStruct((T, D // 2), jnp.uint32),
                   jax.ShapeDtypeStruct((2 * TOP_K, T), jnp.int32),
                   jax.ShapeDtypeStruct((T, LANES), F32),
                   jax.ShapeDtypeStruct((8, LANES), F32)],
        scratch_shapes=[pltpu.VMEM((GLA_HEADS, GLA_HEAD_V, LANES), F32),
                        pltpu.VMEM((1, LANES), F32)],
        compiler_params=pltpu.CompilerParams(dimension_semantics=("arbitrary", "arbitrary"),
                                             vmem_limit_bytes=VMEM_LIMIT),
        name="mixer",
    )(x, mods, g_pre, w_in_p, gmat, g_sgu, ws_cat, bs_exp, wgk_p, bgk, g_gla, w_out_b, g_post, g_ffn, w_r_p, b_r_p)


def _sc_dispatch(rows, dest_kt, n_out):
    T, D = rows.shape
    top_k = dest_kt.shape[0]
    ch = SC_CH
    cpw = T // ch // SC_WORKERS
    mesh = plsc.VectorSubcoreMesh(core_axis_name="c", subcore_axis_name="s")

    @functools.partial(
        pl.kernel, mesh=mesh,
        out_type=jax.ShapeDtypeStruct((n_out, D), rows.dtype),
        scratch_types=[pltpu.VMEM((top_k, ch), jnp.int32), pltpu.VMEM((ch, D), rows.dtype)],
    )
    def k(rows_hbm, dest_hbm, out_hbm, idx_v, rows_v):
        wid = lax.axis_index("s") * SC_CORES + lax.axis_index("c")

        @pl.loop(0, cpw)
        def _(i):
            t0 = (wid * cpw + i) * ch
            for kk in range(top_k):
                pltpu.sync_copy(dest_hbm.at[kk, pl.ds(t0, ch)], idx_v.at[kk])
            pltpu.sync_copy(rows_hbm.at[pl.ds(t0, ch)], rows_v)
            for kk in range(top_k):
                pltpu.sync_copy(rows_v, out_hbm.at[idx_v.at[kk]])

    return k(rows, dest_kt)


def _sc_gather(src, dest_kt):
    _, D = src.shape
    top_k, T = dest_kt.shape
    ch = SC_CH
    cpk = T // ch
    cpw = top_k * cpk // SC_WORKERS
    mesh = plsc.VectorSubcoreMesh(core_axis_name="c", subcore_axis_name="s")

    assert cpw % 2 == 0
    n_pairs = cpw // 2

    @functools.partial(
        pl.kernel, mesh=mesh,
        out_type=jax.ShapeDtypeStruct((top_k * T, D), src.dtype),
        scratch_types=[pltpu.VMEM((2, ch), jnp.int32), pltpu.VMEM((2, ch, D), src.dtype),
                       pltpu.SemaphoreType.DMA, pltpu.SemaphoreType.DMA,
                       pltpu.SemaphoreType.DMA, pltpu.SemaphoreType.DMA],
    )
    def k(src_hbm, idx_hbm, out_hbm, idx_v, rows_v, gsem0, gsem1, wsem0, wsem1):
        wid = lax.axis_index("s") * SC_CORES + lax.axis_index("c")
        gsem = (gsem0, gsem1)
        wsem = (wsem0, wsem1)

        def gather_copy(b):
            return pltpu.make_async_copy(src_hbm.at[idx_v.at[b]], rows_v.at[b], gsem[b])

        def write_copy(c, b):
            return pltpu.make_async_copy(rows_v.at[b], out_hbm.at[pl.ds(c * ch, ch)], wsem[b])

        def start_gather(c, b):
            kk = c // cpk
            t0 = (c - kk * cpk) * ch
            pltpu.sync_copy(idx_hbm.at[kk, pl.ds(t0, ch)], idx_v.at[b])
            gather_copy(b).start()

        c_first = wid * cpw
        start_gather(c_first, 0)

        @pl.loop(0, n_pairs)
        def _(j):
            c0 = c_first + 2 * j
            c1 = c0 + 1

            @pl.when(j > 0)
            def _():
                write_copy(c0 - 1, 1).wait()

            start_gather(c1, 1)
            gather_copy(0).wait()
            write_copy(c0, 0).start()
            write_copy(c0, 0).wait()

            @pl.when(j < n_pairs - 1)
            def _():
                start_gather(c0 + 2, 0)

            gather_copy(1).wait()
            write_copy(c1, 1).start()

        write_copy(c_first + cpw - 1, 1).wait()

    return k(src, dest_kt)


def _expert_kernel(be_ref, first_ref, last_ref, nxt_ref, slot_ref, quarters_ref, nu_ref,
                   x_ref, wg_hbm, bg_ref, wu_hbm, bu_ref, wd_hbm, bd_ref, o_ref,
                   stage_ref, wbf_ref, sem):
    i = pl.program_id(0)
    active = i < nu_ref[0]
    has_next = nxt_ref[i] >= 0

    def weight_copies(e):
        return [pltpu.make_async_copy(w.at[e], stage_ref.at[m], sem.at[m])
                for m, w in enumerate((wg_hbm, wu_hbm, wd_hbm))]

    def cast_stage_into(slot):
        for m in range(3):
            wbf_ref[slot, m] = stage_ref[m].astype(BF16)

    @pl.when(i == 0)
    def _():
        for cp in weight_copies(be_ref[0]):
            cp.start()
        for cp in weight_copies(be_ref[0]):
            cp.wait()
        cast_stage_into(0)

    @pl.when(active & (first_ref[i] == 1) & has_next)
    def _():
        for cp in weight_copies(nxt_ref[i]):
            cp.start()

    def ffn(rows):
        slot = slot_ref[i]
        x_lo, x_hi = _unpack_rows(x_ref[rows, :])
        xb = jnp.concatenate([x_lo, x_hi], axis=1).astype(BF16)
        g = jnp.minimum(_dot(xb, wbf_ref[slot, 0]) + bg_ref[0], SWIGLU_LIMIT)
        u = jnp.clip(_dot(xb, wbf_ref[slot, 1]) + bu_ref[0], -SWIGLU_LIMIT, SWIGLU_LIMIT)
        hdn = (u + 1.0) * (g * jax.nn.sigmoid(SWIGLU_ALPHA * g))
        o_ref[rows, :] = _pack_rows(_dot(hdn.astype(BF16), wbf_ref[slot, 2]) + bd_ref[0])

    bm = x_ref.shape[0]
    for nq in range(1, MOE_QUARTERS + 1):
        @pl.when(active & (quarters_ref[i] == nq))
        def _(nq=nq):
            ffn(slice(0, nq * bm // MOE_QUARTERS))

    @pl.when(active & (last_ref[i] == 1) & has_next)
    def _():
        for cp in weight_copies(nxt_ref[i]):
            cp.wait()
        cast_stage_into(1 - slot_ref[i])


def _expert_schedule(counts, padded, n_blocks, bm):
    pend = jnp.cumsum(padded)
    ids = jnp.arange(N_EXPERTS, dtype=jnp.int32)
    idx = jnp.arange(n_blocks, dtype=jnp.int32)
    n_used = (pend[-1] // bm).astype(jnp.int32)
    block_e = jnp.minimum(jnp.sum((pend[None, :] <= (idx * bm)[:, None]).astype(jnp.int32), axis=1), N_EXPERTS - 1)
    onehot = block_e[:, None] == ids[None, :]

    def per_block(table):
        return jnp.sum(jnp.where(onehot, table[None, :], 0), axis=1)

    start_blk = (pend - padded) // bm
    first = (idx == per_block(start_blk)) & (idx < n_used)
    last = (idx == per_block(pend // bm) - 1) & (idx < n_used)
    later = (ids[None, :] > ids[:, None]) & (padded[None, :] > 0)
    nxt_e = jnp.min(jnp.where(later, ids[None, :], N_EXPERTS), axis=1)
    nxt_e = jnp.where(nxt_e == N_EXPERTS, -1, nxt_e)
    order = jnp.cumsum((padded > 0).astype(jnp.int32)) - 1
    real_rows = per_block(counts) - (idx - per_block(start_blk)) * bm
    qrows = bm // MOE_QUARTERS
    quarters = jnp.clip((real_rows + qrows - 1) // qrows, 1, MOE_QUARTERS)
    as_i32 = lambda a: a.astype(jnp.int32)
    return (block_e, as_i32(first), as_i32(last), per_block(nxt_e), per_block(order) & 1, as_i32(quarters),
            n_used.reshape(1))


def _experts(schedule, xs, w_gate, b_gate, w_up, b_up, w_down, b_down):
    P, half = xs.shape
    bm = MOE_BM
    n_blocks = P // bm
    E, D, DE = w_gate.shape

    def row_map(i, be, first, last, nxt, slot, quarters, nu):
        return (jnp.minimum(i, nu[0] - 1), 0)

    def b_map(i, be, first, last, nxt, slot, quarters, nu):
        return (be[jnp.minimum(i, nu[0] - 1)], 0, 0)

    hbm = pl.BlockSpec(memory_space=pl.ANY)
    grid_spec = pltpu.PrefetchScalarGridSpec(
        num_scalar_prefetch=len(schedule),
        grid=(n_blocks,),
        in_specs=[pl.BlockSpec((bm, half), row_map),
                  hbm, pl.BlockSpec((1, 1, DE), b_map),
                  hbm, pl.BlockSpec((1, 1, DE), b_map),
                  hbm, pl.BlockSpec((1, 1, D), b_map)],
        out_specs=pl.BlockSpec((bm, half), row_map),
        scratch_shapes=[pltpu.VMEM((3, D, DE), F32),
                        pltpu.VMEM((2, 3, D, DE), BF16),
                        pltpu.SemaphoreType.DMA((3,))],
    )
    return pl.pallas_call(
        _expert_kernel,
        grid_spec=grid_spec,
        out_shape=jax.ShapeDtypeStruct((P, half), jnp.uint32),
        compiler_params=pltpu.CompilerParams(dimension_semantics=("arbitrary",), vmem_limit_bytes=VMEM_LIMIT),
        name="experts",
    )(*schedule, xs, w_gate, b_gate.reshape(E, 1, DE), w_up, b_up.reshape(E, 1, DE),
      w_down, b_down.reshape(E, 1, D))


def _combine_kernel(h1_ref, og_ref, probs_ref, mods_ref, g_ref, *rest):
    o_ref = rest[-1]
    tm = h1_ref.shape[1]
    probs = probs_ref[...]
    lane = lax.broadcasted_iota(jnp.int32, (tm, LANES), 1)
    half = og_ref.shape[2]
    f_lo = jnp.zeros((tm, half), F32)
    f_hi = jnp.zeros((tm, half), F32)
    for kk in range(TOP_K):
        pk = jnp.sum(jnp.where(lane == kk, probs, 0.0), axis=-1, keepdims=True)
        o_lo, o_hi = _unpack_rows(og_ref[kk])
        f_lo = f_lo + pk * o_lo
        f_hi = f_hi + pk * o_hi
    f = jnp.concatenate([f_lo, f_hi], axis=1)
    gate2 = mods_ref[0][5:6]
    o_ref[0] = h1_ref[0] + gate2 * (_rms(f) * g_ref[...])


def _combine(b0, b_total, h1, og, probs, mods, g_post_ffn, out_so_far):
    B, S, D = h1.shape
    tm = CMB_TM
    nj = S // tm
    in_specs = [pl.BlockSpec((1, tm, D), lambda b, j: (b, j, 0)),
                pl.BlockSpec((TOP_K, tm, D // 2), lambda b, j: (0, b * nj + j, 0)),
                pl.BlockSpec((tm, LANES), lambda b, j: (b * nj + j, 0)),
                pl.BlockSpec((1, 6, D), lambda b, j: (b + b0, 0, 0)),
                pl.BlockSpec((1, D), lambda b, j: (0, 0))]
    args = [h1, og, probs, mods, g_post_ffn]
    aliases = {}
    if out_so_far is not None:
        in_specs.append(pl.BlockSpec(memory_space=pl.ANY))
        args.append(out_so_far)
        aliases = {len(args) - 1: 0}
    return pl.pallas_call(
        _combine_kernel,
        grid=(B, nj),
        in_specs=in_specs,
        out_specs=pl.BlockSpec((1, tm, D), lambda b, j: (b + b0, j, 0)),
        out_shape=jax.ShapeDtypeStruct((b_total, S, D), F32),
        input_output_aliases=aliases,
        compiler_params=pltpu.CompilerParams(dimension_semantics=("arbitrary", "arbitrary"),
                                             vmem_limit_bytes=VMEM_LIMIT),
        name="combine",
    )(*args)


def _layer(h, c_pad, w_ada, b_ada, g_pre_mix, w_in, g_sgu_v, w_s, b_s, w_gk2, b_gk, g_gla_out, w_out, g_post_mix,
           g_pre_ffn, w_router, b_router, w_gate, b_gate, w_up, b_up, w_down, b_down, g_post_ffn):
    B, S, D = h.shape

    mods = _ada(c_pad, w_ada, b_ada.reshape(1, -1))[:B].reshape(B, 6, D)

    w_in_p = jnp.pad(w_in, ((0, 0), (0, D_IN_PAD - w_in.shape[1]))).astype(BF16)
    head_of = jnp.arange(D_SGU) // SGU_HEAD_DIM
    gmat = jnp.where(head_of[:, None] == head_of[None, :], 1.0 / SGU_HEAD_DIM, 0.0).astype(BF16)
    ws_cat = w_s.reshape(SGU_HEADS // 2, 2, SGU_CHUNK, SGU_CHUNK).transpose(0, 2, 1, 3).reshape(
        SGU_HEADS // 2, SGU_CHUNK, 2 * SGU_CHUNK)
    bs_exp = jnp.repeat(b_s.T, SGU_HEAD_DIM, axis=1)
    wgk_p = jnp.pad(w_gk2, ((0, LANES - GLA_GATE_RANK), (0, 0))).astype(BF16)
    w_r_p = jnp.pad(w_router, ((0, 0), (0, LANES - N_EXPERTS))).astype(BF16)
    b_r_p = jnp.concatenate([b_router, jnp.full((LANES - N_EXPERTS,), NEG_BIG, F32)]).reshape(1, LANES)

    w_out_b = w_out.astype(BF16)

    last = max(1, B // LAST_GROUP_FRACTION)
    group_sizes = [B - last, last] if B > 1 else [B]
    result = None
    b0 = 0
    for bg in group_sizes:
        tg = bg * S
        h1, n2, route, probs, cnt = _mixer(
            b0, bg, h, mods, g_pre_mix.reshape(1, D), w_in_p, gmat, g_sgu_v.reshape(1, D_SGU), ws_cat, bs_exp, wgk_p,
            b_gk.reshape(1, GLA_DK), g_gla_out.reshape(1, GLA_HEAD_V), w_out_b, g_post_mix.reshape(1, D),
            g_pre_ffn.reshape(1, D), w_r_p, b_r_p)

        bm = MOE_BM
        counts = cnt[0, :N_EXPERTS].astype(jnp.int32)
        padded = ((counts + bm - 1) // bm) * bm
        pend = jnp.cumsum(padded)
        pstart = pend - padded
        e_kt, rank_kt = route[:TOP_K], route[TOP_K:]
        onehot = e_kt[:, :, None] == jnp.arange(N_EXPERTS, dtype=jnp.int32)[None, None, :]
        dest_kt = rank_kt + jnp.sum(jnp.where(onehot, pstart[None, None, :], 0), axis=-1)
        n_blocks = tg * TOP_K // bm + N_EXPERTS
        schedule = _expert_schedule(counts, padded, n_blocks, bm)

        xs = _sc_dispatch(n2, dest_kt, n_blocks * bm)
        out = _experts(schedule, xs, w_gate, b_gate, w_up, b_up, w_down, b_down)
        og = _sc_gather(out, dest_kt).reshape(TOP_K, tg, D // 2)
        result = _combine(b0, B, h1, og, probs, mods, g_post_ffn.reshape(1, D), result)
        b0 += bg
    return result


def kernel(x, c, w_ada, b_ada, g_pre_mix, w_in, g_sgu_v, w_s, b_s, w_gk2, b_gk, g_gla_out, w_out, g_post_mix, g_pre_ffn, w_router, b_router, w_gate, b_gate, w_up, b_up, w_down, b_down, g_post_ffn):
    B = x.shape[0]
    c_pad = jnp.pad(c, ((0, -B % 8), (0, 0)))
    h = x
    for l in range(w_ada.shape[0]):
        h = _layer(h, c_pad, w_ada[l], b_ada[l], g_pre_mix[l], w_in[l], g_sgu_v[l], w_s[l], b_s[l], w_gk2[l], b_gk[l],
                   g_gla_out[l], w_out[l], g_post_mix[l], g_pre_ffn[l], w_router[l], b_router[l], w_gate[l], b_gate[l],
                   w_up[l], b_up[l], w_down[l], b_down[l], g_post_ffn[l])
    return h
```

```python
import functools

import jax
import jax.numpy as jnp
from jax import lax
from jax.experimental import pallas as pl
from jax.experimental.pallas import tpu as pltpu
from jax.experimental.pallas import tpu_sc as plsc

F32 = jnp.float32
BF16 = jnp.bfloat16

D_MODEL = 1024
D_SGU = 512
SGU_HEADS = 8
SGU_HEAD_DIM = 64
SGU_CHUNK = 128
D_GLA = 512
GLA_HEADS = 4
GLA_DK = 256
GLA_HEAD_K = 64
GLA_HEAD_V = 128
GLA_GATE_RANK = 16
GLA_TAU = 16.0
GLA_CHUNK = 64
N_EXPERTS = 32
TOP_K = 4
SWIGLU_LIMIT = 7.0
SWIGLU_ALPHA = 1.702
EPS = 1e-6

LANES = 128
D_IN_MAIN = 2 * D_SGU + 2 * GLA_DK + 2 * D_GLA
D_IN_PAD = D_IN_MAIN + LANES

MIX_TM = 1024
MIX_TILE = 512
MIX_SKEW = 1
MIX_SUB = 256
MOE_BM = 1024
MOE_QUARTERS = 8
CMB_TM = 1024
LAST_GROUP_FRACTION = 2
SC_CORES = 2
SC_SUBCORES = 16
SC_WORKERS = SC_CORES * SC_SUBCORES
SC_CH = 64
NEG_BIG = -1e30

VMEM_LIMIT = 56 * 1024 * 1024


def _ada_kernel(c_ref, w_ref, b_ref, o_ref):
    c = c_ref[...]
    sc = c * jax.nn.sigmoid(c)
    o_ref[...] = jnp.dot(sc.astype(BF16), w_ref[...].astype(BF16), preferred_element_type=F32) + b_ref[...]


def _ada(c_pad, w_ada, b_ada):
    rows, d = c_pad.shape
    n = w_ada.shape[1]
    tn = 1536
    return pl.pallas_call(
        _ada_kernel,
        grid=(n // tn,),
        in_specs=[pl.BlockSpec((rows, d), lambda j: (0, 0)),
                  pl.BlockSpec((d, tn), lambda j: (0, j)),
                  pl.BlockSpec((1, tn), lambda j: (0, j))],
        out_specs=pl.BlockSpec((rows, tn), lambda j: (0, j)),
        out_shape=jax.ShapeDtypeStruct((rows, n), F32),
        compiler_params=pltpu.CompilerParams(dimension_semantics=("arbitrary",), vmem_limit_bytes=VMEM_LIMIT),
        name="ada",
    )(c_pad, w_ada, b_ada)


def _rms(x):
    return x * lax.rsqrt(jnp.mean(x * x, axis=-1, keepdims=True) + EPS)


def _gelu_tanh(x):
    return 0.5 * x * (1.0 + jnp.tanh(0.7978845608028654 * (x + 0.044715 * (x * x * x))))


def _log_sigmoid(z):
    return jnp.minimum(z, 0.0) - jnp.log(1.0 + jnp.exp(-jnp.abs(z)))


def _dot(a, b):
    return jnp.dot(a, b, preferred_element_type=F32)


def _dot_nt(a, b):
    return lax.dot_general(a, b, (((1,), (1,)), ((), ())), preferred_element_type=F32)


def _dot_tn(a, b):
    return lax.dot_general(a, b, (((0,), (0,)), ((), ())), preferred_element_type=F32)


def _pack_rows(x):
    n = x.shape[1] // 2
    lo = lax.bitcast_convert_type(x[:, :n].astype(BF16).astype(F32), jnp.uint32)
    hi = lax.bitcast_convert_type(x[:, n:].astype(BF16).astype(F32), jnp.uint32)
    return (lo >> 16) | (hi & jnp.uint32(0xFFFF0000))


def _unpack_rows(p):
    lo = lax.bitcast_convert_type(p << 16, F32)
    hi = lax.bitcast_convert_type(p & jnp.uint32(0xFFFF0000), F32)
    return lo, hi


def _split_dot(l_bf, a):
    hi = a.astype(BF16)
    lo = (a - hi.astype(F32)).astype(BF16)
    return _dot(l_bf, hi) + _dot(l_bf, lo)


def _mixer_kernel(x_ref, mods_ref, g_pre_ref, w_in_ref, gmat_ref, g_sgu_ref, ws_ref, bs_ref, wgk_ref, bgk_ref,
                  g_gla_ref, w_out_ref, g_post_ref, g_ffn_ref, w_r_ref, b_r_ref,
                  h1_ref, n2_ref, route_ref, probs_ref, cnt_ref,
                  st_ref, carry_ref):
    b = pl.program_id(0)
    j = pl.program_id(1)

    @pl.when(j == 0)
    def _():
        st_ref[...] = jnp.zeros_like(st_ref)

    @pl.when((b == 0) & (j == 0))
    def _():
        carry_ref[...] = jnp.zeros_like(carry_ref)

    carry_box = [carry_ref[...]]
    tiles = [_mixer_tile(slice(s * MIX_TILE, (s + 1) * MIX_TILE), carry_box,
                         x_ref, mods_ref, g_pre_ref, w_in_ref, gmat_ref, g_sgu_ref, ws_ref, bs_ref, wgk_ref, bgk_ref,
                         g_gla_ref, w_out_ref, g_post_ref, g_ffn_ref, w_r_ref, b_r_ref,
                         h1_ref, n2_ref, route_ref, probs_ref, st_ref)
             for s in range(x_ref.shape[1] // MIX_TILE)]
    alive = [True] * len(tiles)
    t = 0
    while any(alive):
        for s, tile in enumerate(tiles):
            if alive[s] and t >= s * MIX_SKEW:
                alive[s] = next(tile, None) is not None
        t += 1
    carry_ref[...] = carry_box[0]
    cnt_ref[...] = jnp.broadcast_to(carry_box[0], cnt_ref.shape)


def _mixer_tile(rows, carry_box, x_ref, mods_ref, g_pre_ref, w_in_ref, gmat_ref, g_sgu_ref, ws_ref, bs_ref, wgk_ref,
                bgk_ref, g_gla_ref, w_out_ref, g_post_ref, g_ffn_ref, w_r_ref, b_r_ref,
                h1_ref, n2_ref, route_ref, probs_ref, st_ref):
    tm = rows.stop - rows.start
    mods = mods_ref[0]
    shift1, scale1, gate1 = mods[0:1], mods[1:2], mods[2:3]
    shift2, scale2 = mods[3:4], mods[4:5]

    x = x_ref[0, rows, :]
    n = _rms(x) * (g_pre_ref[...] * (1.0 + scale1)) + shift1
    nb = n.astype(BF16)
    yield True

    pu = _dot(nb, w_in_ref[:, 0:D_SGU])
    pv = _dot(nb, w_in_ref[:, D_SGU:2 * D_SGU])
    yield True

    u = _gelu_tanh(pu)
    v = _gelu_tanh(pv)
    msv = _dot((v * v).astype(BF16), gmat_ref[...])
    vh = v * lax.rsqrt(msv + EPS) * g_sgu_ref[...]
    yield True

    qk = _dot(nb, w_in_ref[:, 2 * D_SGU:2 * D_SGU + 2 * GLA_DK])
    vv = _dot(nb, w_in_ref[:, 2 * D_SGU + 2 * GLA_DK:2 * D_SGU + 2 * GLA_DK + D_GLA]).astype(BF16)
    r = _dot(nb, w_in_ref[:, D_IN_MAIN - D_GLA:D_IN_MAIN])
    g_low = _dot(nb, w_in_ref[:, D_IN_MAIN:D_IN_PAD])
    yield True

    n_sc = tm // SGU_CHUNK
    lane_c = lax.broadcasted_iota(jnp.int32, (SGU_CHUNK, LANES), 1)
    low_half = lane_c < SGU_HEAD_DIM
    w_row = lax.broadcasted_iota(jnp.int32, (SGU_CHUNK, 2 * SGU_CHUNK), 0)
    w_col = lax.broadcasted_iota(jnp.int32, (SGU_CHUNK, 2 * SGU_CHUNK), 1) & (SGU_CHUNK - 1)
    a_cols = []
    for p in range(SGU_HEADS // 2):
        wcat = jnp.where(w_row >= w_col, ws_ref[p], 0.0).astype(BF16)
        rhs = []
        for ci in range(n_sc):
            vp = vh[ci * SGU_CHUNK:(ci + 1) * SGU_CHUNK, p * LANES:(p + 1) * LANES]
            rhs.append(jnp.concatenate([jnp.where(low_half, vp, 0.0), jnp.where(low_half, 0.0, vp)],
                                       axis=0).astype(BF16))
        res = _dot(wcat, jnp.concatenate(rhs, axis=1))
        bias = bs_ref[:, p * LANES:(p + 1) * LANES]
        a_cols.append(jnp.concatenate(
            [res[:, ci * LANES:(ci + 1) * LANES] + bias for ci in range(n_sc)], axis=0))
    a_out = u * jnp.concatenate(a_cols, axis=1)

    q = qk[:, :GLA_DK] * (GLA_HEAD_K ** -0.5)
    k = qk[:, GLA_DK:]
    z = _dot(g_low.astype(BF16), wgk_ref[...]) + bgk_ref[...]
    log_a = _log_sigmoid(z) * (1.0 / GLA_TAU)

    sub = min(tm, MIX_SUB)
    subs = [slice(s * sub, (s + 1) * sub) for s in range(tm // sub)]
    t_row = lax.broadcasted_iota(jnp.int32, (sub, sub), 0)
    t_col = lax.broadcasted_iota(jnp.int32, (sub, sub), 1)
    same_chunk = (t_row >> 6) == (t_col >> 6)
    l_cum = jnp.where(same_chunk & (t_row >= t_col), 1.0, 0.0).astype(BF16)
    l_all = jnp.where(same_chunk, 1.0, 0.0).astype(BF16)
    bcum = jnp.concatenate([_split_dot(l_cum, log_a[s]) for s in subs], axis=0)
    blast = jnp.concatenate([_split_dot(l_all, log_a[s]) for s in subs], axis=0)
    q_s = q * jnp.exp(bcum)
    k_s = (k * jnp.exp(-bcum)).astype(BF16)
    k_dec = (k * jnp.exp(blast - bcum)).astype(BF16)
    decay = jnp.exp(blast)
    yield True

    lane_g = lax.broadcasted_iota(jnp.int32, (GLA_CHUNK, LANES), 1)
    causal = (lax.broadcasted_iota(jnp.int32, (GLA_CHUNK, GLA_CHUNK), 0)
              >= lax.broadcasted_iota(jnp.int32, (GLA_CHUNK, GLA_CHUNK), 1))
    g_gla = g_gla_ref[...]
    o_rows = []
    for ci in range(tm // GLA_CHUNK):
        rs = slice(ci * GLA_CHUNK, (ci + 1) * GLA_CHUNK)
        o_heads = []
        for h in range(GLA_HEADS):
            pl_ = slice((h // 2) * LANES, (h // 2 + 1) * LANES)
            own = (lane_g >= GLA_HEAD_K) if (h % 2) else (lane_g < GLA_HEAD_K)
            qm = jnp.where(own, q_s[rs, pl_], 0.0).astype(BF16)
            v_h = vv[rs, h * GLA_HEAD_V:(h + 1) * GLA_HEAD_V]
            scores = jnp.where(causal, _dot_nt(qm, k_s[rs, pl_]), 0.0)
            intra = _dot(scores.astype(BF16), v_h)
            st = st_ref[h]
            inter = _dot_nt(qm, st.astype(BF16))
            st_ref[h] = st * decay[ci * GLA_CHUNK:ci * GLA_CHUNK + 1, pl_] + _dot_tn(v_h, k_dec[rs, pl_])
            o_heads.append(_rms(intra + inter) * g_gla)
        o_rows.append(jnp.concatenate(o_heads, axis=1))
    o = jnp.concatenate(o_rows, axis=0) * (r * jax.nn.sigmoid(r))
    mix_in = jnp.concatenate([a_out, o], axis=1).astype(BF16)
    yield True

    mix = _dot(mix_in, w_out_ref[...])
    yield True

    h1 = x + gate1 * (_rms(mix) * g_post_ref[...])
    h1_ref[0, rows, :] = h1
    n2 = _rms(h1) * (g_ffn_ref[...] * (1.0 + scale2)) + shift2
    n2_ref[rows, :] = _pack_rows(n2)
    n2b = n2.astype(BF16)
    yield True

    logits = _dot(n2b, w_r_ref[...]) + b_r_ref[...]
    lane = lax.broadcasted_iota(jnp.int32, (tm, LANES), 1)
    lane_f = lane.astype(F32)
    vals = logits
    sels, tops, idxs = [], [], []
    for _ in range(TOP_K):
        m = jnp.max(vals, axis=-1, keepdims=True)
        idx = jnp.min(jnp.where(vals == m, lane_f, float(LANES)), axis=-1, keepdims=True)
        sel = lane_f == idx
        sels.append(sel)
        tops.append(m)
        idxs.append(idx)
        vals = jnp.where(sel, -jnp.inf, vals)
    es = [jnp.exp(t - tops[0]) for t in tops]
    inv = 1.0 / (es[0] + es[1] + es[2] + es[3])

    multi = jnp.zeros((tm, LANES), F32)
    for sel in sels:
        multi = multi + jnp.where(sel, 1.0, 0.0)
    l_strict = jnp.where(t_row > t_col, 1.0, 0.0).astype(BF16)
    carry = carry_box[0]
    before = []
    for s in subs:
        before.append(_dot(l_strict, multi[s].astype(BF16)) + carry)
        carry = carry + jnp.sum(multi[s], axis=0, keepdims=True)
    before = jnp.concatenate(before, axis=0)
    carry_box[0] = carry

    route = jnp.zeros((tm, LANES), F32)
    probs = jnp.zeros((tm, LANES), F32)
    for kk in range(TOP_K):
        rank_kk = jnp.sum(jnp.where(sels[kk], before, 0.0), axis=-1, keepdims=True)
        route = jnp.where(lane == kk, idxs[kk], route)
        route = jnp.where(lane == TOP_K + kk, rank_kk, route)
        probs = jnp.where(lane == kk, es[kk] * inv, probs)
    route_ref[:, rows] = route.T[0:2 * TOP_K, :].astype(jnp.int32)
    probs_ref[rows, :] = probs


def _const_spec(shape):
    return pl.BlockSpec(shape, lambda b, j: (0,) * len(shape))


def _mixer(b0, B, x, mods, g_pre, w_in_p, gmat, g_sgu, ws_cat, bs_exp, wgk_p, bgk, g_gla, w_out_b, g_post, g_ffn, w_r_p,
           b_r_p):
    _, S, D = x.shape
    T = B * S
    tm = MIX_TM
    nj = S // tm
    return pl.pallas_call(
        _mixer_kernel,
        grid=(B, nj),
        in_specs=[pl.BlockSpec((1, tm, D), lambda b, j: (b + b0, j, 0)),
                  pl.BlockSpec((1, 6, D), lambda b, j: (b + b0, 0, 0)),
                  _const_spec((1, D)), _const_spec(w_in_p.shape), _const_spec(gmat.shape), _const_spec((1, D_SGU)),
                  _const_spec(ws_cat.shape), _const_spec(bs_exp.shape), _const_spec(wgk_p.shape),
                  _const_spec((1, GLA_DK)), _const_spec((1, GLA_HEAD_V)), _const_spec(w_out_b.shape),
                  _const_spec((1, D)), _const_spec((1, D)), _const_spec(w_r_p.shape), _const_spec((1, LANES))],
        out_specs=[pl.BlockSpec((1, tm, D), lambda b, j: (b, j, 0)),
                   pl.BlockSpec((tm, D // 2), lambda b, j: (b * nj + j, 0)),
                   pl.BlockSpec((2 * TOP_K, tm), lambda b, j: (0, b * nj + j)),
                   pl.BlockSpec((tm, LANES), lambda b, j: (b * nj + j, 0)),
                   pl.BlockSpec((8, LANES), lambda b, j: (0, 0))],
        out_shape=[jax.ShapeDtypeStruct((B, S, D), F32),
                   jax.ShapeDtypeStruct((T, D // 2), jnp.uint32),
                   jax.ShapeDtypeStruct((2 * TOP_K, T), jnp.int32),
                   jax.ShapeDtypeStruct((T, LANES), F32),
                   jax.ShapeDtypeStruct((8, LANES), F32)],
        scratch_shapes=[pltpu.VMEM((GLA_HEADS, GLA_HEAD_V, LANES), F32),
                        pltpu.VMEM((1, LANES), F32)],
        compiler_params=pltpu.CompilerParams(dimension_semantics=("arbitrary", "arbitrary"),
                                             vmem_limit_bytes=VMEM_LIMIT),
        name="mixer",
    )(x, mods, g_pre, w_in_p, gmat, g_sgu, ws_cat, bs_exp, wgk_p, bgk, g_gla, w_out_b, g_post, g_ffn, w_r_p, b_r_p)


def _sc_dispatch(rows, dest_kt, n_out):
    T, D = rows.shape
    top_k = dest_kt.shape[0]
    ch = SC_CH
    cpw = T // ch // SC_WORKERS
    mesh = plsc.VectorSubcoreMesh(core_axis_name="c", subcore_axis_name="s")

    @functools.partial(
        pl.kernel, mesh=mesh,
        out_type=jax.ShapeDtypeStruct((n_out, D), rows.dtype),
        scratch_types=[pltpu.VMEM((top_k, ch), jnp.int32), pltpu.VMEM((ch, D), rows.dtype)],
    )
    def k(rows_hbm, dest_hbm, out_hbm, idx_v, rows_v):
        wid = lax.axis_index("s") * SC_CORES + lax.axis_index("c")

        @pl.loop(0, cpw)
        def _(i):
            t0 = (wid * cpw + i) * ch
            for kk in range(top_k):
                pltpu.sync_copy(dest_hbm.at[kk, pl.ds(t0, ch)], idx_v.at[kk])
            pltpu.sync_copy(rows_hbm.at[pl.ds(t0, ch)], rows_v)
            for kk in range(top_k):
                pltpu.sync_copy(rows_v, out_hbm.at[idx_v.at[kk]])

    return k(rows, dest_kt)


def _sc_gather(src, dest_kt):
    _, D = src.shape
    top_k, T = dest_kt.shape
    ch = SC_CH
    cpk = T // ch
    cpw = top_k * cpk // SC_WORKERS
    mesh = plsc.VectorSubcoreMesh(core_axis_name="c", subcore_axis_name="s")

    assert cpw % 2 == 0
    n_pairs = cpw // 2

    @functools.partial(
        pl.kernel, mesh=mesh,
        out_type=jax.ShapeDtypeStruct((top_k * T, D), src.dtype),
        scratch_types=[pltpu.VMEM((2, ch), jnp.int32), pltpu.VMEM((2, ch, D), src.dtype),
                       pltpu.SemaphoreType.DMA, pltpu.SemaphoreType.DMA,
                       pltpu.SemaphoreType.DMA, pltpu.SemaphoreType.DMA],
    )
    def k(src_hbm, idx_hbm, out_hbm, idx_v, rows_v, gsem0, gsem1, wsem0, wsem1):
        wid = lax.axis_index("s") * SC_CORES + lax.axis_index("c")
        gsem = (gsem0, gsem1)
        wsem = (wsem0, wsem1)

        def gather_copy(b):
            return pltpu.make_async_copy(src_hbm.at[idx_v.at[b]], rows_v.at[b], gsem[b])

        def write_copy(c, b):
            return pltpu.make_async_copy(rows_v.at[b], out_hbm.at[pl.ds(c * ch, ch)], wsem[b])

        def start_gather(c, b):
            kk = c // cpk
            t0 = (c - kk * cpk) * ch
            pltpu.sync_copy(idx_hbm.at[kk, pl.ds(t0, ch)], idx_v.at[b])
            gather_copy(b).start()

        c_first = wid * cpw
        start_gather(c_first, 0)

        @pl.loop(0, n_pairs)
        def _(j):
            c0 = c_first + 2 * j
            c1 = c0 + 1

            @pl.when(j > 0)
            def _():
                write_copy(c0 - 1, 1).wait()

            start_gather(c1, 1)
            gather_copy(0).wait()
            write_copy(c0, 0).start()
            write_copy(c0, 0).wait()

            @pl.when(j < n_pairs - 1)
            def _():
                start_gather(c0 + 2, 0)

            gather_copy(1).wait()
            write_copy(c1, 1).start()

        write_copy(c_first + cpw - 1, 1).wait()

    return k(src, dest_kt)


def _expert_kernel(be_ref, first_ref, last_ref, nxt_ref, slot_ref, quarters_ref, nu_ref,
                   x_ref, wg_hbm, bg_ref, wu_hbm, bu_ref, wd_hbm, bd_ref, o_ref,
                   stage_ref, wbf_ref, sem):
    i = pl.program_id(0)
    active = i < nu_ref[0]
    has_next = nxt_ref[i] >= 0

    def weight_copies(e):
        return [pltpu.make_async_copy(w.at[e], stage_ref.at[m], sem.at[m])
                for m, w in enumerate((wg_hbm, wu_hbm, wd_hbm))]

    def cast_stage_into(slot):
        for m in range(3):
            wbf_ref[slot, m] = stage_ref[m].astype(BF16)

    @pl.when(i == 0)
    def _():
        for cp in weight_copies(be_ref[0]):
            cp.start()
        for cp in weight_copies(be_ref[0]):
            cp.wait()
        cast_stage_into(0)

    @pl.when(active & (first_ref[i] == 1) & has_next)
    def _():
        for cp in weight_copies(nxt_ref[i]):
            cp.start()

    def ffn(rows):
        slot = slot_ref[i]
        x_lo, x_hi = _unpack_rows(x_ref[rows, :])
        xb = jnp.concatenate([x_lo, x_hi], axis=1).astype(BF16)
        g = jnp.minimum(_dot(xb, wbf_ref[slot, 0]) + bg_ref[0], SWIGLU_LIMIT)
        u = jnp.clip(_dot(xb, wbf_ref[slot, 1]) + bu_ref[0], -SWIGLU_LIMIT, SWIGLU_LIMIT)
        hdn = (u + 1.0) * (g * jax.nn.sigmoid(SWIGLU_ALPHA * g))
        o_ref[rows, :] = _pack_rows(_dot(hdn.astype(BF16), wbf_ref[slot, 2]) + bd_ref[0])

    bm = x_ref.shape[0]
    for nq in range(1, MOE_QUARTERS + 1):
        @pl.when(active & (quarters_ref[i] == nq))
        def _(nq=nq):
            ffn(slice(0, nq * bm // MOE_QUARTERS))

    @pl.when(active & (last_ref[i] == 1) & has_next)
    def _():
        for cp in weight_copies(nxt_ref[i]):
            cp.wait()
        cast_stage_into(1 - slot_ref[i])


def _expert_schedule(counts, padded, n_blocks, bm):
    pend = jnp.cumsum(padded)
    ids = jnp.arange(N_EXPERTS, dtype=jnp.int32)
    idx = jnp.arange(n_blocks, dtype=jnp.int32)
    n_used = (pend[-1] // bm).astype(jnp.int32)
    block_e = jnp.minimum(jnp.sum((pend[None, :] <= (idx * bm)[:, None]).astype(jnp.int32), axis=1), N_EXPERTS - 1)
    onehot = block_e[:, None] == ids[None, :]

    def per_block(table):
        return jnp.sum(jnp.where(onehot, table[None, :], 0), axis=1)

    start_blk = (pend - padded) // bm
    first = (idx == per_block(start_blk)) & (idx < n_used)
    last = (idx == per_block(pend // bm) - 1) & (idx < n_used)
    later = (ids[None, :] > ids[:, None]) & (padded[None, :] > 0)
    nxt_e = jnp.min(jnp.where(later, ids[None, :], N_EXPERTS), axis=1)
    nxt_e = jnp.where(nxt_e == N_EXPERTS, -1, nxt_e)
    order = jnp.cumsum((padded > 0).astype(jnp.int32)) - 1
    real_rows = per_block(counts) - (idx - per_block(start_blk)) * bm
    qrows = bm // MOE_QUARTERS
    quarters = jnp.clip((real_rows + qrows - 1) // qrows, 1, MOE_QUARTERS)
    as_i32 = lambda a: a.astype(jnp.int32)
    return (block_e, as_i32(first), as_i32(last), per_block(nxt_e), per_block(order) & 1, as_i32(quarters),
            n_used.reshape(1))


def _experts(schedule, xs, w_gate, b_gate, w_up, b_up, w_down, b_down):
    P, half = xs.shape
    bm = MOE_BM
    n_blocks = P // bm
    E, D, DE = w_gate.shape

    def row_map(i, be, first, last, nxt, slot, quarters, nu):
        return (jnp.minimum(i, nu[0] - 1), 0)

    def b_map(i, be, first, last, nxt, slot, quarters, nu):
        return (be[jnp.minimum(i, nu[0] - 1)], 0, 0)

    hbm = pl.BlockSpec(memory_space=pl.ANY)
    grid_spec = pltpu.PrefetchScalarGridSpec(
        num_scalar_prefetch=len(schedule),
        grid=(n_blocks,),
        in_specs=[pl.BlockSpec((bm, half), row_map),
                  hbm, pl.BlockSpec((1, 1, DE), b_map),
                  hbm, pl.BlockSpec((1, 1, DE), b_map),
                  hbm, pl.BlockSpec((1, 1, D), b_map)],
        out_specs=pl.BlockSpec((bm, half), row_map),
        scratch_shapes=[pltpu.VMEM((3, D, DE), F32),
                        pltpu.VMEM((2, 3, D, DE), BF16),
                        pltpu.SemaphoreType.DMA((3,))],
    )
    return pl.pallas_call(
        _expert_kernel,
        grid_spec=grid_spec,
        out_shape=jax.ShapeDtypeStruct((P, half), jnp.uint32),
        compiler_params=pltpu.CompilerParams(dimension_semantics=("arbitrary",), vmem_limit_bytes=VMEM_LIMIT),
        name="experts",
    )(*schedule, xs, w_gate, b_gate.reshape(E, 1, DE), w_up, b_up.reshape(E, 1, DE),
      w_down, b_down.reshape(E, 1, D))


def _combine_kernel(h1_ref, og_ref, probs_ref, mods_ref, g_ref, *rest):
    o_ref = rest[-1]
    tm = h1_ref.shape[1]
    probs = probs_ref[...]
    lane = lax.broadcasted_iota(jnp.int32, (tm, LANES), 1)
    half = og_ref.shape[2]
    f_lo = jnp.zeros((tm, half), F32)
    f_hi = jnp.zeros((tm, half), F32)
    for kk in range(TOP_K):
        pk = jnp.sum(jnp.where(lane == kk, probs, 0.0), axis=-1, keepdims=True)
        o_lo, o_hi = _unpack_rows(og_ref[kk])
        f_lo = f_lo + pk * o_lo
        f_hi = f_hi + pk * o_hi
    f = jnp.concatenate([f_lo, f_hi], axis=1)
    gate2 = mods_ref[0][5:6]
    o_ref[0] = h1_ref[0] + gate2 * (_rms(f) * g_ref[...])


def _combine(b0, b_total, h1, og, probs, mods, g_post_ffn, out_so_far):
    B, S, D = h1.shape
    tm = CMB_TM
    nj = S // tm
    in_specs = [pl.BlockSpec((1, tm, D), lambda b, j: (b, j, 0)),
                pl.BlockSpec((TOP_K, tm, D // 2), lambda b, j: (0, b * nj + j, 0)),
                pl.BlockSpec((tm, LANES), lambda b, j: (b * nj + j, 0)),
                pl.BlockSpec((1, 6, D), lambda b, j: (b + b0, 0, 0)),
                pl.BlockSpec((1, D), lambda b, j: (0, 0))]
    args = [h1, og, probs, mods, g_post_ffn]
    aliases = {}
    if out_so_far is not None:
        in_specs.append(pl.BlockSpec(memory_space=pl.ANY))
        args.append(out_so_far)
        aliases = {len(args) - 1: 0}
    return pl.pallas_call(
        _combine_kernel,
        grid=(B, nj),
        in_specs=in_specs,
        out_specs=pl.BlockSpec((1, tm, D), lambda b, j: (b + b0, j, 0)),
        out_shape=jax.ShapeDtypeStruct((b_total, S, D), F32),
        input_output_aliases=aliases,
        compiler_params=pltpu.CompilerParams(dimension_semantics=("arbitrary", "arbitrary"),
                                             vmem_limit_bytes=VMEM_LIMIT),
        name="combine",
    )(*args)


def _layer(h, c_pad, w_ada, b_ada, g_pre_mix, w_in, g_sgu_v, w_s, b_s, w_gk2, b_gk, g_gla_out, w_out, g_post_mix,
           g_pre_ffn, w_router, b_router, w_gate, b_gate, w_up, b_up, w_down, b_down, g_post_ffn):
    B, S, D = h.shape

    mods = _ada(c_pad, w_ada, b_ada.reshape(1, -1))[:B].reshape(B, 6, D)

    w_in_p = jnp.pad(w_in, ((0, 0), (0, D_IN_PAD - w_in.shape[1]))).astype(BF16)
    head_of = jnp.arange(D_SGU) // SGU_HEAD_DIM
    gmat = jnp.where(head_of[:, None] == head_of[None, :], 1.0 / SGU_HEAD_DIM, 0.0).astype(BF16)
    ws_cat = w_s.reshape(SGU_HEADS // 2, 2, SGU_CHUNK, SGU_CHUNK).transpose(0, 2, 1, 3).reshape(
        SGU_HEADS // 2, SGU_CHUNK, 2 * SGU_CHUNK)
    bs_exp = jnp.repeat(b_s.T, SGU_HEAD_DIM, axis=1)
    wgk_p = jnp.pad(w_gk2, ((0, LANES - GLA_GATE_RANK), (0, 0))).astype(BF16)
    w_r_p = jnp.pad(w_router, ((0, 0), (0, LANES - N_EXPERTS))).astype(BF16)
    b_r_p = jnp.concatenate([b_router, jnp.full((LANES - N_EXPERTS,), NEG_BIG, F32)]).reshape(1, LANES)

    w_out_b = w_out.astype(BF16)

    last = max(1, B // LAST_GROUP_FRACTION)
    group_sizes = [B - last, last] if B > 1 else [B]
    result = None
    b0 = 0
    for bg in group_sizes:
        tg = bg * S
        h1, n2, route, probs, cnt = _mixer(
            b0, bg, h, mods, g_pre_mix.reshape(1, D), w_in_p, gmat, g_sgu_v.reshape(1, D_SGU), ws_cat, bs_exp, wgk_p,
            b_gk.reshape(1, GLA_DK), g_gla_out.reshape(1, GLA_HEAD_V), w_out_b, g_post_mix.reshape(1, D),
            g_pre_ffn.reshape(1, D), w_r_p, b_r_p)

        bm = MOE_BM
        counts = cnt[0, :N_EXPERTS].astype(jnp.int32)
        padded = ((counts + bm - 1) // bm) * bm
        pend = jnp.cumsum(padded)
        pstart = pend - padded
        e_kt, rank_kt = route[:TOP_K], route[TOP_K:]
        onehot = e_kt[:, :, None] == jnp.arange(N_EXPERTS, dtype=jnp.int32)[None, None, :]
        dest_kt = rank_kt + jnp.sum(jnp.where(onehot, pstart[None, None, :], 0), axis=-1)
        n_blocks = tg * TOP_K // bm + N_EXPERTS
        schedule = _expert_schedule(counts, padded, n_blocks, bm)

        xs = _sc_dispatch(n2, dest_kt, n_blocks * bm)
        out = _experts(schedule, xs, w_gate, b_gate, w_up, b_up, w_down, b_down)
        og = _sc_gather(out, dest_kt).reshape(TOP_K, tg, D // 2)
        result = _combine(b0, B, h1, og, probs, mods, g_post_ffn.reshape(1, D), result)
        b0 += bg
    return result


def kernel(x, c, w_ada, b_ada, g_pre_mix, w_in, g_sgu_v, w_s, b_s, w_gk2, b_gk, g_gla_out, w_out, g_post_mix, g_pre_ffn, w_router, b_router, w_gate, b_gate, w_up, b_up, w_down, b_down, g_post_ffn):
    B = x.shape[0]
    c_pad = jnp.pad(c, ((0, -B % 8), (0, 0)))
    h = x
    for l in range(w_ada.shape[0]):
        h = _layer(h, c_pad, w_ada[l], b_ada[l], g_pre_mix[l], w_in[l], g_sgu_v[l], w_s[l], b_s[l], w_gk2[l], b_gk[l],
                   g_gla_out[l], w_out[l], g_post_mix[l], g_pre_ffn[l], w_router[l], b_router[l], w_gate[l], b_gate[l],
                   w_up[l], b_up[l], w_down[l], b_down[l], g_post_ffn[l])
    return h
```

```python
import functools

import jax
import jax.numpy as jnp
from jax import lax
from jax.experimental import pallas as pl
from jax.experimental.pallas import tpu as pltpu
from jax.experimental.pallas import tpu_sc as plsc

F32 = jnp.float32
BF16 = jnp.bfloat16

D_MODEL = 1024
D_SGU = 512
SGU_HEADS = 8
SGU_HEAD_DIM = 64
SGU_CHUNK = 128
D_GLA = 512
GLA_HEADS = 4
GLA_DK = 256
GLA_HEAD_K = 64
GLA_HEAD_V = 128
GLA_GATE_RANK = 16
GLA_TAU = 16.0
GLA_CHUNK = 64
N_EXPERTS = 32
TOP_K = 4
SWIGLU_LIMIT = 7.0
SWIGLU_ALPHA = 1.702
EPS = 1e-6

LANES = 128
D_IN_MAIN = 2 * D_SGU + 2 * GLA_DK + 2 * D_GLA
D_IN_PAD = D_IN_MAIN + LANES

MIX_TM = 1024
MIX_TILE = 512
MIX_SKEW = 1
MIX_SUB = 256
MOE_BM = 1024
MOE_QUARTERS = 4
CMB_TM = 1024
LAST_GROUP_FRACTION = 2
TAIL_CHUNKS = 4
SC_CORES = 2
SC_SUBCORES = 16
SC_WORKERS = SC_CORES * SC_SUBCORES
SC_CH = 64
ADA_TN = 1536
ROUTER_PAD = -jnp.inf

VMEM_LIMIT = 56 * 1024 * 1024


def _ada_kernel(c_ref, w_ref, b_ref, o_ref):
    c = c_ref[...]
    sc = c * jax.nn.sigmoid(c)
    o_ref[...] = jnp.dot(sc.astype(BF16), w_ref[...].astype(BF16), preferred_element_type=F32) + b_ref[...]


def _ada(c_pad, w_ada, b_ada):
    rows, d = c_pad.shape
    n = w_ada.shape[1]
    tn = ADA_TN
    return pl.pallas_call(
        _ada_kernel,
        grid=(n // tn,),
        in_specs=[pl.BlockSpec((rows, d), lambda j: (0, 0)),
                  pl.BlockSpec((d, tn), lambda j: (0, j)),
                  pl.BlockSpec((1, tn), lambda j: (0, j))],
        out_specs=pl.BlockSpec((rows, tn), lambda j: (0, j)),
        out_shape=jax.ShapeDtypeStruct((rows, n), F32),
        compiler_params=pltpu.CompilerParams(dimension_semantics=("arbitrary",), vmem_limit_bytes=VMEM_LIMIT),
        name="ada",
    )(c_pad, w_ada, b_ada)


def _rms(x):
    return x * lax.rsqrt(jnp.mean(x * x, axis=-1, keepdims=True) + EPS)


def _gelu_tanh(x):
    return 0.5 * x * (1.0 + jnp.tanh(0.7978845608028654 * (x + 0.044715 * (x * x * x))))


def _log_sigmoid(z):
    return jnp.minimum(z, 0.0) - jnp.log(1.0 + jnp.exp(-jnp.abs(z)))


def _dot(a, b):
    return jnp.dot(a, b, preferred_element_type=F32)


def _dot_nt(a, b):
    return lax.dot_general(a, b, (((1,), (1,)), ((), ())), preferred_element_type=F32)


def _dot_tn(a, b):
    return lax.dot_general(a, b, (((0,), (0,)), ((), ())), preferred_element_type=F32)


def _pack_rows(x):
    n = x.shape[1] // 2
    lo = lax.bitcast_convert_type(x[:, :n].astype(BF16).astype(F32), jnp.uint32)
    hi = lax.bitcast_convert_type(x[:, n:].astype(BF16).astype(F32), jnp.uint32)
    return (lo >> 16) | (hi & jnp.uint32(0xFFFF0000))


def _unpack_rows(p):
    lo = lax.bitcast_convert_type(p << 16, F32)
    hi = lax.bitcast_convert_type(p & jnp.uint32(0xFFFF0000), F32)
    return lo, hi


def _split_dot(l_bf, a):
    hi = a.astype(BF16)
    lo = (a - hi.astype(F32)).astype(BF16)
    return _dot(l_bf, hi) + _dot(l_bf, lo)


def _mixer_kernel(x_ref, mods_ref, g_pre_ref, w_in_ref, gmat_ref, g_sgu_ref, ws_ref, bs_ref, wgk_ref, bgk_ref,
                  g_gla_ref, w_out_ref, g_post_ref, g_ffn_ref, w_r_ref, b_r_ref,
                  h1_ref, n2_ref, route_ref, probs_ref, cnt_ref,
                  st_ref, carry_ref):
    b = pl.program_id(0)
    j = pl.program_id(1)

    @pl.when(j == 0)
    def _():
        st_ref[...] = jnp.zeros_like(st_ref)

    @pl.when((b == 0) & (j == 0))
    def _():
        carry_ref[...] = jnp.zeros_like(carry_ref)

    carry_box = [carry_ref[...]]
    tiles = [_mixer_tile(slice(s * MIX_TILE, (s + 1) * MIX_TILE), carry_box,
                         x_ref, mods_ref, g_pre_ref, w_in_ref, gmat_ref, g_sgu_ref, ws_ref, bs_ref, wgk_ref, bgk_ref,
                         g_gla_ref, w_out_ref, g_post_ref, g_ffn_ref, w_r_ref, b_r_ref,
                         h1_ref, n2_ref, route_ref, probs_ref, st_ref)
             for s in range(x_ref.shape[1] // MIX_TILE)]
    alive = [True] * len(tiles)
    t = 0
    while any(alive):
        for s, tile in enumerate(tiles):
            if alive[s] and t >= s * MIX_SKEW:
                alive[s] = next(tile, None) is not None
        t += 1
    carry_ref[...] = carry_box[0]
    cnt_ref[...] = jnp.broadcast_to(carry_box[0], cnt_ref.shape)


def _mixer_tile(rows, carry_box, x_ref, mods_ref, g_pre_ref, w_in_ref, gmat_ref, g_sgu_ref, ws_ref, bs_ref, wgk_ref,
                bgk_ref, g_gla_ref, w_out_ref, g_post_ref, g_ffn_ref, w_r_ref, b_r_ref,
                h1_ref, n2_ref, route_ref, probs_ref, st_ref):
    tm = rows.stop - rows.start
    mods = mods_ref[0]
    shift1, scale1, gate1 = mods[0:1], mods[1:2], mods[2:3]
    shift2, scale2 = mods[3:4], mods[4:5]

    x = x_ref[0, rows, :]
    n = _rms(x) * (g_pre_ref[...] * (1.0 + scale1)) + shift1
    nb = n.astype(BF16)
    yield True

    pu = _dot(nb, w_in_ref[:, 0:D_SGU])
    pv = _dot(nb, w_in_ref[:, D_SGU:2 * D_SGU])
    yield True

    u = _gelu_tanh(pu)
    v = _gelu_tanh(pv)
    msv = _dot((v * v).astype(BF16), gmat_ref[...])
    vh = v * lax.rsqrt(msv + EPS) * g_sgu_ref[...]
    yield True

    qk = _dot(nb, w_in_ref[:, 2 * D_SGU:2 * D_SGU + 2 * GLA_DK])
    vv = _dot(nb, w_in_ref[:, 2 * D_SGU + 2 * GLA_DK:2 * D_SGU + 2 * GLA_DK + D_GLA]).astype(BF16)
    r = _dot(nb, w_in_ref[:, D_IN_MAIN - D_GLA:D_IN_MAIN])
    g_low = _dot(nb, w_in_ref[:, D_IN_MAIN:D_IN_PAD])
    yield True

    n_sc = tm // SGU_CHUNK
    lane_c = lax.broadcasted_iota(jnp.int32, (SGU_CHUNK, LANES), 1)
    low_half = lane_c < SGU_HEAD_DIM
    w_row = lax.broadcasted_iota(jnp.int32, (SGU_CHUNK, 2 * SGU_CHUNK), 0)
    w_col = lax.broadcasted_iota(jnp.int32, (SGU_CHUNK, 2 * SGU_CHUNK), 1) & (SGU_CHUNK - 1)
    a_cols = []
    for p in range(SGU_HEADS // 2):
        wcat = jnp.where(w_row >= w_col, ws_ref[p], 0.0).astype(BF16)
        rhs = []
        for ci in range(n_sc):
            vp = vh[ci * SGU_CHUNK:(ci + 1) * SGU_CHUNK, p * LANES:(p + 1) * LANES]
            rhs.append(jnp.concatenate([jnp.where(low_half, vp, 0.0), jnp.where(low_half, 0.0, vp)],
                                       axis=0).astype(BF16))
        res = _dot(wcat, jnp.concatenate(rhs, axis=1))
        bias = bs_ref[:, p * LANES:(p + 1) * LANES]
        a_cols.append(jnp.concatenate(
            [res[:, ci * LANES:(ci + 1) * LANES] + bias for ci in range(n_sc)], axis=0))
    a_out = u * jnp.concatenate(a_cols, axis=1)

    q = qk[:, :GLA_DK] * (GLA_HEAD_K ** -0.5)
    k = qk[:, GLA_DK:]
    z = _dot(g_low.astype(BF16), wgk_ref[...]) + bgk_ref[...]
    log_a = _log_sigmoid(z) * (1.0 / GLA_TAU)

    sub = min(tm, MIX_SUB)
    subs = [slice(s * sub, (s + 1) * sub) for s in range(tm // sub)]
    t_row = lax.broadcasted_iota(jnp.int32, (sub, sub), 0)
    t_col = lax.broadcasted_iota(jnp.int32, (sub, sub), 1)
    same_chunk = (t_row >> 6) == (t_col >> 6)
    l_cum = jnp.where(same_chunk & (t_row >= t_col), 1.0, 0.0).astype(BF16)
    l_all = jnp.where(same_chunk, 1.0, 0.0).astype(BF16)
    bcum = jnp.concatenate([_split_dot(l_cum, log_a[s]) for s in subs], axis=0)
    blast = jnp.concatenate([_split_dot(l_all, log_a[s]) for s in subs], axis=0)
    q_s = q * jnp.exp(bcum)
    k_s = (k * jnp.exp(-bcum)).astype(BF16)
    k_dec = (k * jnp.exp(blast - bcum)).astype(BF16)
    decay = jnp.exp(blast)
    yield True

    lane_g = lax.broadcasted_iota(jnp.int32, (GLA_CHUNK, LANES), 1)
    causal = (lax.broadcasted_iota(jnp.int32, (GLA_CHUNK, GLA_CHUNK), 0)
              >= lax.broadcasted_iota(jnp.int32, (GLA_CHUNK, GLA_CHUNK), 1))
    g_gla = g_gla_ref[...]
    o_rows = []
    for ci in range(tm // GLA_CHUNK):
        rs = slice(ci * GLA_CHUNK, (ci + 1) * GLA_CHUNK)
        o_heads = []
        for h in range(GLA_HEADS):
            pl_ = slice((h // 2) * LANES, (h // 2 + 1) * LANES)
            own = (lane_g >= GLA_HEAD_K) if (h % 2) else (lane_g < GLA_HEAD_K)
            qm = jnp.where(own, q_s[rs, pl_], 0.0).astype(BF16)
            v_h = vv[rs, h * GLA_HEAD_V:(h + 1) * GLA_HEAD_V]
            scores = jnp.where(causal, _dot_nt(qm, k_s[rs, pl_]), 0.0)
            intra = _dot(scores.astype(BF16), v_h)
            st = st_ref[h]
            inter = _dot_nt(qm, st.astype(BF16))
            st_ref[h] = st * decay[ci * GLA_CHUNK:ci * GLA_CHUNK + 1, pl_] + _dot_tn(v_h, k_dec[rs, pl_])
            o_heads.append(_rms(intra + inter) * g_gla)
        o_rows.append(jnp.concatenate(o_heads, axis=1))
    o = jnp.concatenate(o_rows, axis=0) * (r * jax.nn.sigmoid(r))
    mix_in = jnp.concatenate([a_out, o], axis=1).astype(BF16)
    yield True

    mix = _dot(mix_in, w_out_ref[...])
    yield True

    h1 = x + gate1 * (_rms(mix) * g_post_ref[...])
    h1_ref[0, rows, :] = h1
    n2 = _rms(h1) * (g_ffn_ref[...] * (1.0 + scale2)) + shift2
    n2_ref[rows, :] = _pack_rows(n2)
    n2b = n2.astype(BF16)
    yield True

    logits = _dot(n2b, w_r_ref[...]) + b_r_ref[...]
    lane = lax.broadcasted_iota(jnp.int32, (tm, LANES), 1)
    lane_f = lane.astype(F32)
    vals = logits
    sels, tops, idxs = [], [], []
    for _ in range(TOP_K):
        m = jnp.max(vals, axis=-1, keepdims=True)
        idx = jnp.min(jnp.where(vals == m, lane_f, float(LANES)), axis=-1, keepdims=True)
        sel = lane_f == idx
        sels.append(sel)
        tops.append(m)
        idxs.append(idx)
        vals = jnp.where(sel, -jnp.inf, vals)
    es = [jnp.exp(t - tops[0]) for t in tops]
    inv = 1.0 / (es[0] + es[1] + es[2] + es[3])

    multi = jnp.zeros((tm, LANES), F32)
    for sel in sels:
        multi = multi + jnp.where(sel, 1.0, 0.0)
    l_strict = jnp.where(t_row > t_col, 1.0, 0.0).astype(BF16)
    carry = carry_box[0]
    before = []
    for s in subs:
        before.append(_dot(l_strict, multi[s].astype(BF16)) + carry)
        carry = carry + jnp.sum(multi[s], axis=0, keepdims=True)
    before = jnp.concatenate(before, axis=0)
    carry_box[0] = carry

    route = jnp.zeros((tm, LANES), F32)
    probs = jnp.zeros((tm, LANES), F32)
    for kk in range(TOP_K):
        rank_kk = jnp.sum(jnp.where(sels[kk], before, 0.0), axis=-1, keepdims=True)
        route = jnp.where(lane == kk, idxs[kk], route)
        route = jnp.where(lane == TOP_K + kk, rank_kk, route)
        probs = jnp.where(lane == kk, es[kk] * inv, probs)
    route_ref[:, rows] = route.T[0:2 * TOP_K, :].astype(jnp.int32)
    probs_ref[rows, :] = probs


def _const_spec(shape):
    return pl.BlockSpec(shape, lambda b, j: (0,) * len(shape))


def _mixer(b0, B, x, mods, g_pre, w_in_p, gmat, g_sgu, ws_cat, bs_exp, wgk_p, bgk, g_gla, w_out_b, g_post, g_ffn, w_r_p,
           b_r_p):
    _, S, D = x.shape
    T = B * S
    tm = MIX_TM
    nj = S // tm
    return pl.pallas_call(
        _mixer_kernel,
        grid=(B, nj),
        in_specs=[pl.BlockSpec((1, tm, D), lambda b, j: (b + b0, j, 0)),
                  pl.BlockSpec((1, 6, D), lambda b, j: (b + b0, 0, 0)),
                  _const_spec((1, D)), _const_spec(w_in_p.shape), _const_spec(gmat.shape), _const_spec((1, D_SGU)),
                  _const_spec(ws_cat.shape), _const_spec(bs_exp.shape), _const_spec(wgk_p.shape),
                  _const_spec((1, GLA_DK)), _const_spec((1, GLA_HEAD_V)), _const_spec(w_out_b.shape),
                  _const_spec((1, D)), _const_spec((1, D)), _const_spec(w_r_p.shape), _const_spec((1, LANES))],
        out_specs=[pl.BlockSpec((1, tm, D), lambda b, j: (b, j, 0)),
                   pl.BlockSpec((tm, D // 2), lambda b, j: (b * nj + j, 0)),
                   pl.BlockSpec((2 * TOP_K, tm), lambda b, j: (0, b * nj + j)),
                   pl.BlockSpec((tm, LANES), lambda b, j: (b * nj + j, 0)),
                   pl.BlockSpec((8, LANES), lambda b, j: (0, 0))],
        out_shape=[jax.ShapeDtypeStruct((B, S, D), F32),
                   jax.ShapeDtypeStruct((T, D // 2), jnp.uint32),
                   jax.ShapeDtypeStruct((2 * TOP_K, T), jnp.int32),
                   jax.ShapeDtypeStruct((T, LANES), F32),
                   jax.ShapeDtypeStruct((8, LANES), F32)],
        scratch_shapes=[pltpu.VMEM((GLA_HEADS, GLA_HEAD_V, LANES), F32),
                        pltpu.VMEM((1, LANES), F32)],
        compiler_params=pltpu.CompilerParams(dimension_semantics=("arbitrary", "arbitrary"),
                                             vmem_limit_bytes=VMEM_LIMIT),
        name="mixer",
    )(x, mods, g_pre, w_in_p, gmat, g_sgu, ws_cat, bs_exp, wgk_p, bgk, g_gla, w_out_b, g_post, g_ffn, w_r_p, b_r_p)


def _sc_dispatch(rows, dest_kt, n_out):
    T, D = rows.shape
    top_k = dest_kt.shape[0]
    ch = SC_CH
    cpw = T // ch // SC_WORKERS
    assert T % (ch * SC_WORKERS) == 0
    mesh = plsc.VectorSubcoreMesh(core_axis_name="c", subcore_axis_name="s")

    @functools.partial(
        pl.kernel, mesh=mesh,
        out_type=jax.ShapeDtypeStruct((n_out, D), rows.dtype),
        scratch_types=[pltpu.VMEM((top_k, ch), jnp.int32), pltpu.VMEM((ch, D), rows.dtype)],
    )
    def k(rows_hbm, dest_hbm, out_hbm, idx_v, rows_v):
        wid = lax.axis_index("s") * SC_CORES + lax.axis_index("c")

        @pl.loop(0, cpw)
        def _(i):
            t0 = (wid * cpw + i) * ch
            for kk in range(top_k):
                pltpu.sync_copy(dest_hbm.at[kk, pl.ds(t0, ch)], idx_v.at[kk])
            pltpu.sync_copy(rows_hbm.at[pl.ds(t0, ch)], rows_v)
            for kk in range(top_k):
                pltpu.sync_copy(rows_v, out_hbm.at[idx_v.at[kk]])

    return k(rows, dest_kt)


def _sc_gather(src, dest_kt, t_lo, t_n):
    _, D = src.shape
    top_k = dest_kt.shape[0]
    ch = SC_CH
    cpk = t_n // ch
    cpw = top_k * cpk // SC_WORKERS
    mesh = plsc.VectorSubcoreMesh(core_axis_name="c", subcore_axis_name="s")
    assert t_n % ch == 0 and (top_k * cpk) % (2 * SC_WORKERS) == 0
    n_pairs = cpw // 2

    @functools.partial(
        pl.kernel, mesh=mesh,
        out_type=jax.ShapeDtypeStruct((top_k * t_n, D), src.dtype),
        scratch_types=[pltpu.VMEM((2, ch), jnp.int32), pltpu.VMEM((2, ch, D), src.dtype),
                       pltpu.SemaphoreType.DMA, pltpu.SemaphoreType.DMA,
                       pltpu.SemaphoreType.DMA, pltpu.SemaphoreType.DMA],
    )
    def k(src_hbm, idx_hbm, out_hbm, idx_v, rows_v, gsem0, gsem1, wsem0, wsem1):
        wid = lax.axis_index("s") * SC_CORES + lax.axis_index("c")
        gsem = (gsem0, gsem1)
        wsem = (wsem0, wsem1)

        def gather_copy(b):
            return pltpu.make_async_copy(src_hbm.at[idx_v.at[b]], rows_v.at[b], gsem[b])

        def write_copy(c, b):
            return pltpu.make_async_copy(rows_v.at[b], out_hbm.at[pl.ds(c * ch, ch)], wsem[b])

        def start_gather(c, b):
            kk = c // cpk
            t0 = t_lo + (c - kk * cpk) * ch
            pltpu.sync_copy(idx_hbm.at[kk, pl.ds(t0, ch)], idx_v.at[b])
            gather_copy(b).start()

        c_first = wid * cpw
        start_gather(c_first, 0)

        @pl.loop(0, n_pairs)
        def _(j):
            c0 = c_first + 2 * j
            c1 = c0 + 1

            @pl.when(j > 0)
            def _():
                write_copy(c0 - 1, 1).wait()

            start_gather(c1, 1)
            gather_copy(0).wait()
            write_copy(c0, 0).start()
            write_copy(c0, 0).wait()

            @pl.when(j < n_pairs - 1)
            def _():
                start_gather(c0 + 2, 0)

            gather_copy(1).wait()
            write_copy(c1, 1).start()

        write_copy(c_first + cpw - 1, 1).wait()

    return k(src, dest_kt)


def _expert_kernel(be_ref, first_ref, last_ref, nxt_ref, slot_ref, quarters_ref, rng_ref,
                   x_ref, wg_hbm, bg_ref, wu_hbm, bu_ref, wd_hbm, bd_ref, *rest):
    o_ref, stage_ref, wbf_ref, sem = rest[-4:]
    n_sched = be_ref.shape[0]
    i = jnp.minimum(pl.program_id(0) + rng_ref[0], n_sched - 1)
    active = pl.program_id(0) + rng_ref[0] < rng_ref[1]
    has_next = nxt_ref[i] >= 0

    def weight_copies(e):
        return [pltpu.make_async_copy(w.at[e], stage_ref.at[m], sem.at[m])
                for m, w in enumerate((wg_hbm, wu_hbm, wd_hbm))]

    def cast_stage_into(slot):
        for m in range(3):
            wbf_ref[slot, m] = stage_ref[m].astype(BF16)

    @pl.when(pl.program_id(0) == 0)
    def _():
        for cp in weight_copies(be_ref[i]):
            cp.start()
        for cp in weight_copies(be_ref[i]):
            cp.wait()
        cast_stage_into(slot_ref[i])

    @pl.when(active & (first_ref[i] == 1) & has_next)
    def _():
        for cp in weight_copies(nxt_ref[i]):
            cp.start()

    def ffn(rows):
        slot = slot_ref[i]
        x_lo, x_hi = _unpack_rows(x_ref[rows, :])
        xb = jnp.concatenate([x_lo, x_hi], axis=1).astype(BF16)
        g = jnp.minimum(_dot(xb, wbf_ref[slot, 0]) + bg_ref[0], SWIGLU_LIMIT)
        u = jnp.clip(_dot(xb, wbf_ref[slot, 1]) + bu_ref[0], -SWIGLU_LIMIT, SWIGLU_LIMIT)
        hdn = (u + 1.0) * (g * jax.nn.sigmoid(SWIGLU_ALPHA * g))
        o_ref[rows, :] = _pack_rows(_dot(hdn.astype(BF16), wbf_ref[slot, 2]) + bd_ref[0])

    bm = x_ref.shape[0]
    for nq in range(1, MOE_QUARTERS + 1):
        @pl.when(active & (quarters_ref[i] == nq))
        def _(nq=nq):
            ffn(slice(0, nq * bm // MOE_QUARTERS))

    @pl.when(active & (last_ref[i] == 1) & has_next)
    def _():
        for cp in weight_copies(nxt_ref[i]):
            cp.wait()
        cast_stage_into(1 - slot_ref[i])


def _expert_schedule(counts, padded, n_blocks, bm):
    pend = jnp.cumsum(padded)
    ids = jnp.arange(N_EXPERTS, dtype=jnp.int32)
    idx = jnp.arange(n_blocks, dtype=jnp.int32)
    n_used = (pend[-1] // bm).astype(jnp.int32)
    block_e = jnp.minimum(jnp.sum((pend[None, :] <= (idx * bm)[:, None]).astype(jnp.int32), axis=1), N_EXPERTS - 1)
    onehot = block_e[:, None] == ids[None, :]

    def per_block(table):
        return jnp.sum(jnp.where(onehot, table[None, :], 0), axis=1)

    start_blk = (pend - padded) // bm
    first = (idx == per_block(start_blk)) & (idx < n_used)
    last = (idx == per_block(pend // bm) - 1) & (idx < n_used)
    later = (ids[None, :] > ids[:, None]) & (padded[None, :] > 0)
    nxt_e = jnp.min(jnp.where(later, ids[None, :], N_EXPERTS), axis=1)
    nxt_e = jnp.where(nxt_e == N_EXPERTS, -1, nxt_e)
    order = jnp.cumsum((padded > 0).astype(jnp.int32)) - 1
    real_rows = per_block(counts) - (idx - per_block(start_blk)) * bm
    qrows = bm // MOE_QUARTERS
    quarters = jnp.clip((real_rows + qrows - 1) // qrows, 1, MOE_QUARTERS)
    as_i32 = lambda a: a.astype(jnp.int32)
    schedule = (block_e, as_i32(first), as_i32(last), per_block(nxt_e), per_block(order) & 1, as_i32(quarters))
    return schedule, n_used, (pend // bm).astype(jnp.int32)


def _experts(schedule, block_range, xs, w_gate, b_gate, w_up, b_up, w_down, b_down, prev_out=None, after=None):
    P, half = xs.shape
    bm = MOE_BM
    n_blocks = P // bm
    E, D, DE = w_gate.shape
    assert D == DE

    def block_of(i, rng):
        return jnp.maximum(jnp.minimum(i + rng[0], rng[1] - 1), 0)

    def row_map(i, be, first, last, nxt, slot, quarters, rng):
        return (block_of(i, rng), 0)

    def b_map(i, be, first, last, nxt, slot, quarters, rng):
        return (be[block_of(i, rng)], 0, 0)

    hbm = pl.BlockSpec(memory_space=pl.ANY)
    extra = [a for a in (prev_out, after) if a is not None]
    n_fixed = len(schedule) + 1 + 7
    aliases = {n_fixed: 0} if prev_out is not None else {}
    grid_spec = pltpu.PrefetchScalarGridSpec(
        num_scalar_prefetch=len(schedule) + 1,
        grid=(n_blocks,),
        in_specs=[pl.BlockSpec((bm, half), row_map),
                  hbm, pl.BlockSpec((1, 1, DE), b_map),
                  hbm, pl.BlockSpec((1, 1, DE), b_map),
                  hbm, pl.BlockSpec((1, 1, D), b_map)] + [hbm] * len(extra),
        out_specs=pl.BlockSpec((bm, half), row_map),
        scratch_shapes=[pltpu.VMEM((3, D, DE), F32),
                        pltpu.VMEM((2, 3, D, DE), BF16),
                        pltpu.SemaphoreType.DMA((3,))],
    )
    return pl.pallas_call(
        _expert_kernel,
        grid_spec=grid_spec,
        out_shape=jax.ShapeDtypeStruct((P, half), jnp.uint32),
        input_output_aliases=aliases,
        compiler_params=pltpu.CompilerParams(dimension_semantics=("arbitrary",), vmem_limit_bytes=VMEM_LIMIT),
        name="experts",
    )(*schedule, block_range, xs, w_gate, b_gate.reshape(E, 1, DE), w_up, b_up.reshape(E, 1, DE),
      w_down, b_down.reshape(E, 1, D), *extra)


def _combine_kernel(h1_ref, og_ref, probs_ref, mods_ref, g_ref, *rest):
    o_ref = rest[-1]
    tm = h1_ref.shape[1]
    probs = probs_ref[...]
    lane = lax.broadcasted_iota(jnp.int32, (tm, LANES), 1)
    half = og_ref.shape[2]
    f_lo = jnp.zeros((tm, half), F32)
    f_hi = jnp.zeros((tm, half), F32)
    for kk in range(TOP_K):
        pk = jnp.sum(jnp.where(lane == kk, probs, 0.0), axis=-1, keepdims=True)
        o_lo, o_hi = _unpack_rows(og_ref[kk])
        f_lo = f_lo + pk * o_lo
        f_hi = f_hi + pk * o_hi
    f = jnp.concatenate([f_lo, f_hi], axis=1)
    gate2 = mods_ref[0][5:6]
    o_ref[0] = h1_ref[0] + gate2 * (_rms(f) * g_ref[...])


def _combine(b0, b_total, t_lo, h1, og, probs, mods, g_post_ffn, out_so_far, after=None):
    _, S, D = h1.shape
    t_n = og.shape[1]
    tm = CMB_TM
    nj = S // tm
    q0 = t_lo // tm

    in_specs = [pl.BlockSpec((1, tm, D), lambda q: ((q0 + q) // nj, (q0 + q) % nj, 0)),
                pl.BlockSpec((TOP_K, tm, D // 2), lambda q: (0, q, 0)),
                pl.BlockSpec((tm, LANES), lambda q: (q0 + q, 0)),
                pl.BlockSpec((1, 6, D), lambda q: (b0 + (q0 + q) // nj, 0, 0)),
                pl.BlockSpec((1, D), lambda q: (0, 0))]
    args = [h1, og, probs, mods, g_post_ffn]
    aliases = {}
    if after is not None:
        in_specs.append(pl.BlockSpec(memory_space=pl.ANY))
        args.append(after)
    if out_so_far is not None:
        in_specs.append(pl.BlockSpec(memory_space=pl.ANY))
        args.append(out_so_far)
        aliases = {len(args) - 1: 0}
    return pl.pallas_call(
        _combine_kernel,
        grid=(t_n // tm,),
        in_specs=in_specs,
        out_specs=pl.BlockSpec((1, tm, D), lambda q: (b0 + (q0 + q) // nj, (q0 + q) % nj, 0)),
        out_shape=jax.ShapeDtypeStruct((b_total, S, D), F32),
        input_output_aliases=aliases,
        compiler_params=pltpu.CompilerParams(dimension_semantics=("arbitrary",), vmem_limit_bytes=VMEM_LIMIT),
        name="combine",
    )(*args)


def _layer(h, c_pad, w_ada, b_ada, g_pre_mix, w_in, g_sgu_v, w_s, b_s, w_gk2, b_gk, g_gla_out, w_out, g_post_mix,
           g_pre_ffn, w_router, b_router, w_gate, b_gate, w_up, b_up, w_down, b_down, g_post_ffn):
    B, S, D = h.shape

    mods = _ada(c_pad, w_ada, b_ada.reshape(1, -1))[:B].reshape(B, 6, D)

    w_in_p = jnp.pad(w_in, ((0, 0), (0, D_IN_PAD - w_in.shape[1]))).astype(BF16)
    head_of = jnp.arange(D_SGU) // SGU_HEAD_DIM
    gmat = jnp.where(head_of[:, None] == head_of[None, :], 1.0 / SGU_HEAD_DIM, 0.0).astype(BF16)
    ws_cat = w_s.reshape(SGU_HEADS // 2, 2, SGU_CHUNK, SGU_CHUNK).transpose(0, 2, 1, 3).reshape(
        SGU_HEADS // 2, SGU_CHUNK, 2 * SGU_CHUNK)
    bs_exp = jnp.repeat(b_s.T, SGU_HEAD_DIM, axis=1)
    wgk_p = jnp.pad(w_gk2, ((0, LANES - GLA_GATE_RANK), (0, 0))).astype(BF16)
    w_r_p = jnp.pad(w_router, ((0, 0), (0, LANES - N_EXPERTS))).astype(BF16)
    b_r_p = jnp.concatenate([b_router, jnp.full((LANES - N_EXPERTS,), ROUTER_PAD, F32)]).reshape(1, LANES)

    w_out_b = w_out.astype(BF16)

    last = max(1, B // LAST_GROUP_FRACTION)
    group_sizes = [B - last, last] if B > 1 else [B]
    g_post = g_post_ffn.reshape(1, D)
    result = None
    pending = None
    b0 = 0
    for gi, bg in enumerate(group_sizes):
        tg = bg * S
        h1, n2, route, probs, cnt = _mixer(
            b0, bg, h, mods, g_pre_mix.reshape(1, D), w_in_p, gmat, g_sgu_v.reshape(1, D_SGU), ws_cat, bs_exp, wgk_p,
            b_gk.reshape(1, GLA_DK), g_gla_out.reshape(1, GLA_HEAD_V), w_out_b, g_post_mix.reshape(1, D),
            g_pre_ffn.reshape(1, D), w_r_p, b_r_p)

        bm = MOE_BM
        counts = cnt[0, :N_EXPERTS].astype(jnp.int32)
        padded = ((counts + bm - 1) // bm) * bm
        pend = jnp.cumsum(padded)
        pstart = pend - padded
        e_kt, rank_kt = route[:TOP_K], route[TOP_K:]
        onehot = e_kt[:, :, None] == jnp.arange(N_EXPERTS, dtype=jnp.int32)[None, None, :]
        dest_kt = rank_kt + jnp.sum(jnp.where(onehot, pstart[None, None, :], 0), axis=-1)
        n_blocks = tg * TOP_K // bm + N_EXPERTS
        schedule, n_used, blocks_end = _expert_schedule(counts, padded, n_blocks, bm)
        weights = (w_gate, b_gate, w_up, b_up, w_down, b_down)

        xs = _sc_dispatch(n2, dest_kt, n_blocks * bm)
        if pending is None:
            out = _experts(schedule, jnp.stack([jnp.zeros_like(n_used), n_used]), xs, *weights)
        else:
            split = blocks_end[N_EXPERTS // 2 - 1]
            out_a = _experts(schedule, jnp.stack([jnp.zeros_like(split), split]), xs, *weights)
            p_b0, p_h1, p_probs, p_out, p_dest, p_tg = pending
            og = _sc_gather(p_out, p_dest, 0, p_tg).reshape(TOP_K, p_tg, D // 2)
            result = _combine(p_b0, B, 0, p_h1, og, p_probs, mods, g_post, result, after=out_a)
            out = _experts(schedule, jnp.stack([split, n_used]), xs, *weights, prev_out=out_a, after=result)
        pending = (b0, h1, probs, out, dest_kt, tg)
        b0 += bg

    p_b0, p_h1, p_probs, p_out, p_dest, p_tg = pending
    n_tail = next(n for n in (TAIL_CHUNKS, 2, 1)
                  if p_tg % (n * CMB_TM) == 0 and (p_tg // n * TOP_K // SC_CH) % (2 * SC_WORKERS) == 0 or n == 1)
    t_n = p_tg // n_tail
    for ci in range(n_tail):
        og = _sc_gather(p_out, p_dest, ci * t_n, t_n).reshape(TOP_K, t_n, D // 2)
        result = _combine(p_b0, B, ci * t_n, p_h1, og, p_probs, mods, g_post, result)
    return result


def kernel(x, c, w_ada, b_ada, g_pre_mix, w_in, g_sgu_v, w_s, b_s, w_gk2, b_gk, g_gla_out, w_out, g_post_mix, g_pre_ffn, w_router, b_router, w_gate, b_gate, w_up, b_up, w_down, b_down, g_post_ffn):
    B = x.shape[0]
    c_pad = jnp.pad(c, ((0, -B % 8), (0, 0)))
    h = x
    for l in range(w_ada.shape[0]):
        h = _layer(h, c_pad, w_ada[l], b_ada[l], g_pre_mix[l], w_in[l], g_sgu_v[l], w_s[l], b_s[l], w_gk2[l], b_gk[l],
                   g_gla_out[l], w_out[l], g_post_mix[l], g_pre_ffn[l], w_router[l], b_router[l], w_gate[l], b_gate[l],
                   w_up[l], b_up[l], w_down[l], b_down[l], g_post_ffn[l])
    return h
```

```python
import functools

import jax
import jax.numpy as jnp
from jax import lax
from jax.experimental import pallas as pl
from jax.experimental.pallas import tpu as pltpu
from jax.experimental.pallas import tpu_sc as plsc

F32 = jnp.float32
BF16 = jnp.bfloat16

D_MODEL = 1024
D_SGU = 512
SGU_HEADS = 8
SGU_HEAD_DIM = 64
SGU_CHUNK = 128
D_GLA = 512
GLA_HEADS = 4
GLA_DK = 256
GLA_HEAD_K = 64
GLA_HEAD_V = 128
GLA_GATE_RANK = 16
GLA_TAU = 16.0
GLA_CHUNK = 64
N_EXPERTS = 32
TOP_K = 4
SWIGLU_LIMIT = 7.0
SWIGLU_ALPHA = 1.702
EPS = 1e-6

LANES = 128
D_IN_MAIN = 2 * D_SGU + 2 * GLA_DK + 2 * D_GLA
D_IN_PAD = D_IN_MAIN + LANES

MIX_TM = 1024
MIX_TILE = 512
MIX_SKEW = 1
MIX_SUB = 256
MOE_BM = 1024
MOE_QUARTERS = 4
CMB_TM = 1024
LAST_GROUP_FRACTION = 2
SC_CORES = 2
SC_SUBCORES = 16
SC_WORKERS = SC_CORES * SC_SUBCORES
SC_CH = 64
ADA_TN = 1536
ROUTER_PAD = -jnp.inf

VMEM_LIMIT = 56 * 1024 * 1024


def _ada_kernel(c_ref, w_ref, b_ref, o_ref):
    c = c_ref[...]
    sc = c * jax.nn.sigmoid(c)
    o_ref[...] = jnp.dot(sc.astype(BF16), w_ref[...].astype(BF16), preferred_element_type=F32) + b_ref[...]


def _ada(c_pad, w_ada, b_ada):
    rows, d = c_pad.shape
    n = w_ada.shape[1]
    tn = ADA_TN
    return pl.pallas_call(
        _ada_kernel,
        grid=(n // tn,),
        in_specs=[pl.BlockSpec((rows, d), lambda j: (0, 0)),
                  pl.BlockSpec((d, tn), lambda j: (0, j)),
                  pl.BlockSpec((1, tn), lambda j: (0, j))],
        out_specs=pl.BlockSpec((rows, tn), lambda j: (0, j)),
        out_shape=jax.ShapeDtypeStruct((rows, n), F32),
        compiler_params=pltpu.CompilerParams(dimension_semantics=("arbitrary",), vmem_limit_bytes=VMEM_LIMIT),
        name="ada",
    )(c_pad, w_ada, b_ada)


def _rms(x):
    return x * lax.rsqrt(jnp.mean(x * x, axis=-1, keepdims=True) + EPS)


def _gelu_tanh(x):
    return 0.5 * x * (1.0 + jnp.tanh(0.7978845608028654 * (x + 0.044715 * (x * x * x))))


def _log_sigmoid(z):
    return jnp.minimum(z, 0.0) - jnp.log(1.0 + jnp.exp(-jnp.abs(z)))


def _dot(a, b):
    return jnp.dot(a, b, preferred_element_type=F32)


def _dot_nt(a, b):
    return lax.dot_general(a, b, (((1,), (1,)), ((), ())), preferred_element_type=F32)


def _dot_tn(a, b):
    return lax.dot_general(a, b, (((0,), (0,)), ((), ())), preferred_element_type=F32)


def _pack_rows(x):
    n = x.shape[1] // 2
    lo = lax.bitcast_convert_type(x[:, :n].astype(BF16).astype(F32), jnp.uint32)
    hi = lax.bitcast_convert_type(x[:, n:].astype(BF16).astype(F32), jnp.uint32)
    return (lo >> 16) | (hi & jnp.uint32(0xFFFF0000))


def _unpack_rows(p):
    lo = lax.bitcast_convert_type(p << 16, F32)
    hi = lax.bitcast_convert_type(p & jnp.uint32(0xFFFF0000), F32)
    return lo, hi


def _split_dot(l_bf, a):
    hi = a.astype(BF16)
    lo = (a - hi.astype(F32)).astype(BF16)
    return _dot(l_bf, hi) + _dot(l_bf, lo)


def _mixer_kernel(x_ref, mods_ref, g_pre_ref, w_in_ref, gmat_ref, g_sgu_ref, ws_ref, bs_ref, wgk_ref, bgk_ref,
                  g_gla_ref, w_out_ref, g_post_ref, g_ffn_ref, w_r_ref, b_r_ref,
                  h1_ref, n2_ref, route_ref, probs_ref, cnt_ref,
                  st_ref, carry_ref):
    b = pl.program_id(0)
    j = pl.program_id(1)

    @pl.when(j == 0)
    def _():
        st_ref[...] = jnp.zeros_like(st_ref)

    @pl.when((b == 0) & (j == 0))
    def _():
        carry_ref[...] = jnp.zeros_like(carry_ref)

    carry_box = [carry_ref[...]]
    tiles = [_mixer_tile(slice(s * MIX_TILE, (s + 1) * MIX_TILE), carry_box,
                         x_ref, mods_ref, g_pre_ref, w_in_ref, gmat_ref, g_sgu_ref, ws_ref, bs_ref, wgk_ref, bgk_ref,
                         g_gla_ref, w_out_ref, g_post_ref, g_ffn_ref, w_r_ref, b_r_ref,
                         h1_ref, n2_ref, route_ref, probs_ref, st_ref)
             for s in range(x_ref.shape[1] // MIX_TILE)]
    alive = [True] * len(tiles)
    t = 0
    while any(alive):
        for s, tile in enumerate(tiles):
            if alive[s] and t >= s * MIX_SKEW:
                alive[s] = next(tile, None) is not None
        t += 1
    carry_ref[...] = carry_box[0]
    cnt_ref[...] = jnp.broadcast_to(carry_box[0], cnt_ref.shape)


def _mixer_tile(rows, carry_box, x_ref, mods_ref, g_pre_ref, w_in_ref, gmat_ref, g_sgu_ref, ws_ref, bs_ref, wgk_ref,
                bgk_ref, g_gla_ref, w_out_ref, g_post_ref, g_ffn_ref, w_r_ref, b_r_ref,
                h1_ref, n2_ref, route_ref, probs_ref, st_ref):
    tm = rows.stop - rows.start
    mods = mods_ref[0]
    shift1, scale1, gate1 = mods[0:1], mods[1:2], mods[2:3]
    shift2, scale2 = mods[3:4], mods[4:5]

    x = x_ref[0, rows, :]
    n = _rms(x) * (g_pre_ref[...] * (1.0 + scale1)) + shift1
    nb = n.astype(BF16)
    yield True

    pu = _dot(nb, w_in_ref[:, 0:D_SGU])
    pv = _dot(nb, w_in_ref[:, D_SGU:2 * D_SGU])
    yield True

    u = _gelu_tanh(pu)
    v = _gelu_tanh(pv)
    msv = _dot((v * v).astype(BF16), gmat_ref[...])
    vh = v * lax.rsqrt(msv + EPS) * g_sgu_ref[...]
    yield True

    qk = _dot(nb, w_in_ref[:, 2 * D_SGU:2 * D_SGU + 2 * GLA_DK])
    vv = _dot(nb, w_in_ref[:, 2 * D_SGU + 2 * GLA_DK:2 * D_SGU + 2 * GLA_DK + D_GLA]).astype(BF16)
    r = _dot(nb, w_in_ref[:, D_IN_MAIN - D_GLA:D_IN_MAIN])
    g_low = _dot(nb, w_in_ref[:, D_IN_MAIN:D_IN_PAD])
    yield True

    n_sc = tm // SGU_CHUNK
    lane_c = lax.broadcasted_iota(jnp.int32, (SGU_CHUNK, LANES), 1)
    low_half = lane_c < SGU_HEAD_DIM
    w_row = lax.broadcasted_iota(jnp.int32, (SGU_CHUNK, 2 * SGU_CHUNK), 0)
    w_col = lax.broadcasted_iota(jnp.int32, (SGU_CHUNK, 2 * SGU_CHUNK), 1) & (SGU_CHUNK - 1)
    a_cols = []
    for p in range(SGU_HEADS // 2):
        wcat = jnp.where(w_row >= w_col, ws_ref[p], 0.0).astype(BF16)
        rhs = []
        for ci in range(n_sc):
            vp = vh[ci * SGU_CHUNK:(ci + 1) * SGU_CHUNK, p * LANES:(p + 1) * LANES]
            rhs.append(jnp.concatenate([jnp.where(low_half, vp, 0.0), jnp.where(low_half, 0.0, vp)],
                                       axis=0).astype(BF16))
        res = _dot(wcat, jnp.concatenate(rhs, axis=1))
        bias = bs_ref[:, p * LANES:(p + 1) * LANES]
        a_cols.append(jnp.concatenate(
            [res[:, ci * LANES:(ci + 1) * LANES] + bias for ci in range(n_sc)], axis=0))
    a_out = u * jnp.concatenate(a_cols, axis=1)

    q = qk[:, :GLA_DK] * (GLA_HEAD_K ** -0.5)
    k = qk[:, GLA_DK:]
    z = _dot(g_low.astype(BF16), wgk_ref[...]) + bgk_ref[...]
    log_a = _log_sigmoid(z) * (1.0 / GLA_TAU)

    sub = min(tm, MIX_SUB)
    subs = [slice(s * sub, (s + 1) * sub) for s in range(tm // sub)]
    t_row = lax.broadcasted_iota(jnp.int32, (sub, sub), 0)
    t_col = lax.broadcasted_iota(jnp.int32, (sub, sub), 1)
    same_chunk = (t_row >> 6) == (t_col >> 6)
    l_cum = jnp.where(same_chunk & (t_row >= t_col), 1.0, 0.0).astype(BF16)
    n_gc = tm // GLA_CHUNK
    bcum = jnp.concatenate([_split_dot(l_cum, log_a[s]) for s in subs], axis=0)
    blast = jnp.concatenate(
        [jnp.broadcast_to(bcum[(ci + 1) * GLA_CHUNK - 1:(ci + 1) * GLA_CHUNK], (GLA_CHUNK, GLA_DK)) for ci in range(n_gc)],
        axis=0)
    q_s = q * jnp.exp(bcum)
    k_s = k * jnp.exp(-bcum)
    k_dec = (k * jnp.exp(blast - bcum)).astype(BF16)
    in_chunk = jnp.where((lax.broadcasted_iota(jnp.int32, (tm, LANES), 0) >> 6)
                         == lax.broadcasted_iota(jnp.int32, (tm, LANES), 1), 1.0, 0.0).astype(BF16)
    la_hi = log_a.astype(BF16)
    la_lo = (log_a - la_hi.astype(F32)).astype(BF16)
    dcol_t = jnp.exp(_dot_tn(la_hi, in_chunk) + _dot_tn(la_lo, in_chunk))
    yield True

    low_k = lax.broadcasted_iota(jnp.int32, (GLA_CHUNK, LANES), 1) < GLA_HEAD_K
    c_row = lax.broadcasted_iota(jnp.int32, (GLA_CHUNK, LANES), 0)
    c_col = lax.broadcasted_iota(jnp.int32, (GLA_CHUNK, LANES), 1) & (GLA_CHUNK - 1)
    causal2 = c_row >= c_col
    own_block = ((lax.broadcasted_iota(jnp.int32, (LANES, 2 * GLA_HEAD_V), 0) < GLA_HEAD_K)
                 == (lax.broadcasted_iota(jnp.int32, (LANES, 2 * GLA_HEAD_V), 1) < GLA_HEAD_V))
    zeros_v = jnp.zeros((GLA_CHUNK, GLA_HEAD_V), BF16)
    g_gla = g_gla_ref[...]
    o_rows = []
    for ci in range(n_gc):
        rs = slice(ci * GLA_CHUNK, (ci + 1) * GLA_CHUNK)
        o_heads = []
        for p in range(GLA_HEADS // 2):
            pl_ = slice(p * LANES, (p + 1) * LANES)
            pv_ = slice(2 * p * GLA_HEAD_V, 2 * (p + 1) * GLA_HEAD_V)
            qp = q_s[rs, pl_].astype(BF16)
            ksp = k_s[rs, pl_]
            k2 = jnp.concatenate([jnp.where(low_k, ksp, 0.0), jnp.where(low_k, 0.0, ksp)], axis=0).astype(BF16)
            scores2 = jnp.where(causal2, _dot_nt(qp, k2), 0.0).astype(BF16)
            v0 = vv[rs, pv_][:, :GLA_HEAD_V]
            v1 = vv[rs, pv_][:, GLA_HEAD_V:]
            v2 = jnp.concatenate([jnp.concatenate([v0, zeros_v], axis=1),
                                  jnp.concatenate([zeros_v, v1], axis=1)], axis=0)
            st = st_ref[p]
            o2 = _dot(jnp.concatenate([scores2, qp], axis=1), jnp.concatenate([v2, st.astype(BF16)], axis=0))
            upd = _dot_tn(k_dec[rs, pl_], vv[rs, pv_])
            st_ref[p] = st * dcol_t[pl_, ci:ci + 1] + jnp.where(own_block, upd, 0.0)
            o_heads.append(_rms(o2[:, :GLA_HEAD_V]) * g_gla)
            o_heads.append(_rms(o2[:, GLA_HEAD_V:]) * g_gla)
        o_rows.append(jnp.concatenate(o_heads, axis=1))
    o = jnp.concatenate(o_rows, axis=0) * (r * jax.nn.sigmoid(r))
    mix_in = jnp.concatenate([a_out, o], axis=1).astype(BF16)
    yield True

    mix = _dot(mix_in, w_out_ref[...])
    yield True

    h1 = x + gate1 * (_rms(mix) * g_post_ref[...])
    h1_ref[0, rows, :] = h1
    n2 = _rms(h1) * (g_ffn_ref[...] * (1.0 + scale2)) + shift2
    n2_ref[rows, :] = _pack_rows(n2)
    n2b = n2.astype(BF16)
    yield True

    logits = _dot(n2b, w_r_ref[...]) + b_r_ref[...]
    lane = lax.broadcasted_iota(jnp.int32, (tm, LANES), 1)
    lane_f = lane.astype(F32)
    vals = logits
    sels, tops, idxs = [], [], []
    for _ in range(TOP_K):
        m = jnp.max(vals, axis=-1, keepdims=True)
        idx = jnp.min(jnp.where(vals == m, lane_f, float(LANES)), axis=-1, keepdims=True)
        sel = lane_f == idx
        sels.append(sel)
        tops.append(m)
        idxs.append(idx)
        vals = jnp.where(sel, -jnp.inf, vals)
    es = [jnp.exp(t - tops[0]) for t in tops]
    inv = 1.0 / (es[0] + es[1] + es[2] + es[3])

    multi = jnp.zeros((tm, LANES), F32)
    for sel in sels:
        multi = multi + jnp.where(sel, 1.0, 0.0)
    l_strict = jnp.where(t_row > t_col, 1.0, 0.0).astype(BF16)
    carry = carry_box[0]
    before = []
    for s in subs:
        before.append(_dot(l_strict, multi[s].astype(BF16)) + carry)
        carry = carry + jnp.sum(multi[s], axis=0, keepdims=True)
    before = jnp.concatenate(before, axis=0)
    carry_box[0] = carry

    route = jnp.zeros((tm, LANES), F32)
    probs = jnp.zeros((tm, LANES), F32)
    for kk in range(TOP_K):
        rank_kk = jnp.sum(jnp.where(sels[kk], before, 0.0), axis=-1, keepdims=True)
        route = jnp.where(lane == kk, idxs[kk], route)
        route = jnp.where(lane == TOP_K + kk, rank_kk, route)
        probs = jnp.where(lane == kk, es[kk] * inv, probs)
    route_ref[:, rows] = route.T[0:2 * TOP_K, :].astype(jnp.int32)
    probs_ref[rows, :] = probs


def _const_spec(shape):
    return pl.BlockSpec(shape, lambda b, j: (0,) * len(shape))


def _mixer(b0, B, x, mods, g_pre, w_in_p, gmat, g_sgu, ws_cat, bs_exp, wgk_p, bgk, g_gla, w_out_b, g_post, g_ffn, w_r_p,
           b_r_p):
    _, S, D = x.shape
    T = B * S
    tm = MIX_TM
    nj = S // tm
    return pl.pallas_call(
        _mixer_kernel,
        grid=(B, nj),
        in_specs=[pl.BlockSpec((1, tm, D), lambda b, j: (b + b0, j, 0)),
                  pl.BlockSpec((1, 6, D), lambda b, j: (b + b0, 0, 0)),
                  _const_spec((1, D)), _const_spec(w_in_p.shape), _const_spec(gmat.shape), _const_spec((1, D_SGU)),
                  _const_spec(ws_cat.shape), _const_spec(bs_exp.shape), _const_spec(wgk_p.shape),
                  _const_spec((1, GLA_DK)), _const_spec((1, GLA_HEAD_V)), _const_spec(w_out_b.shape),
                  _const_spec((1, D)), _const_spec((1, D)), _const_spec(w_r_p.shape), _const_spec((1, LANES))],
        out_specs=[pl.BlockSpec((1, tm, D), lambda b, j: (b, j, 0)),
                   pl.BlockSpec((tm, D // 2), lambda b, j: (b * nj + j, 0)),
                   pl.BlockSpec((2 * TOP_K, tm), lambda b, j: (0, b * nj + j)),
                   pl.BlockSpec((tm, LANES), lambda b, j: (b * nj + j, 0)),
                   pl.BlockSpec((8, LANES), lambda b, j: (0, 0))],
        out_shape=[jax.ShapeDtypeStruct((B, S, D), F32),
                   jax.ShapeDtypeStruct((T, D // 2), jnp.uint32),
                   jax.ShapeDtypeStruct((2 * TOP_K, T), jnp.int32),
                   jax.ShapeDtypeStruct((T, LANES), F32),
                   jax.ShapeDtypeStruct((8, LANES), F32)],
        scratch_shapes=[pltpu.VMEM((GLA_HEADS // 2, 2 * GLA_HEAD_K, 2 * GLA_HEAD_V), F32),
                        pltpu.VMEM((1, LANES), F32)],
        compiler_params=pltpu.CompilerParams(dimension_semantics=("arbitrary", "arbitrary"),
                                             vmem_limit_bytes=VMEM_LIMIT),
        name="mixer",
    )(x, mods, g_pre, w_in_p, gmat, g_sgu, ws_cat, bs_exp, wgk_p, bgk, g_gla, w_out_b, g_post, g_ffn, w_r_p, b_r_p)


def _sc_dispatch(rows, dest_kt, n_out):
    T, D = rows.shape
    top_k = dest_kt.shape[0]
    ch = SC_CH
    cpw = T // ch // SC_WORKERS
    assert T % (ch * SC_WORKERS) == 0
    mesh = plsc.VectorSubcoreMesh(core_axis_name="c", subcore_axis_name="s")

    @functools.partial(
        pl.kernel, mesh=mesh,
        out_type=jax.ShapeDtypeStruct((n_out, D), rows.dtype),
        scratch_types=[pltpu.VMEM((top_k, ch), jnp.int32), pltpu.VMEM((ch, D), rows.dtype)],
    )
    def k(rows_hbm, dest_hbm, out_hbm, idx_v, rows_v):
        wid = lax.axis_index("s") * SC_CORES + lax.axis_index("c")

        @pl.loop(0, cpw)
        def _(i):
            t0 = (wid * cpw + i) * ch
            for kk in range(top_k):
                pltpu.sync_copy(dest_hbm.at[kk, pl.ds(t0, ch)], idx_v.at[kk])
            pltpu.sync_copy(rows_hbm.at[pl.ds(t0, ch)], rows_v)
            for kk in range(top_k):
                pltpu.sync_copy(rows_v, out_hbm.at[idx_v.at[kk]])

    return k(rows, dest_kt)


def _sc_gather(src, dest_kt):
    _, D = src.shape
    top_k, T = dest_kt.shape
    ch = SC_CH
    cpk = T // ch
    cpw = top_k * cpk // SC_WORKERS
    mesh = plsc.VectorSubcoreMesh(core_axis_name="c", subcore_axis_name="s")
    assert T % ch == 0 and (top_k * cpk) % (2 * SC_WORKERS) == 0
    n_pairs = cpw // 2

    @functools.partial(
        pl.kernel, mesh=mesh,
        out_type=jax.ShapeDtypeStruct((top_k * T, D), src.dtype),
        scratch_types=[pltpu.VMEM((2, ch), jnp.int32), pltpu.VMEM((2, ch, D), src.dtype),
                       pltpu.SemaphoreType.DMA, pltpu.SemaphoreType.DMA,
                       pltpu.SemaphoreType.DMA, pltpu.SemaphoreType.DMA],
    )
    def k(src_hbm, idx_hbm, out_hbm, idx_v, rows_v, gsem0, gsem1, wsem0, wsem1):
        wid = lax.axis_index("s") * SC_CORES + lax.axis_index("c")
        gsem = (gsem0, gsem1)
        wsem = (wsem0, wsem1)

        def gather_copy(b):
            return pltpu.make_async_copy(src_hbm.at[idx_v.at[b]], rows_v.at[b], gsem[b])

        def write_copy(c, b):
            return pltpu.make_async_copy(rows_v.at[b], out_hbm.at[pl.ds(c * ch, ch)], wsem[b])

        def start_gather(c, b):
            kk = c // cpk
            t0 = (c - kk * cpk) * ch
            pltpu.sync_copy(idx_hbm.at[kk, pl.ds(t0, ch)], idx_v.at[b])
            gather_copy(b).start()

        c_first = wid * cpw
        start_gather(c_first, 0)

        @pl.loop(0, n_pairs)
        def _(j):
            c0 = c_first + 2 * j
            c1 = c0 + 1

            @pl.when(j > 0)
            def _():
                write_copy(c0 - 1, 1).wait()

            start_gather(c1, 1)
            gather_copy(0).wait()
            write_copy(c0, 0).start()
            write_copy(c0, 0).wait()

            @pl.when(j < n_pairs - 1)
            def _():
                start_gather(c0 + 2, 0)

            gather_copy(1).wait()
            write_copy(c1, 1).start()

        write_copy(c_first + cpw - 1, 1).wait()

    return k(src, dest_kt)


def _expert_kernel(be_ref, first_ref, last_ref, nxt_ref, slot_ref, quarters_ref, nu_ref,
                   x_ref, wg_hbm, bg_ref, wu_hbm, bu_ref, wd_hbm, bd_ref, o_ref,
                   stage_ref, wbf_ref, sem):
    i = pl.program_id(0)
    active = i < nu_ref[0]
    has_next = nxt_ref[i] >= 0

    def weight_copies(e):
        return [pltpu.make_async_copy(w.at[e], stage_ref.at[m], sem.at[m])
                for m, w in enumerate((wg_hbm, wu_hbm, wd_hbm))]

    def cast_stage_into(slot):
        for m in range(3):
            wbf_ref[slot, m] = stage_ref[m].astype(BF16)

    @pl.when(i == 0)
    def _():
        for cp in weight_copies(be_ref[0]):
            cp.start()
        for cp in weight_copies(be_ref[0]):
            cp.wait()
        cast_stage_into(0)

    @pl.when(active & (first_ref[i] == 1) & has_next)
    def _():
        for cp in weight_copies(nxt_ref[i]):
            cp.start()

    def ffn(rows):
        slot = slot_ref[i]
        x_lo, x_hi = _unpack_rows(x_ref[rows, :])
        xb = jnp.concatenate([x_lo, x_hi], axis=1).astype(BF16)
        g = jnp.minimum(_dot(xb, wbf_ref[slot, 0]) + bg_ref[0], SWIGLU_LIMIT)
        u = jnp.clip(_dot(xb, wbf_ref[slot, 1]) + bu_ref[0], -SWIGLU_LIMIT, SWIGLU_LIMIT)
        hdn = (u + 1.0) * (g * jax.nn.sigmoid(SWIGLU_ALPHA * g))
        o_ref[rows, :] = _pack_rows(_dot(hdn.astype(BF16), wbf_ref[slot, 2]) + bd_ref[0])

    bm = x_ref.shape[0]
    for nq in range(1, MOE_QUARTERS + 1):
        @pl.when(active & (quarters_ref[i] == nq))
        def _(nq=nq):
            ffn(slice(0, nq * bm // MOE_QUARTERS))

    @pl.when(active & (last_ref[i] == 1) & has_next)
    def _():
        for cp in weight_copies(nxt_ref[i]):
            cp.wait()
        cast_stage_into(1 - slot_ref[i])


def _expert_schedule(counts, padded, n_blocks, bm):
    pend = jnp.cumsum(padded)
    ids = jnp.arange(N_EXPERTS, dtype=jnp.int32)
    idx = jnp.arange(n_blocks, dtype=jnp.int32)
    n_used = (pend[-1] // bm).astype(jnp.int32)
    block_e = jnp.minimum(jnp.sum((pend[None, :] <= (idx * bm)[:, None]).astype(jnp.int32), axis=1), N_EXPERTS - 1)
    onehot = block_e[:, None] == ids[None, :]

    def per_block(table):
        return jnp.sum(jnp.where(onehot, table[None, :], 0), axis=1)

    start_blk = (pend - padded) // bm
    first = (idx == per_block(start_blk)) & (idx < n_used)
    last = (idx == per_block(pend // bm) - 1) & (idx < n_used)
    later = (ids[None, :] > ids[:, None]) & (padded[None, :] > 0)
    nxt_e = jnp.min(jnp.where(later, ids[None, :], N_EXPERTS), axis=1)
    nxt_e = jnp.where(nxt_e == N_EXPERTS, -1, nxt_e)
    order = jnp.cumsum((padded > 0).astype(jnp.int32)) - 1
    real_rows = per_block(counts) - (idx - per_block(start_blk)) * bm
    qrows = bm // MOE_QUARTERS
    quarters = jnp.clip((real_rows + qrows - 1) // qrows, 1, MOE_QUARTERS)
    as_i32 = lambda a: a.astype(jnp.int32)
    return (block_e, as_i32(first), as_i32(last), per_block(nxt_e), per_block(order) & 1, as_i32(quarters),
            n_used.reshape(1))


def _experts(schedule, xs, w_gate, b_gate, w_up, b_up, w_down, b_down):
    P, half = xs.shape
    bm = MOE_BM
    n_blocks = P // bm
    E, D, DE = w_gate.shape
    assert D == DE

    def row_map(i, be, first, last, nxt, slot, quarters, nu):
        return (jnp.minimum(i, nu[0] - 1), 0)

    def b_map(i, be, first, last, nxt, slot, quarters, nu):
        return (be[jnp.minimum(i, nu[0] - 1)], 0, 0)

    hbm = pl.BlockSpec(memory_space=pl.ANY)
    grid_spec = pltpu.PrefetchScalarGridSpec(
        num_scalar_prefetch=len(schedule),
        grid=(n_blocks,),
        in_specs=[pl.BlockSpec((bm, half), row_map),
                  hbm, pl.BlockSpec((1, 1, DE), b_map),
                  hbm, pl.BlockSpec((1, 1, DE), b_map),
                  hbm, pl.BlockSpec((1, 1, D), b_map)],
        out_specs=pl.BlockSpec((bm, half), row_map),
        scratch_shapes=[pltpu.VMEM((3, D, DE), F32),
                        pltpu.VMEM((2, 3, D, DE), BF16),
                        pltpu.SemaphoreType.DMA((3,))],
    )
    return pl.pallas_call(
        _expert_kernel,
        grid_spec=grid_spec,
        out_shape=jax.ShapeDtypeStruct((P, half), jnp.uint32),
        compiler_params=pltpu.CompilerParams(dimension_semantics=("arbitrary",), vmem_limit_bytes=VMEM_LIMIT),
        name="experts",
    )(*schedule, xs, w_gate, b_gate.reshape(E, 1, DE), w_up, b_up.reshape(E, 1, DE),
      w_down, b_down.reshape(E, 1, D))


def _combine_kernel(h1_ref, og_ref, probs_ref, mods_ref, g_ref, *rest):
    o_ref = rest[-1]
    tm = h1_ref.shape[1]
    probs = probs_ref[...]
    lane = lax.broadcasted_iota(jnp.int32, (tm, LANES), 1)
    half = og_ref.shape[2]
    f_lo = jnp.zeros((tm, half), F32)
    f_hi = jnp.zeros((tm, half), F32)
    for kk in range(TOP_K):
        pk = jnp.sum(jnp.where(lane == kk, probs, 0.0), axis=-1, keepdims=True)
        o_lo, o_hi = _unpack_rows(og_ref[kk])
        f_lo = f_lo + pk * o_lo
        f_hi = f_hi + pk * o_hi
    f = jnp.concatenate([f_lo, f_hi], axis=1)
    gate2 = mods_ref[0][5:6]
    o_ref[0] = h1_ref[0] + gate2 * (_rms(f) * g_ref[...])


def _combine(b0, b_total, h1, og, probs, mods, g_post_ffn, out_so_far):
    B, S, D = h1.shape
    tm = CMB_TM
    nj = S // tm
    in_specs = [pl.BlockSpec((1, tm, D), lambda b, j: (b, j, 0)),
                pl.BlockSpec((TOP_K, tm, D // 2), lambda b, j: (0, b * nj + j, 0)),
                pl.BlockSpec((tm, LANES), lambda b, j: (b * nj + j, 0)),
                pl.BlockSpec((1, 6, D), lambda b, j: (b + b0, 0, 0)),
                pl.BlockSpec((1, D), lambda b, j: (0, 0))]
    args = [h1, og, probs, mods, g_post_ffn]
    aliases = {}
    if out_so_far is not None:
        in_specs.append(pl.BlockSpec(memory_space=pl.ANY))
        args.append(out_so_far)
        aliases = {len(args) - 1: 0}
    return pl.pallas_call(
        _combine_kernel,
        grid=(B, nj),
        in_specs=in_specs,
        out_specs=pl.BlockSpec((1, tm, D), lambda b, j: (b + b0, j, 0)),
        out_shape=jax.ShapeDtypeStruct((b_total, S, D), F32),
        input_output_aliases=aliases,
        compiler_params=pltpu.CompilerParams(dimension_semantics=("arbitrary", "arbitrary"),
                                             vmem_limit_bytes=VMEM_LIMIT),
        name="combine",
    )(*args)


def _layer(h, c_pad, w_ada, b_ada, g_pre_mix, w_in, g_sgu_v, w_s, b_s, w_gk2, b_gk, g_gla_out, w_out, g_post_mix,
           g_pre_ffn, w_router, b_router, w_gate, b_gate, w_up, b_up, w_down, b_down, g_post_ffn):
    B, S, D = h.shape

    mods = _ada(c_pad, w_ada, b_ada.reshape(1, -1))[:B].reshape(B, 6, D)

    w_in_p = jnp.pad(w_in, ((0, 0), (0, D_IN_PAD - w_in.shape[1]))).astype(BF16)
    head_of = jnp.arange(D_SGU) // SGU_HEAD_DIM
    gmat = jnp.where(head_of[:, None] == head_of[None, :], 1.0 / SGU_HEAD_DIM, 0.0).astype(BF16)
    ws_cat = w_s.reshape(SGU_HEADS // 2, 2, SGU_CHUNK, SGU_CHUNK).transpose(0, 2, 1, 3).reshape(
        SGU_HEADS // 2, SGU_CHUNK, 2 * SGU_CHUNK)
    bs_exp = jnp.repeat(b_s.T, SGU_HEAD_DIM, axis=1)
    wgk_p = jnp.pad(w_gk2, ((0, LANES - GLA_GATE_RANK), (0, 0))).astype(BF16)
    w_r_p = jnp.pad(w_router, ((0, 0), (0, LANES - N_EXPERTS))).astype(BF16)
    b_r_p = jnp.concatenate([b_router, jnp.full((LANES - N_EXPERTS,), ROUTER_PAD, F32)]).reshape(1, LANES)

    w_out_b = w_out.astype(BF16)

    last = max(1, B // LAST_GROUP_FRACTION)
    group_sizes = [B - last, last] if B > 1 else [B]
    g_post = g_post_ffn.reshape(1, D)
    result = None
    b0 = 0
    for bg in group_sizes:
        tg = bg * S
        h1, n2, route, probs, cnt = _mixer(
            b0, bg, h, mods, g_pre_mix.reshape(1, D), w_in_p, gmat, g_sgu_v.reshape(1, D_SGU), ws_cat, bs_exp, wgk_p,
            b_gk.reshape(1, GLA_DK), g_gla_out.reshape(1, GLA_HEAD_V), w_out_b, g_post_mix.reshape(1, D),
            g_pre_ffn.reshape(1, D), w_r_p, b_r_p)

        bm = MOE_BM
        counts = cnt[0, :N_EXPERTS].astype(jnp.int32)
        padded = ((counts + bm - 1) // bm) * bm
        pend = jnp.cumsum(padded)
        pstart = pend - padded
        e_kt, rank_kt = route[:TOP_K], route[TOP_K:]
        onehot = e_kt[:, :, None] == jnp.arange(N_EXPERTS, dtype=jnp.int32)[None, None, :]
        dest_kt = rank_kt + jnp.sum(jnp.where(onehot, pstart[None, None, :], 0), axis=-1)
        n_blocks = tg * TOP_K // bm + N_EXPERTS
        schedule = _expert_schedule(counts, padded, n_blocks, bm)

        xs = _sc_dispatch(n2, dest_kt, n_blocks * bm)
        out = _experts(schedule, xs, w_gate, b_gate, w_up, b_up, w_down, b_down)
        og = _sc_gather(out, dest_kt).reshape(TOP_K, tg, D // 2)
        result = _combine(b0, B, h1, og, probs, mods, g_post, result)
        b0 += bg
    return result


def kernel(x, c, w_ada, b_ada, g_pre_mix, w_in, g_sgu_v, w_s, b_s, w_gk2, b_gk, g_gla_out, w_out, g_post_mix, g_pre_ffn, w_router, b_router, w_gate, b_gate, w_up, b_up, w_down, b_down, g_post_ffn):
    B = x.shape[0]
    c_pad = jnp.pad(c, ((0, -B % 8), (0, 0)))
    h = x
    for l in range(w_ada.shape[0]):
        h = _layer(h, c_pad, w_ada[l], b_ada[l], g_pre_mix[l], w_in[l], g_sgu_v[l], w_s[l], b_s[l], w_gk2[l], b_gk[l],
                   g_gla_out[l], w_out[l], g_post_mix[l], g_pre_ffn[l], w_router[l], b_router[l], w_gate[l], b_gate[l],
                   w_up[l], b_up[l], w_down[l], b_down[l], g_post_ffn[l])
    return h
```

```python
import functools

import jax
import jax.numpy as jnp
from jax import lax
from jax.experimental import pallas as pl
from jax.experimental.pallas import tpu as pltpu
from jax.experimental.pallas import tpu_sc as plsc

F32 = jnp.float32
BF16 = jnp.bfloat16

D_MODEL = 1024
D_SGU = 512
SGU_HEADS = 8
SGU_HEAD_DIM = 64
SGU_CHUNK = 128
D_GLA = 512
GLA_HEADS = 4
GLA_DK = 256
GLA_HEAD_K = 64
GLA_HEAD_V = 128
GLA_GATE_RANK = 16
GLA_TAU = 16.0
GLA_CHUNK = 64
N_EXPERTS = 32
TOP_K = 4
SWIGLU_LIMIT = 7.0
SWIGLU_ALPHA = 1.702
EPS = 1e-6

LANES = 128
D_IN_MAIN = 2 * D_SGU + 2 * GLA_DK + 2 * D_GLA
D_IN_PAD = D_IN_MAIN + LANES

MIX_TM = 1024
MIX_TILE = 512
MIX_SKEW = 1
MIX_SUB = 256
MOE_BM = 1024
MOE_QUARTERS = 4
CMB_TM = 1024
LAST_GROUP_FRACTION = 2
SC_CORES = 2
SC_SUBCORES = 16
SC_WORKERS = SC_CORES * SC_SUBCORES
SC_CH = 64
ADA_TN = 1536
ROUTER_PAD = -jnp.inf

VMEM_LIMIT = 56 * 1024 * 1024


def _ada_kernel(c_ref, w_ref, b_ref, o_ref):
    c = c_ref[...]
    sc = c * jax.nn.sigmoid(c)
    o_ref[...] = jnp.dot(sc.astype(BF16), w_ref[...].astype(BF16), preferred_element_type=F32) + b_ref[...]


def _ada(c_pad, w_ada, b_ada):
    rows, d = c_pad.shape
    n = w_ada.shape[1]
    tn = ADA_TN
    return pl.pallas_call(
        _ada_kernel,
        grid=(n // tn,),
        in_specs=[pl.BlockSpec((rows, d), lambda j: (0, 0)),
                  pl.BlockSpec((d, tn), lambda j: (0, j)),
                  pl.BlockSpec((1, tn), lambda j: (0, j))],
        out_specs=pl.BlockSpec((rows, tn), lambda j: (0, j)),
        out_shape=jax.ShapeDtypeStruct((rows, n), F32),
        compiler_params=pltpu.CompilerParams(dimension_semantics=("arbitrary",), vmem_limit_bytes=VMEM_LIMIT),
        name="ada",
    )(c_pad, w_ada, b_ada)


def _rms(x):
    return x * lax.rsqrt(jnp.mean(x * x, axis=-1, keepdims=True) + EPS)


def _gelu_tanh(x):
    return 0.5 * x * (1.0 + jnp.tanh(0.7978845608028654 * (x + 0.044715 * (x * x * x))))


def _log_sigmoid(z):
    return jnp.minimum(z, 0.0) - jnp.log(1.0 + jnp.exp(-jnp.abs(z)))


def _dot(a, b):
    return jnp.dot(a, b, preferred_element_type=F32)


def _dot_nt(a, b):
    return lax.dot_general(a, b, (((1,), (1,)), ((), ())), preferred_element_type=F32)


def _dot_tn(a, b):
    return lax.dot_general(a, b, (((0,), (0,)), ((), ())), preferred_element_type=F32)


def _pack_rows(x):
    n = x.shape[1] // 2
    lo = lax.bitcast_convert_type(x[:, :n].astype(BF16).astype(F32), jnp.uint32)
    hi = lax.bitcast_convert_type(x[:, n:].astype(BF16).astype(F32), jnp.uint32)
    return (lo >> 16) | (hi & jnp.uint32(0xFFFF0000))


def _unpack_rows(p):
    lo = lax.bitcast_convert_type(p << 16, F32)
    hi = lax.bitcast_convert_type(p & jnp.uint32(0xFFFF0000), F32)
    return lo, hi


def _split_dot(l_bf, a):
    hi = a.astype(BF16)
    lo = (a - hi.astype(F32)).astype(BF16)
    return _dot(l_bf, hi) + _dot(l_bf, lo)


def _mixer_kernel(x_ref, mods_ref, g_pre_ref, w_in_ref, gmat_ref, g_sgu_ref, ws_ref, bs_ref, wgk_ref, bgk_ref,
                  g_gla_ref, w_out_ref, g_post_ref, g_ffn_ref, w_r_ref, b_r_ref,
                  h1_ref, n2_ref, route_ref, probs_ref, cnt_ref,
                  st_ref, carry_ref):
    b = pl.program_id(0)
    j = pl.program_id(1)

    @pl.when(j == 0)
    def _():
        st_ref[...] = jnp.zeros_like(st_ref)

    @pl.when((b == 0) & (j == 0))
    def _():
        carry_ref[...] = jnp.zeros_like(carry_ref)

    carry_box = [carry_ref[...]]
    tiles = [_mixer_tile(slice(s * MIX_TILE, (s + 1) * MIX_TILE), carry_box,
                         x_ref, mods_ref, g_pre_ref, w_in_ref, gmat_ref, g_sgu_ref, ws_ref, bs_ref, wgk_ref, bgk_ref,
                         g_gla_ref, w_out_ref, g_post_ref, g_ffn_ref, w_r_ref, b_r_ref,
                         h1_ref, n2_ref, route_ref, probs_ref, st_ref)
             for s in range(x_ref.shape[1] // MIX_TILE)]
    alive = [True] * len(tiles)
    t = 0
    while any(alive):
        for s, tile in enumerate(tiles):
            if alive[s] and t >= s * MIX_SKEW:
                alive[s] = next(tile, None) is not None
        t += 1
    carry_ref[...] = carry_box[0]
    cnt_ref[...] = jnp.broadcast_to(carry_box[0], cnt_ref.shape)


def _mixer_tile(rows, carry_box, x_ref, mods_ref, g_pre_ref, w_in_ref, gmat_ref, g_sgu_ref, ws_ref, bs_ref, wgk_ref,
                bgk_ref, g_gla_ref, w_out_ref, g_post_ref, g_ffn_ref, w_r_ref, b_r_ref,
                h1_ref, n2_ref, route_ref, probs_ref, st_ref):
    tm = rows.stop - rows.start
    mods = mods_ref[0]
    shift1, scale1, gate1 = mods[0:1], mods[1:2], mods[2:3]
    shift2, scale2 = mods[3:4], mods[4:5]

    x = x_ref[0, rows, :]
    n = _rms(x) * (g_pre_ref[...] * (1.0 + scale1)) + shift1
    nb = n.astype(BF16)
    yield True

    pu = _dot(nb, w_in_ref[:, 0:D_SGU])
    pv = _dot(nb, w_in_ref[:, D_SGU:2 * D_SGU])
    yield True

    u = _gelu_tanh(pu)
    v = _gelu_tanh(pv)
    msv = _dot((v * v).astype(BF16), gmat_ref[...])
    vh = v * lax.rsqrt(msv + EPS) * g_sgu_ref[...]
    yield True

    qk = _dot(nb, w_in_ref[:, 2 * D_SGU:2 * D_SGU + 2 * GLA_DK])
    vv = _dot(nb, w_in_ref[:, 2 * D_SGU + 2 * GLA_DK:2 * D_SGU + 2 * GLA_DK + D_GLA]).astype(BF16)
    r = _dot(nb, w_in_ref[:, D_IN_MAIN - D_GLA:D_IN_MAIN])
    g_low = _dot(nb, w_in_ref[:, D_IN_MAIN:D_IN_PAD])
    yield True

    n_sc = tm // SGU_CHUNK
    lane_c = lax.broadcasted_iota(jnp.int32, (SGU_CHUNK, LANES), 1)
    low_half = lane_c < SGU_HEAD_DIM
    w_row = lax.broadcasted_iota(jnp.int32, (SGU_CHUNK, 2 * SGU_CHUNK), 0)
    w_col = lax.broadcasted_iota(jnp.int32, (SGU_CHUNK, 2 * SGU_CHUNK), 1) & (SGU_CHUNK - 1)
    a_cols = []
    for p in range(SGU_HEADS // 2):
        wcat = jnp.where(w_row >= w_col, ws_ref[p], 0.0).astype(BF16)
        rhs = []
        for ci in range(n_sc):
            vp = vh[ci * SGU_CHUNK:(ci + 1) * SGU_CHUNK, p * LANES:(p + 1) * LANES]
            rhs.append(jnp.concatenate([jnp.where(low_half, vp, 0.0), jnp.where(low_half, 0.0, vp)],
                                       axis=0).astype(BF16))
        res = _dot(wcat, jnp.concatenate(rhs, axis=1))
        bias = bs_ref[:, p * LANES:(p + 1) * LANES]
        a_cols.append(jnp.concatenate(
            [res[:, ci * LANES:(ci + 1) * LANES] + bias for ci in range(n_sc)], axis=0))
    a_out = u * jnp.concatenate(a_cols, axis=1)

    q = qk[:, :GLA_DK] * (GLA_HEAD_K ** -0.5)
    k = qk[:, GLA_DK:]
    z = _dot(g_low.astype(BF16), wgk_ref[...]) + bgk_ref[...]
    log_a = _log_sigmoid(z) * (1.0 / GLA_TAU)

    sub = min(tm, MIX_SUB)
    subs = [slice(s * sub, (s + 1) * sub) for s in range(tm // sub)]
    t_row = lax.broadcasted_iota(jnp.int32, (sub, sub), 0)
    t_col = lax.broadcasted_iota(jnp.int32, (sub, sub), 1)
    same_chunk = (t_row >> 6) == (t_col >> 6)
    l_cum = jnp.where(same_chunk & (t_row >= t_col), 1.0, 0.0).astype(BF16)
    n_gc = tm // GLA_CHUNK
    bcum = jnp.concatenate([_split_dot(l_cum, log_a[s]) for s in subs], axis=0)
    blast = jnp.concatenate(
        [jnp.broadcast_to(bcum[(ci + 1) * GLA_CHUNK - 1:(ci + 1) * GLA_CHUNK], (GLA_CHUNK, GLA_DK)) for ci in range(n_gc)],
        axis=0)
    q_s = q * jnp.exp(bcum)
    k_s = k * jnp.exp(-bcum)
    k_dec = (k * jnp.exp(blast - bcum)).astype(BF16)
    in_chunk = jnp.where((lax.broadcasted_iota(jnp.int32, (tm, LANES), 0) >> 6)
                         == lax.broadcasted_iota(jnp.int32, (tm, LANES), 1), 1.0, 0.0).astype(BF16)
    la_hi = log_a.astype(BF16)
    la_lo = (log_a - la_hi.astype(F32)).astype(BF16)
    dcol_t = jnp.exp(_dot_tn(la_hi, in_chunk) + _dot_tn(la_lo, in_chunk))
    yield True

    low_k = lax.broadcasted_iota(jnp.int32, (GLA_CHUNK, LANES), 1) < GLA_HEAD_K
    c_row = lax.broadcasted_iota(jnp.int32, (GLA_CHUNK, LANES), 0)
    c_col = lax.broadcasted_iota(jnp.int32, (GLA_CHUNK, LANES), 1) & (GLA_CHUNK - 1)
    causal2 = c_row >= c_col
    own_block = ((lax.broadcasted_iota(jnp.int32, (LANES, 2 * GLA_HEAD_V), 0) < GLA_HEAD_K)
                 == (lax.broadcasted_iota(jnp.int32, (LANES, 2 * GLA_HEAD_V), 1) < GLA_HEAD_V))
    zeros_v = jnp.zeros((GLA_CHUNK, GLA_HEAD_V), BF16)
    g_gla = g_gla_ref[...]
    o_rows = []
    for ci in range(n_gc):
        rs = slice(ci * GLA_CHUNK, (ci + 1) * GLA_CHUNK)
        o_heads = []
        for p in range(GLA_HEADS // 2):
            pl_ = slice(p * LANES, (p + 1) * LANES)
            pv_ = slice(2 * p * GLA_HEAD_V, 2 * (p + 1) * GLA_HEAD_V)
            qp = q_s[rs, pl_].astype(BF16)
            ksp = k_s[rs, pl_]
            k2 = jnp.concatenate([jnp.where(low_k, ksp, 0.0), jnp.where(low_k, 0.0, ksp)], axis=0).astype(BF16)
            scores2 = jnp.where(causal2, _dot_nt(qp, k2), 0.0).astype(BF16)
            v0 = vv[rs, pv_][:, :GLA_HEAD_V]
            v1 = vv[rs, pv_][:, GLA_HEAD_V:]
            v2 = jnp.concatenate([jnp.concatenate([v0, zeros_v], axis=1),
                                  jnp.concatenate([zeros_v, v1], axis=1)], axis=0)
            st = st_ref[p]
            o2 = _dot(jnp.concatenate([scores2, qp], axis=1), jnp.concatenate([v2, st.astype(BF16)], axis=0))
            upd = _dot_tn(k_dec[rs, pl_], vv[rs, pv_])
            st_ref[p] = st * dcol_t[pl_, ci:ci + 1] + jnp.where(own_block, upd, 0.0)
            o_heads.append(_rms(o2[:, :GLA_HEAD_V]) * g_gla)
            o_heads.append(_rms(o2[:, GLA_HEAD_V:]) * g_gla)
        o_rows.append(jnp.concatenate(o_heads, axis=1))
    o = jnp.concatenate(o_rows, axis=0) * (r * jax.nn.sigmoid(r))
    mix_in = jnp.concatenate([a_out, o], axis=1).astype(BF16)
    yield True

    mix = _dot(mix_in, w_out_ref[...])
    yield True

    h1 = x + gate1 * (_rms(mix) * g_post_ref[...])
    h1_ref[0, rows, :] = h1
    n2 = _rms(h1) * (g_ffn_ref[...] * (1.0 + scale2)) + shift2
    n2_ref[rows, :] = _pack_rows(n2)
    n2b = n2.astype(BF16)
    yield True

    logits = _dot(n2b, w_r_ref[...]) + b_r_ref[...]
    lane = lax.broadcasted_iota(jnp.int32, (tm, LANES), 1)
    lane_f = lane.astype(F32)
    vals = logits
    sels, tops, idxs = [], [], []
    for _ in range(TOP_K):
        m = jnp.max(vals, axis=-1, keepdims=True)
        idx = jnp.min(jnp.where(vals == m, lane_f, float(LANES)), axis=-1, keepdims=True)
        sel = lane_f == idx
        sels.append(sel)
        tops.append(m)
        idxs.append(idx)
        vals = jnp.where(sel, -jnp.inf, vals)
    es = [jnp.exp(t - tops[0]) for t in tops]
    inv = 1.0 / (es[0] + es[1] + es[2] + es[3])

    multi = jnp.zeros((tm, LANES), F32)
    for sel in sels:
        multi = multi + jnp.where(sel, 1.0, 0.0)
    l_strict = jnp.where(t_row > t_col, 1.0, 0.0).astype(BF16)
    carry = carry_box[0]
    before = []
    for s in subs:
        before.append(_dot(l_strict, multi[s].astype(BF16)) + carry)
        carry = carry + jnp.sum(multi[s], axis=0, keepdims=True)
    before = jnp.concatenate(before, axis=0)
    carry_box[0] = carry

    route = jnp.zeros((tm, LANES), F32)
    probs = jnp.zeros((tm, LANES), F32)
    for kk in range(TOP_K):
        rank_kk = jnp.sum(jnp.where(sels[kk], before, 0.0), axis=-1, keepdims=True)
        route = jnp.where(lane == kk, idxs[kk], route)
        route = jnp.where(lane == TOP_K + kk, rank_kk, route)
        probs = jnp.where(lane == kk, es[kk] * inv, probs)
    route_ref[:, rows] = route.T[0:2 * TOP_K, :].astype(jnp.int32)
    probs_ref[rows, :] = probs


def _const_spec(shape):
    return pl.BlockSpec(shape, lambda b, j: (0,) * len(shape))


def _mixer(b0, B, x, mods, g_pre, w_in_p, gmat, g_sgu, ws_cat, bs_exp, wgk_p, bgk, g_gla, w_out_b, g_post, g_ffn, w_r_p,
           b_r_p):
    _, S, D = x.shape
    T = B * S
    tm = MIX_TM
    nj = S // tm
    return pl.pallas_call(
        _mixer_kernel,
        grid=(B, nj),
        in_specs=[pl.BlockSpec((1, tm, D), lambda b, j: (b + b0, j, 0)),
                  pl.BlockSpec((1, 6, D), lambda b, j: (b + b0, 0, 0)),
                  _const_spec((1, D)), _const_spec(w_in_p.shape), _const_spec(gmat.shape), _const_spec((1, D_SGU)),
                  _const_spec(ws_cat.shape), _const_spec(bs_exp.shape), _const_spec(wgk_p.shape),
                  _const_spec((1, GLA_DK)), _const_spec((1, GLA_HEAD_V)), _const_spec(w_out_b.shape),
                  _const_spec((1, D)), _const_spec((1, D)), _const_spec(w_r_p.shape), _const_spec((1, LANES))],
        out_specs=[pl.BlockSpec((1, tm, D), lambda b, j: (b, j, 0)),
                   pl.BlockSpec((tm, D // 2), lambda b, j: (b * nj + j, 0)),
                   pl.BlockSpec((2 * TOP_K, tm), lambda b, j: (0, b * nj + j)),
                   pl.BlockSpec((tm, LANES), lambda b, j: (b * nj + j, 0)),
                   pl.BlockSpec((8, LANES), lambda b, j: (0, 0))],
        out_shape=[jax.ShapeDtypeStruct((B, S, D), F32),
                   jax.ShapeDtypeStruct((T, D // 2), jnp.uint32),
                   jax.ShapeDtypeStruct((2 * TOP_K, T), jnp.int32),
                   jax.ShapeDtypeStruct((T, LANES), F32),
                   jax.ShapeDtypeStruct((8, LANES), F32)],
        scratch_shapes=[pltpu.VMEM((GLA_HEADS // 2, 2 * GLA_HEAD_K, 2 * GLA_HEAD_V), F32),
                        pltpu.VMEM((1, LANES), F32)],
        compiler_params=pltpu.CompilerParams(dimension_semantics=("arbitrary", "arbitrary"),
                                             vmem_limit_bytes=VMEM_LIMIT),
        name="mixer",
    )(x, mods, g_pre, w_in_p, gmat, g_sgu, ws_cat, bs_exp, wgk_p, bgk, g_gla, w_out_b, g_post, g_ffn, w_r_p, b_r_p)


def _sc_dispatch(rows, dest_kt, n_out):
    T, D = rows.shape
    top_k = dest_kt.shape[0]
    ch = SC_CH
    cpw = T // ch // SC_WORKERS
    assert T % (ch * SC_WORKERS) == 0
    mesh = plsc.VectorSubcoreMesh(core_axis_name="c", subcore_axis_name="s")

    @functools.partial(
        pl.kernel, mesh=mesh,
        out_type=jax.ShapeDtypeStruct((n_out, D), rows.dtype),
        scratch_types=[pltpu.VMEM((top_k, ch), jnp.int32), pltpu.VMEM((ch, D), rows.dtype)],
    )
    def k(rows_hbm, dest_hbm, out_hbm, idx_v, rows_v):
        wid = lax.axis_index("s") * SC_CORES + lax.axis_index("c")

        @pl.loop(0, cpw)
        def _(i):
            t0 = (wid * cpw + i) * ch
            for kk in range(top_k):
                pltpu.sync_copy(dest_hbm.at[kk, pl.ds(t0, ch)], idx_v.at[kk])
            pltpu.sync_copy(rows_hbm.at[pl.ds(t0, ch)], rows_v)
            for kk in range(top_k):
                pltpu.sync_copy(rows_v, out_hbm.at[idx_v.at[kk]])

    return k(rows, dest_kt)


def _sc_gather(src, dest_kt):
    _, D = src.shape
    top_k, T = dest_kt.shape
    ch = SC_CH
    cpk = T // ch
    cpw = top_k * cpk // SC_WORKERS
    mesh = plsc.VectorSubcoreMesh(core_axis_name="c", subcore_axis_name="s")
    assert T % ch == 0 and (top_k * cpk) % (2 * SC_WORKERS) == 0
    n_pairs = cpw // 2

    @functools.partial(
        pl.kernel, mesh=mesh,
        out_type=jax.ShapeDtypeStruct((top_k * T, D), src.dtype),
        scratch_types=[pltpu.VMEM((2, ch), jnp.int32), pltpu.VMEM((2, ch, D), src.dtype),
                       pltpu.SemaphoreType.DMA, pltpu.SemaphoreType.DMA,
                       pltpu.SemaphoreType.DMA, pltpu.SemaphoreType.DMA],
    )
    def k(src_hbm, idx_hbm, out_hbm, idx_v, rows_v, gsem0, gsem1, wsem0, wsem1):
        wid = lax.axis_index("s") * SC_CORES + lax.axis_index("c")
        gsem = (gsem0, gsem1)
        wsem = (wsem0, wsem1)

        def gather_copy(b):
            return pltpu.make_async_copy(src_hbm.at[idx_v.at[b]], rows_v.at[b], gsem[b])

        def write_copy(c, b):
            return pltpu.make_async_copy(rows_v.at[b], out_hbm.at[pl.ds(c * ch, ch)], wsem[b])

        def start_gather(c, b):
            kk = c // cpk
            t0 = (c - kk * cpk) * ch
            pltpu.sync_copy(idx_hbm.at[kk, pl.ds(t0, ch)], idx_v.at[b])
            gather_copy(b).start()

        c_first = wid * cpw
        start_gather(c_first, 0)

        @pl.loop(0, n_pairs)
        def _(j):
            c0 = c_first + 2 * j
            c1 = c0 + 1

            @pl.when(j > 0)
            def _():
                write_copy(c0 - 1, 1).wait()

            start_gather(c1, 1)
            gather_copy(0).wait()
            write_copy(c0, 0).start()
            write_copy(c0, 0).wait()

            @pl.when(j < n_pairs - 1)
            def _():
                start_gather(c0 + 2, 0)

            gather_copy(1).wait()
            write_copy(c1, 1).start()

        write_copy(c_first + cpw - 1, 1).wait()

    return k(src, dest_kt)


def _expert_kernel(be_ref, first_ref, last_ref, nxt_ref, slot_ref, quarters_ref, nu_ref,
                   x_ref, wg_hbm, bg_ref, wu_hbm, bu_ref, wd_hbm, bd_ref, o_ref,
                   w_ref, sem):
    i = pl.program_id(0)
    active = i < nu_ref[0]
    has_next = nxt_ref[i] >= 0

    def weight_copies(e, slot):
        return [pltpu.make_async_copy(w.at[e], w_ref.at[slot, m], sem.at[slot, m])
                for m, w in enumerate((wg_hbm, wu_hbm, wd_hbm))]

    @pl.when(i == 0)
    def _():
        for cp in weight_copies(be_ref[0], slot_ref[0]):
            cp.start()
        for cp in weight_copies(be_ref[0], slot_ref[0]):
            cp.wait()

    @pl.when(active & (first_ref[i] == 1) & has_next)
    def _():
        for cp in weight_copies(nxt_ref[i], 1 - slot_ref[i]):
            cp.start()

    def ffn(rows):
        slot = slot_ref[i]
        x_lo, x_hi = _unpack_rows(x_ref[rows, :])
        xf = jnp.concatenate([x_lo, x_hi], axis=1)
        g = jnp.minimum(_dot(xf, w_ref[slot, 0]) + bg_ref[0], SWIGLU_LIMIT)
        u = jnp.clip(_dot(xf, w_ref[slot, 1]) + bu_ref[0], -SWIGLU_LIMIT, SWIGLU_LIMIT)
        hdn = (u + 1.0) * (g * jax.nn.sigmoid(SWIGLU_ALPHA * g))
        o_ref[rows, :] = _pack_rows(_dot(hdn, w_ref[slot, 2]) + bd_ref[0])

    bm = x_ref.shape[0]
    for nq in range(1, MOE_QUARTERS + 1):
        @pl.when(active & (quarters_ref[i] == nq))
        def _(nq=nq):
            ffn(slice(0, nq * bm // MOE_QUARTERS))

    @pl.when(active & (last_ref[i] == 1) & has_next)
    def _():
        for cp in weight_copies(nxt_ref[i], 1 - slot_ref[i]):
            cp.wait()


def _expert_schedule(counts, padded, n_blocks, bm):
    pend = jnp.cumsum(padded)
    ids = jnp.arange(N_EXPERTS, dtype=jnp.int32)
    idx = jnp.arange(n_blocks, dtype=jnp.int32)
    n_used = (pend[-1] // bm).astype(jnp.int32)
    block_e = jnp.minimum(jnp.sum((pend[None, :] <= (idx * bm)[:, None]).astype(jnp.int32), axis=1), N_EXPERTS - 1)
    onehot = block_e[:, None] == ids[None, :]

    def per_block(table):
        return jnp.sum(jnp.where(onehot, table[None, :], 0), axis=1)

    start_blk = (pend - padded) // bm
    first = (idx == per_block(start_blk)) & (idx < n_used)
    last = (idx == per_block(pend // bm) - 1) & (idx < n_used)
    later = (ids[None, :] > ids[:, None]) & (padded[None, :] > 0)
    nxt_e = jnp.min(jnp.where(later, ids[None, :], N_EXPERTS), axis=1)
    nxt_e = jnp.where(nxt_e == N_EXPERTS, -1, nxt_e)
    order = jnp.cumsum((padded > 0).astype(jnp.int32)) - 1
    real_rows = per_block(counts) - (idx - per_block(start_blk)) * bm
    qrows = bm // MOE_QUARTERS
    quarters = jnp.clip((real_rows + qrows - 1) // qrows, 1, MOE_QUARTERS)
    as_i32 = lambda a: a.astype(jnp.int32)
    return (block_e, as_i32(first), as_i32(last), per_block(nxt_e), per_block(order) & 1, as_i32(quarters),
            n_used.reshape(1))


def _experts(schedule, xs, w_gate, b_gate, w_up, b_up, w_down, b_down):
    P, half = xs.shape
    bm = MOE_BM
    n_blocks = P // bm
    E, D, DE = w_gate.shape
    assert D == DE

    def row_map(i, be, first, last, nxt, slot, quarters, nu):
        return (jnp.minimum(i, nu[0] - 1), 0)

    def b_map(i, be, first, last, nxt, slot, quarters, nu):
        return (be[jnp.minimum(i, nu[0] - 1)], 0, 0)

    hbm = pl.BlockSpec(memory_space=pl.ANY)
    grid_spec = pltpu.PrefetchScalarGridSpec(
        num_scalar_prefetch=len(schedule),
        grid=(n_blocks,),
        in_specs=[pl.BlockSpec((bm, half), row_map),
                  hbm, pl.BlockSpec((1, 1, DE), b_map),
                  hbm, pl.BlockSpec((1, 1, DE), b_map),
                  hbm, pl.BlockSpec((1, 1, D), b_map)],
        out_specs=pl.BlockSpec((bm, half), row_map),
        scratch_shapes=[pltpu.VMEM((2, 3, D, DE), F32),
                        pltpu.SemaphoreType.DMA((2, 3))],
    )
    return pl.pallas_call(
        _expert_kernel,
        grid_spec=grid_spec,
        out_shape=jax.ShapeDtypeStruct((P, half), jnp.uint32),
        compiler_params=pltpu.CompilerParams(dimension_semantics=("arbitrary",), vmem_limit_bytes=VMEM_LIMIT),
        name="experts",
    )(*schedule, xs, w_gate, b_gate.reshape(E, 1, DE), w_up, b_up.reshape(E, 1, DE),
      w_down, b_down.reshape(E, 1, D))


def _combine_kernel(h1_ref, og_ref, probs_ref, mods_ref, g_ref, *rest):
    o_ref = rest[-1]
    tm = h1_ref.shape[1]
    probs = probs_ref[...]
    lane = lax.broadcasted_iota(jnp.int32, (tm, LANES), 1)
    half = og_ref.shape[2]
    f_lo = jnp.zeros((tm, half), F32)
    f_hi = jnp.zeros((tm, half), F32)
    for kk in range(TOP_K):
        pk = jnp.sum(jnp.where(lane == kk, probs, 0.0), axis=-1, keepdims=True)
        o_lo, o_hi = _unpack_rows(og_ref[kk])
        f_lo = f_lo + pk * o_lo
        f_hi = f_hi + pk * o_hi
    f = jnp.concatenate([f_lo, f_hi], axis=1)
    gate2 = mods_ref[0][5:6]
    o_ref[0] = h1_ref[0] + gate2 * (_rms(f) * g_ref[...])


def _combine(b0, b_total, h1, og, probs, mods, g_post_ffn, out_so_far):
    B, S, D = h1.shape
    tm = CMB_TM
    nj = S // tm
    in_specs = [pl.BlockSpec((1, tm, D), lambda b, j: (b, j, 0)),
                pl.BlockSpec((TOP_K, tm, D // 2), lambda b, j: (0, b * nj + j, 0)),
                pl.BlockSpec((tm, LANES), lambda b, j: (b * nj + j, 0)),
                pl.BlockSpec((1, 6, D), lambda b, j: (b + b0, 0, 0)),
                pl.BlockSpec((1, D), lambda b, j: (0, 0))]
    args = [h1, og, probs, mods, g_post_ffn]
    aliases = {}
    if out_so_far is not None:
        in_specs.append(pl.BlockSpec(memory_space=pl.ANY))
        args.append(out_so_far)
        aliases = {len(args) - 1: 0}
    return pl.pallas_call(
        _combine_kernel,
        grid=(B, nj),
        in_specs=in_specs,
        out_specs=pl.BlockSpec((1, tm, D), lambda b, j: (b + b0, j, 0)),
        out_shape=jax.ShapeDtypeStruct((b_total, S, D), F32),
        input_output_aliases=aliases,
        compiler_params=pltpu.CompilerParams(dimension_semantics=("arbitrary", "arbitrary"),
                                             vmem_limit_bytes=VMEM_LIMIT),
        name="combine",
    )(*args)


def _layer(h, c_pad, w_ada, b_ada, g_pre_mix, w_in, g_sgu_v, w_s, b_s, w_gk2, b_gk, g_gla_out, w_out, g_post_mix,
           g_pre_ffn, w_router, b_router, w_gate, b_gate, w_up, b_up, w_down, b_down, g_post_ffn):
    B, S, D = h.shape

    mods = _ada(c_pad, w_ada, b_ada.reshape(1, -1))[:B].reshape(B, 6, D)

    w_in_p = jnp.pad(w_in, ((0, 0), (0, D_IN_PAD - w_in.shape[1]))).astype(BF16)
    head_of = jnp.arange(D_SGU) // SGU_HEAD_DIM
    gmat = jnp.where(head_of[:, None] == head_of[None, :], 1.0 / SGU_HEAD_DIM, 0.0).astype(BF16)
    ws_cat = w_s.reshape(SGU_HEADS // 2, 2, SGU_CHUNK, SGU_CHUNK).transpose(0, 2, 1, 3).reshape(
        SGU_HEADS // 2, SGU_CHUNK, 2 * SGU_CHUNK)
    bs_exp = jnp.repeat(b_s.T, SGU_HEAD_DIM, axis=1)
    wgk_p = jnp.pad(w_gk2, ((0, LANES - GLA_GATE_RANK), (0, 0))).astype(BF16)
    w_r_p = jnp.pad(w_router, ((0, 0), (0, LANES - N_EXPERTS))).astype(BF16)
    b_r_p = jnp.concatenate([b_router, jnp.full((LANES - N_EXPERTS,), ROUTER_PAD, F32)]).reshape(1, LANES)

    w_out_b = w_out.astype(BF16)

    last = max(1, B // LAST_GROUP_FRACTION)
    group_sizes = [B - last, last] if B > 1 else [B]
    g_post = g_post_ffn.reshape(1, D)
    result = None
    b0 = 0
    for bg in group_sizes:
        tg = bg * S
        h1, n2, route, probs, cnt = _mixer(
            b0, bg, h, mods, g_pre_mix.reshape(1, D), w_in_p, gmat, g_sgu_v.reshape(1, D_SGU), ws_cat, bs_exp, wgk_p,
            b_gk.reshape(1, GLA_DK), g_gla_out.reshape(1, GLA_HEAD_V), w_out_b, g_post_mix.reshape(1, D),
            g_pre_ffn.reshape(1, D), w_r_p, b_r_p)

        bm = MOE_BM
        counts = cnt[0, :N_EXPERTS].astype(jnp.int32)
        padded = ((counts + bm - 1) // bm) * bm
        pend = jnp.cumsum(padded)
        pstart = pend - padded
        e_kt, rank_kt = route[:TOP_K], route[TOP_K:]
        onehot = e_kt[:, :, None] == jnp.arange(N_EXPERTS, dtype=jnp.int32)[None, None, :]
        dest_kt = rank_kt + jnp.sum(jnp.where(onehot, pstart[None, None, :], 0), axis=-1)
        n_blocks = tg * TOP_K // bm + N_EXPERTS
        schedule = _expert_schedule(counts, padded, n_blocks, bm)

        xs = _sc_dispatch(n2, dest_kt, n_blocks * bm)
        out = _experts(schedule, xs, w_gate, b_gate, w_up, b_up, w_down, b_down)
        og = _sc_gather(out, dest_kt).reshape(TOP_K, tg, D // 2)
        result = _combine(b0, B, h1, og, probs, mods, g_post, result)
        b0 += bg
    return result


def kernel(x, c, w_ada, b_ada, g_pre_mix, w_in, g_sgu_v, w_s, b_s, w_gk2, b_gk, g_gla_out, w_out, g_post_mix, g_pre_ffn, w_router, b_router, w_gate, b_gate, w_up, b_up, w_down, b_down, g_post_ffn):
    B = x.shape[0]
    c_pad = jnp.pad(c, ((0, -B % 8), (0, 0)))
    h = x
    for l in range(w_ada.shape[0]):
        h = _layer(h, c_pad, w_ada[l], b_ada[l], g_pre_mix[l], w_in[l], g_sgu_v[l], w_s[l], b_s[l], w_gk2[l], b_gk[l],
                   g_gla_out[l], w_out[l], g_post_mix[l], g_pre_ffn[l], w_router[l], b_router[l], w_gate[l], b_gate[l],
                   w_up[l], b_up[l], w_down[l], b_down[l], g_post_ffn[l])
    return h
```
